```python
import math, functools
import jax, jax.numpy as jnp
from jax import lax
import numpy as np

D_MODEL = 1024
BATCH = 4
SEQ = 4096
DEPTH = 2
DEC_BATCH = 128
DEC_SEQ = 4
PAST_LEN = 2048
PAGE_SIZE = 128

RET_HEADS = 4
RET_DK = 128
RET_DV = 128
RET_WIDTH = RET_HEADS * RET_DV
RET_CHUNK = 128
MOBA_HEADS = 8
MOBA_DH = 64
MOBA_WIDTH = MOBA_HEADS * MOBA_DH
MOBA_BLOCK = 256
MOBA_TOPK = 3
MOBA_QBLOCK = 128
N_GROUPS = 4
EXPERTS_PER_GROUP = 8
N_EXPERTS = N_GROUPS * EXPERTS_PER_GROUP
EXPERT_TOPK = 2
EXPERT_HIDDEN = 128
ALPHA = (2 * DEPTH) ** 0.25
BETA = (8 * DEPTH) ** -0.25
LN_EPS = 1e-5
W_IN = 4 * RET_WIDTH + 3 * MOBA_WIDTH + 2 * D_MODEL

kernel_name = "hybrid_retention_moba_hmoe_step"

F32 = jnp.float32


def retention_gammas():
    return jnp.asarray(1.0 - 2.0 ** (-5.0 - np.arange(RET_HEADS)), dtype=F32)


def alibi_slopes():
    return jnp.asarray(2.0 ** (-8.0 * np.arange(1, MOBA_HEADS + 1) / MOBA_HEADS), dtype=F32)


def layer_norm(x, g, b):
    xf = x.astype(F32)
    mu = jnp.mean(xf, axis=-1, keepdims=True)
    var = jnp.mean(jnp.square(xf - mu), axis=-1, keepdims=True)
    return ((xf - mu) * lax.rsqrt(var + LN_EPS) * g + b).astype(x.dtype)


def retention_chunk(state, q, k, v):
    L = q.shape[2]
    log_g = jnp.log(retention_gammas())
    idx = jnp.arange(L, dtype=F32)
    diff = idx[:, None] - idx[None, :]
    decay = jnp.where(diff >= 0, jnp.exp(log_g[:, None, None] * jnp.maximum(diff, 0.0)), 0.0)
    scores = jnp.einsum('bhid,bhjd->bhij', q, k) * decay
    inner = jnp.einsum('bhij,bhjv->bhiv', scores, v)
    q_dec = q * jnp.exp(log_g[:, None] * (idx + 1.0))[:, :, None]
    cross = jnp.einsum('bhid,bhdv->bhiv', q_dec, state)
    k_dec = k * jnp.exp(log_g[:, None] * (L - 1.0 - idx))[:, :, None]
    new_state = state * jnp.exp(log_g * L)[:, None, None] + jnp.einsum('bhjd,bhjv->bhdv', k_dec, v)
    return new_state, inner + cross


def retention_prompt(q, k, v):
    B, H, S, dk = q.shape
    n_c = S // RET_CHUNK
    to_chunks = lambda t: t.reshape(B, H, n_c, RET_CHUNK, t.shape[-1]).transpose(2, 0, 1, 3, 4)
    s0 = jnp.zeros((B, H, dk, v.shape[-1]), F32)
    s_fin, o = lax.scan(lambda s, c: retention_chunk(s, *c), s0, (to_chunks(q), to_chunks(k), to_chunks(v)))
    o = o.transpose(1, 2, 0, 3, 4).reshape(B, H, S, v.shape[-1])
    return o, s_fin


def retention_sample(state, q, k, v):
    s_new, o = retention_chunk(state.astype(F32), q, k, v)
    return o, s_new


def moba_blocks(k, v):
    B, H, T, dh = k.shape
    nb = -(-T // MOBA_BLOCK)
    pad = ((0, 0), (0, 0), (0, nb * MOBA_BLOCK - T), (0, 0))
    k_blk = jnp.pad(k, pad).reshape(B, H, nb, MOBA_BLOCK, dh)
    v_blk = jnp.pad(v, pad).reshape(B, H, nb, MOBA_BLOCK, dh)
    k_mean = jnp.mean(k_blk.astype(F32), axis=3).astype(k.dtype)
    return k_blk, v_blk, k_mean


def moba_query_block(q, q_pos0, k_blk, v_blk, k_mean):
    B, H, Q, dh = q.shape
    nb = k_blk.shape[2]
    scale = dh ** -0.5
    slopes = alibi_slopes()[None, :, None, None]
    own = q_pos0 // MOBA_BLOCK
    q_pos = q_pos0 + jnp.arange(Q)
    gate = jnp.einsum('bhqd,bhnd->bhqn', q, k_mean).astype(F32)
    gate = jnp.where(jnp.arange(nb) < own, gate, -jnp.inf)
    n_sel = min(MOBA_TOPK, nb)
    gval, sel = lax.top_k(gate, n_sel)
    sel_ok = jnp.isfinite(gval)
    bi = jnp.arange(B)[:, None, None, None]
    hi = jnp.arange(H)[None, :, None, None]
    kg = k_blk[bi, hi, sel]
    vg = v_blk[bi, hi, sel]
    key_pos = sel[..., None] * MOBA_BLOCK + jnp.arange(MOBA_BLOCK)
    dist = (q_pos[None, None, :, None, None] - key_pos).astype(F32)
    l_sel = jnp.einsum('bhqd,bhqnkd->bhqnk', q, kg).astype(F32) * scale - slopes[..., None] * dist
    l_sel = jnp.where(sel_ok[..., None], l_sel, -jnp.inf).reshape(B, H, Q, n_sel * MOBA_BLOCK)
    k_own = lax.dynamic_index_in_dim(k_blk, own, axis=2, keepdims=False)
    v_own = lax.dynamic_index_in_dim(v_blk, own, axis=2, keepdims=False)
    own_pos = own * MOBA_BLOCK + jnp.arange(MOBA_BLOCK)
    dist_own = (q_pos[:, None] - own_pos[None, :]).astype(F32)
    l_own = jnp.einsum('bhqd,bhkd->bhqk', q, k_own).astype(F32) * scale - slopes * dist_own
    l_own = jnp.where(dist_own >= 0, l_own, -jnp.inf)
    p = jax.nn.softmax(jnp.concatenate([l_sel, l_own], axis=-1), axis=-1).astype(v_blk.dtype)
    p_sel = p[..., :n_sel * MOBA_BLOCK].reshape(B, H, Q, n_sel, MOBA_BLOCK)
    p_own = p[..., n_sel * MOBA_BLOCK:]
    return (jnp.einsum('bhqnk,bhqnkd->bhqd', p_sel, vg)
            + jnp.einsum('bhqk,bhkd->bhqd', p_own, v_own))


def moba_prompt(q, k, v):
    B, H, S, dh = q.shape
    k_blk, v_blk, k_mean = moba_blocks(k, v)
    n_q = S // MOBA_QBLOCK
    qc = q.reshape(B, H, n_q, MOBA_QBLOCK, dh).transpose(2, 0, 1, 3, 4)
    o = lax.map(lambda a: moba_query_block(a[1], a[0] * MOBA_QBLOCK, k_blk, v_blk, k_mean),
                (jnp.arange(n_q), qc))
    return o.transpose(1, 2, 0, 3, 4).reshape(B, H, S, dh)


def moba_sample(q, k, v, past_k, past_v):
    past = past_k.shape[1]
    k_all = jnp.concatenate([past_k.transpose(0, 2, 1, 3), k], axis=2)
    v_all = jnp.concatenate([past_v.transpose(0, 2, 1, 3), v], axis=2)
    k_blk, v_blk, k_mean = moba_blocks(k_all, v_all)
    return moba_query_block(q, past, k_blk, v_blk, k_mean)


def token_mixer(x, ret_fn, moba_fn, w_in, b_merge, gn_g, w_br_ret, w_br_moba, w_out):
    B, L, _ = x.shape
    h = jnp.einsum('bld,de->ble', x, w_in)
    cuts = [RET_WIDTH, 2 * RET_WIDTH, 3 * RET_WIDTH, 4 * RET_WIDTH,
            4 * RET_WIDTH + MOBA_WIDTH, 4 * RET_WIDTH + 2 * MOBA_WIDTH,
            4 * RET_WIDTH + 3 * MOBA_WIDTH, 4 * RET_WIDTH + 3 * MOBA_WIDTH + D_MODEL]
    rq, rk, rv, rg, mq, mk, mv, ga, gb = jnp.split(h, cuts, axis=-1)
    heads = lambda t, n, d: t.reshape(B, L, n, d).transpose(0, 2, 1, 3)
    o_r, ret_state = ret_fn(heads(rq, RET_HEADS, RET_DK).astype(F32),
                            heads(rk, RET_HEADS, RET_DK).astype(F32) * RET_DK ** -0.5,
                            heads(rv, RET_HEADS, RET_DV).astype(F32))
    mu = jnp.mean(o_r, axis=-1, keepdims=True)
    var = jnp.mean(jnp.square(o_r - mu), axis=-1, keepdims=True)
    o_r = ((o_r - mu) * lax.rsqrt(var + LN_EPS)).transpose(0, 2, 1, 3).reshape(B, L, RET_WIDTH) * gn_g
    o_r = (jax.nn.silu(rg.astype(F32)) * o_r).astype(x.dtype)
    mk_h = mk.reshape(B, L, MOBA_HEADS, MOBA_DH)
    mv_h = mv.reshape(B, L, MOBA_HEADS, MOBA_DH)
    o_m = moba_fn(heads(mq, MOBA_HEADS, MOBA_DH), mk_h.transpose(0, 2, 1, 3), mv_h.transpose(0, 2, 1, 3))
    o_m = o_m.transpose(0, 2, 1, 3).reshape(B, L, MOBA_WIDTH)
    gates = jax.nn.sigmoid(jnp.concatenate([ga, gb], axis=-1).astype(F32) + b_merge)
    merged = (gates[..., :D_MODEL] * jnp.einsum('blr,rd->bld', o_r, w_br_ret).astype(F32)
              + gates[..., D_MODEL:] * jnp.einsum('blm,md->bld', o_m, w_br_moba).astype(F32))
    y = jnp.einsum('bld,de->ble', merged.astype(x.dtype), w_out)
    return y, (mk_h, mv_h, ret_state)


def hier_moe(x, w_rg, b_rg, w_re, b_re, w_eg, w_eu, w_ed):
    B, L, D = x.shape
    t = x.reshape(B * L, D)
    g_logit = jnp.einsum('td,dg->tg', t, w_rg).astype(F32) + b_rg
    g_idx = jnp.argmax(g_logit, axis=-1)
    g_w = jnp.take_along_axis(jax.nn.softmax(g_logit, axis=-1), g_idx[:, None], axis=-1)[:, 0]
    e_logit = jnp.einsum('td,gde->tge', t, w_re).astype(F32) + b_re
    e_logit = jnp.take_along_axis(e_logit, g_idx[:, None, None], axis=1)[:, 0]
    e_val, e_idx = lax.top_k(e_logit, EXPERT_TOPK)
    e_w = jax.nn.softmax(e_val, axis=-1) * g_w[:, None]
    gid = g_idx[:, None] * EXPERTS_PER_GROUP + e_idx
    combine = jnp.sum(jax.nn.one_hot(gid, N_EXPERTS, dtype=F32) * e_w[..., None], axis=1)
    hid = (jax.nn.silu(jnp.einsum('td,edf->tef', t, w_eg).astype(F32))
           * jnp.einsum('td,edf->tef', t, w_eu).astype(F32))
    out = jnp.einsum('tef,efd->td', (hid * combine[:, :, None]).astype(x.dtype), w_ed)
    return out.reshape(B, L, D)


def decoder_layer(x, ret_fn, moba_fn, w_in, b_merge, gn_g, w_br_ret, w_br_moba, w_out, ln1_g, ln1_b,
                  w_rg, b_rg, w_re, b_re, w_eg, w_eu, w_ed, ln2_g, ln2_b):
    y, st = token_mixer(x, ret_fn, moba_fn, w_in, b_merge, gn_g, w_br_ret, w_br_moba, w_out)
    x = layer_norm(ALPHA * x + y, ln1_g, ln1_b)
    x = layer_norm(ALPHA * x + hier_moe(x, w_rg, b_rg, w_re, b_re, w_eg, w_eu, w_ed), ln2_g, ln2_b)
    return x, st


def setup_inputs(seed: int = 0) -> dict:
    key = jax.random.key(seed)
    ks = jax.random.split(key, 32)
    n = lambda i, shape, s: jax.random.normal(ks[i], shape, F32) * s
    n_pages = PAST_LEN // PAGE_SIZE
    used = DEC_BATCH * n_pages
    n_pool = used + max(1, used // 4)
    page_table = jax.random.permutation(ks[0], n_pool)[:used].reshape(DEC_BATCH, n_pages).astype(jnp.int32)
    col_scale = jnp.concatenate([
        jnp.ones((2 * RET_WIDTH,), F32), jnp.full((RET_WIDTH,), BETA, F32), jnp.ones((RET_WIDTH,), F32),
        jnp.ones((2 * MOBA_WIDTH,), F32), jnp.full((MOBA_WIDTH,), BETA, F32), jnp.ones((2 * D_MODEL,), F32)])
    return {
        "x_prompt": n(1, (BATCH, SEQ, D_MODEL), 1.0),
        "x_sample": n(2, (DEC_BATCH, DEC_SEQ, D_MODEL), 1.0),
        "cache_k": n(3, (DEPTH, n_pool, PAGE_SIZE, MOBA_HEADS, MOBA_DH), 1.0),
        "cache_v": n(4, (DEPTH, n_pool, PAGE_SIZE, MOBA_HEADS, MOBA_DH), 1.0),
        "state_ret": n(5, (DEPTH, DEC_BATCH, RET_HEADS, RET_DK, RET_DV), 1.0),
        "page_table": page_table,
        "w_in": n(6, (DEPTH, D_MODEL, W_IN), D_MODEL ** -0.5) * col_scale,
        "b_merge": n(7, (DEPTH, 2 * D_MODEL), 0.01),
        "gn_g": 1.0 + n(8, (DEPTH, RET_WIDTH), 0.01),
        "w_br_ret": n(9, (DEPTH, RET_WIDTH, D_MODEL), BETA * RET_WIDTH ** -0.5),
        "w_br_moba": n(10, (DEPTH, MOBA_WIDTH, D_MODEL), BETA * MOBA_WIDTH ** -0.5),
        "w_out": n(11, (DEPTH, D_MODEL, D_MODEL), BETA * D_MODEL ** -0.5),
        "ln1_g": 1.0 + n(12, (DEPTH, D_MODEL), 0.01),
        "ln1_b": n(13, (DEPTH, D_MODEL), 0.01),
        "w_router_group": n(14, (DEPTH, D_MODEL, N_GROUPS), D_MODEL ** -0.5),
        "b_router_group": n(15, (DEPTH, N_GROUPS), 0.01),
        "w_router_expert": n(16, (DEPTH, N_GROUPS, D_MODEL, EXPERTS_PER_GROUP), D_MODEL ** -0.5),
        "b_router_expert": n(17, (DEPTH, N_GROUPS, EXPERTS_PER_GROUP), 0.01),
        "w_exp_gate": n(18, (DEPTH, N_EXPERTS, D_MODEL, EXPERT_HIDDEN), D_MODEL ** -0.5),
        "w_exp_up": n(19, (DEPTH, N_EXPERTS, D_MODEL, EXPERT_HIDDEN), BETA * D_MODEL ** -0.5),
        "w_exp_down": n(20, (DEPTH, N_EXPERTS, EXPERT_HIDDEN, D_MODEL), BETA * EXPERT_HIDDEN ** -0.5),
        "ln2_g": 1.0 + n(21, (DEPTH, D_MODEL), 0.01),
        "ln2_b": n(22, (DEPTH, D_MODEL), 0.01),
    }


def reference(x_prompt, x_sample, cache_k, cache_v, state_ret, page_table, w_in, b_merge, gn_g,
              w_br_ret, w_br_moba, w_out, ln1_g, ln1_b, w_router_group, b_router_group,
              w_router_expert, b_router_expert, w_exp_gate, w_exp_up, w_exp_down, ln2_g, ln2_b):
    db, n_pages = page_table.shape
    yp, ys = x_prompt, x_sample
    kp, vp, sp, ksm, vsm, ssm = [], [], [], [], [], []
    for l in range(DEPTH):
        lw = (w_in[l], b_merge[l], gn_g[l], w_br_ret[l], w_br_moba[l], w_out[l], ln1_g[l], ln1_b[l],
              w_router_group[l], b_router_group[l], w_router_expert[l], b_router_expert[l],
              w_exp_gate[l], w_exp_up[l], w_exp_down[l], ln2_g[l], ln2_b[l])
        yp, (k_p, v_p, s_p) = decoder_layer(yp, retention_prompt, moba_prompt, *lw)
        past_k = cache_k[l][page_table].reshape(db, n_pages * PAGE_SIZE, MOBA_HEADS, MOBA_DH)
        past_v = cache_v[l][page_table].reshape(db, n_pages * PAGE_SIZE, MOBA_HEADS, MOBA_DH)
        ys, (k_s, v_s, s_s) = decoder_layer(
            ys, functools.partial(retention_sample, state_ret[l]),
            functools.partial(moba_sample, past_k=past_k, past_v=past_v), *lw)
        kp.append(k_p); vp.append(v_p); sp.append(s_p)
        ksm.append(k_s); vsm.append(v_s); ssm.append(s_s)
    return (yp, ys, jnp.stack(kp), jnp.stack(vp), jnp.stack(sp),
            jnp.stack(ksm), jnp.stack(vsm), jnp.stack(ssm))
```

```python
import functools

import numpy as np
import jax
import jax.numpy as jnp
from jax import lax
from jax.experimental import pallas as pl
from jax.experimental.pallas import tpu as pltpu

F32 = jnp.float32
BF16 = jnp.bfloat16

D_MODEL = 1024
RET_HEADS = 4
RET_DK = 128
RET_WIDTH = RET_HEADS * RET_DK
RET_CHUNK = 128
MOBA_HEADS = 8
MOBA_DH = 64
MOBA_WIDTH = MOBA_HEADS * MOBA_DH
MOBA_BLOCK = 256
MOBA_TOPK = 3
PAGE_SIZE = 128
N_GROUPS = 4
EXPERTS_PER_GROUP = 8
EXPERT_HIDDEN = 128
GROUP_HIDDEN = EXPERTS_PER_GROUP * EXPERT_HIDDEN
DEPTH = 2
ALPHA = (2 * DEPTH) ** 0.25
LN_EPS = 1e-5
N_PROJ = 4 * RET_WIDTH + 3 * MOBA_WIDTH

LANES = 128
PAIR = 2 * MOBA_DH
N_PAIRS = MOBA_HEADS // 2
SAMPLE_ROWS = 16
ROUTER_LANES = 128
VMEM_LIMIT = 56 * 1024 * 1024

NT = (((1,), (1,)), ((), ()))
TN = (((0,), (0,)), ((), ()))


def _params(*sem):
    return pltpu.CompilerParams(dimension_semantics=sem, vmem_limit_bytes=VMEM_LIMIT)


def _const_spec(shape):
    nd = len(shape)
    return pl.BlockSpec(shape, lambda *_: (0,) * nd, pipeline_mode=pl.Buffered(1))


def _layer_norm_rows(z, g, b):
    mu = jnp.mean(z, axis=-1, keepdims=True)
    zc = z - mu
    var = jnp.mean(zc * zc, axis=-1, keepdims=True)
    return zc * lax.rsqrt(var + LN_EPS) * g + b


def _sigmoid(x):
    return 1.0 / (1.0 + jnp.exp(-x))


def _in_proj_kernel(x_ref, w_ref, *refs, transposed, tm):
    if transposed:
        (wqT_ref, wvT_ref, rq_ref, rk_ref, rv_ref, rg_ref, mq_ref, k_ref, v_ref,
         k16_ref, qT_ref, vT_ref, kmean_ref) = refs
    else:
        rq_ref, rk_ref, rv_ref, rg_ref, mq_ref, k_ref, v_ref = refs
    xb = x_ref[...].astype(BF16)

    def proj(c):
        return jnp.dot(xb, w_ref[:, c * RET_WIDTH:(c + 1) * RET_WIDTH], preferred_element_type=F32)

    rq_ref[...] = proj(0).astype(BF16)
    rk_ref[...] = (proj(1) * RET_DK ** -0.5).astype(BF16)
    rv_ref[...] = proj(2).astype(BF16)
    rg_ref[...] = proj(3)
    mq_ref[...] = (proj(4) * MOBA_DH ** -0.5).astype(BF16)
    k = proj(5)
    k_ref[...] = k
    v_ref[...] = proj(6)
    if transposed:
        k16_ref[...] = k.astype(BF16)
        qT = lax.dot_general(wqT_ref[...], xb, NT, preferred_element_type=F32)
        qT_ref[...] = (qT * MOBA_DH ** -0.5).astype(BF16)
        vT_ref[...] = lax.dot_general(wvT_ref[...], xb, NT, preferred_element_type=F32).astype(BF16)
        nb = tm // MOBA_BLOCK
        kmean_ref[0] = jnp.mean(k.reshape(nb, MOBA_BLOCK, MOBA_WIDTH), axis=1)


def _in_proj(x, w16, wqT16, wvT16, *, transposed, tm):
    T = x.shape[0]
    row = lambda i: (i, 0)
    tile = lambda: pl.BlockSpec((tm, RET_WIDTH), row)
    in_specs = [pl.BlockSpec((tm, D_MODEL), row), _const_spec(w16.shape)]
    args = [x, w16]
    out_shape = [jax.ShapeDtypeStruct((T, RET_WIDTH), BF16)] * 3 + [
        jax.ShapeDtypeStruct((T, RET_WIDTH), F32), jax.ShapeDtypeStruct((T, MOBA_WIDTH), BF16),
        jax.ShapeDtypeStruct((T, MOBA_WIDTH), F32), jax.ShapeDtypeStruct((T, MOBA_WIDTH), F32)]
    out_specs = [tile() for _ in range(7)]
    if transposed:
        in_specs += [_const_spec(wqT16.shape), _const_spec(wvT16.shape)]
        args += [wqT16, wvT16]
        col = lambda i: (0, i)
        out_shape += [jax.ShapeDtypeStruct((T, MOBA_WIDTH), BF16),
                      jax.ShapeDtypeStruct((MOBA_WIDTH, T), BF16),
                      jax.ShapeDtypeStruct((MOBA_WIDTH, T), BF16),
                      jax.ShapeDtypeStruct((T // tm, tm // MOBA_BLOCK, MOBA_WIDTH), F32)]
        out_specs += [tile(), pl.BlockSpec((MOBA_WIDTH, tm), col), pl.BlockSpec((MOBA_WIDTH, tm), col),
                      pl.BlockSpec((1, tm // MOBA_BLOCK, MOBA_WIDTH), lambda i: (i, 0, 0))]
    outs = pl.pallas_call(
        functools.partial(_in_proj_kernel, transposed=transposed, tm=tm),
        grid=(T // tm,), in_specs=in_specs, out_specs=out_specs, out_shape=out_shape,
        compiler_params=_params("arbitrary"), name="in_proj",
    )(*args)
    if transposed:
        outs = list(outs)
        outs[-1] = outs[-1].reshape(T // MOBA_BLOCK, MOBA_WIDTH)
    return outs


def _ret_tables(L, rows):
    log_g = jnp.log(jnp.asarray(1.0 - 2.0 ** (-5.0 - np.arange(RET_HEADS)), dtype=F32))
    idx = jnp.arange(L, dtype=F32)
    diff = idx[:, None] - idx[None, :]
    decay = jnp.where(diff >= 0, jnp.exp(log_g[:, None, None] * jnp.maximum(diff, 0.0)), 0.0)
    qdec = jnp.exp(log_g[:, None] * (idx + 1.0))
    kdec = jnp.exp(log_g[:, None] * (L - 1.0 - idx))
    g_chunk = jnp.exp(log_g * L)
    pad = rows - L
    decay = jnp.pad(decay, ((0, 0), (0, pad), (0, pad)))
    lanes = lambda t: jnp.broadcast_to(jnp.pad(t, ((0, 0), (0, pad)))[:, :, None], (RET_HEADS, rows, LANES))
    return decay, lanes(qdec), lanes(kdec), jnp.broadcast_to(g_chunk[:, None, None], (RET_HEADS, 1, LANES))


def _ret_head(q, k, v, state, decay, qdec, kdec, g_chunk):
    scores = lax.dot_general(q, k, NT, preferred_element_type=F32) * decay
    inner = jnp.dot(scores.astype(BF16), v, preferred_element_type=F32)
    q_dec = (q.astype(F32) * qdec).astype(BF16)
    cross = jnp.dot(q_dec, state.astype(BF16), preferred_element_type=F32)
    k_dec = (k.astype(F32) * kdec).astype(BF16)
    new_state = state * g_chunk + lax.dot_general(k_dec, v, TN, preferred_element_type=F32)
    return inner + cross, new_state


def _ret_gate(o, rg, gn):
    mu = jnp.mean(o, axis=-1, keepdims=True)
    oc = o - mu
    var = jnp.mean(oc * oc, axis=-1, keepdims=True)
    return (rg * _sigmoid(rg)) * (oc * lax.rsqrt(var + LN_EPS) * gn)


def _ret_prompt_kernel(q_ref, k_ref, v_ref, rg_ref, gn_ref, decay_ref, qdec_ref, kdec_ref, gc_ref,
                       o_ref, sfin_ref, state_sc):
    c = pl.program_id(1)

    @pl.when(c == 0)
    def _():
        state_sc[...] = jnp.zeros_like(state_sc)

    for h in range(RET_HEADS):
        sl = slice(h * RET_DK, (h + 1) * RET_DK)
        o, new_state = _ret_head(q_ref[:, sl], k_ref[:, sl], v_ref[:, sl], state_sc[h],
                                 decay_ref[h], qdec_ref[h], kdec_ref[h], gc_ref[h])
        state_sc[h] = new_state
        o_ref[:, sl] = _ret_gate(o, rg_ref[:, sl], gn_ref[:, sl]).astype(BF16)

    @pl.when(c == pl.num_programs(1) - 1)
    def _():
        sfin_ref[0] = state_sc[...]


def _ret_prompt(rq, rk, rv, rg, gn, batch):
    T = rq.shape[0]
    n_c = T // batch // RET_CHUNK
    tables = _ret_tables(RET_CHUNK, RET_CHUNK)
    row = lambda b, c: (b * n_c + c, 0)
    tile = pl.BlockSpec((RET_CHUNK, RET_WIDTH), row)
    return pl.pallas_call(
        _ret_prompt_kernel,
        grid=(batch, n_c),
        in_specs=[tile, tile, tile, tile, _const_spec(gn.shape)] + [_const_spec(t.shape) for t in tables],
        out_specs=[tile, pl.BlockSpec((1, RET_HEADS, RET_DK, RET_DK), lambda b, c: (b, 0, 0, 0))],
        out_shape=[jax.ShapeDtypeStruct((T, RET_WIDTH), BF16),
                   jax.ShapeDtypeStruct((batch, RET_HEADS, RET_DK, RET_DK), F32)],
        scratch_shapes=[pltpu.VMEM((RET_HEADS, RET_DK, RET_DK), F32)],
        compiler_params=_params("arbitrary", "arbitrary"), name="ret_prompt",
    )(rq, rk, rv, rg, gn, *tables)


def _ret_sample_kernel(q_ref, k_ref, v_ref, rg_ref, gn_ref, s_ref, decay_ref, qdec_ref, kdec_ref, gc_ref,
                       o_ref, snew_ref, *, bt, L):
    for bi in range(bt):
        for h in range(RET_HEADS):
            sl = slice(h * RET_DK, (h + 1) * RET_DK)
            o, new_state = _ret_head(q_ref[bi, :, sl], k_ref[bi, :, sl], v_ref[bi, :, sl], s_ref[bi, h],
                                     decay_ref[h], qdec_ref[h], kdec_ref[h], gc_ref[h])
            snew_ref[bi, h] = new_state
            o_ref[bi, :, sl] = _ret_gate(o[:L], rg_ref[bi, :, sl], gn_ref[:, sl])


def _ret_sample(rq, rk, rv, rg, gn, state, *, bt=8):
    db, L = rg.shape[0], rg.shape[1]
    tables = _ret_tables(L, SAMPLE_ROWS)
    b3 = lambda i: (i, 0, 0)
    qkv = pl.BlockSpec((bt, SAMPLE_ROWS, RET_WIDTH), b3)
    st = pl.BlockSpec((bt, RET_HEADS, RET_DK, RET_DK), lambda i: (i, 0, 0, 0))
    return pl.pallas_call(
        functools.partial(_ret_sample_kernel, bt=bt, L=L),
        grid=(db // bt,),
        in_specs=[qkv, qkv, qkv, pl.BlockSpec((bt, L, RET_WIDTH), b3), _const_spec(gn.shape), st]
        + [_const_spec(t.shape) for t in tables],
        out_specs=[pl.BlockSpec((bt, L, RET_WIDTH), b3), st],
        out_shape=[jax.ShapeDtypeStruct((db, L, RET_WIDTH), F32),
                   jax.ShapeDtypeStruct((db, RET_HEADS, RET_DK, RET_DK), F32)],
        compiler_params=_params("arbitrary"), name="ret_sample",
    )(rq, rk, rv, rg, gn, state, *tables)


def _alibi_slopes():
    return 2.0 ** (-8.0 * np.arange(1, MOBA_HEADS + 1) / MOBA_HEADS)


def _moba_prompt_tables():
    slopes = jnp.asarray(_alibi_slopes(), dtype=F32)[:, None, None]
    kk = jnp.arange(MOBA_BLOCK, dtype=F32)[:, None]
    qq = jnp.arange(MOBA_BLOCK, dtype=F32)[None, :]
    dist = (qq - kk)[None]
    past = -(slopes * dist)
    own = jnp.where(dist >= 0, past, -jnp.inf)
    block_step = -(slopes * float(MOBA_BLOCK))
    return past, own, jnp.broadcast_to(block_step, (MOBA_HEADS, 1, MOBA_BLOCK))


def _moba_prompt_kernel(qT_ref, k_ref, vT_ref, kmean_ref, past_ref, own_ref, step_ref, o_ref,
                        m_sc, l_sc, acc_sc, term_sc):
    j = pl.program_id(1)
    nb = kmean_ref.shape[0]
    dh_row = lax.broadcasted_iota(jnp.int32, (PAIR, MOBA_BLOCK), 0)
    blk = lax.broadcasted_iota(jnp.int32, (nb, MOBA_BLOCK), 0)

    def head_q(p, e):
        qT = qT_ref[p * PAIR:(p + 1) * PAIR, :]
        keep = (dh_row < MOBA_DH) if e == 0 else (dh_row >= MOBA_DH)
        return jnp.where(keep, qT, jnp.zeros_like(qT))

    own_start = pl.multiple_of(j * MOBA_BLOCK, MOBA_BLOCK)
    for p in range(N_PAIRS):
        cols = slice(p * PAIR, (p + 1) * PAIR)
        kmean = kmean_ref[:, cols].astype(BF16)
        k_own = k_ref[pl.ds(own_start, MOBA_BLOCK), cols]
        for e in range(2):
            h = 2 * p + e
            qh = head_q(p, e)
            gate = jnp.dot(kmean, qh, preferred_element_type=F32)
            gate = jnp.where(blk < j, gate, -jnp.inf)
            sel = jnp.zeros(gate.shape, dtype=jnp.bool_)
            for _ in range(MOBA_TOPK):
                top = jnp.max(gate, axis=0, keepdims=True)
                first = jnp.min(jnp.where(gate == top, blk, nb), axis=0, keepdims=True)
                pick = jnp.logical_and(blk == first, top > -jnp.inf)
                sel = jnp.logical_or(sel, pick)
                gate = jnp.where(pick, -jnp.inf, gate)
            term_sc[h] = jnp.where(sel, (j - blk).astype(F32) * step_ref[h], -jnp.inf)
            s = jnp.dot(k_own, qh, preferred_element_type=F32) + own_ref[h]
            m = jnp.max(s, axis=0, keepdims=True)
            pexp = jnp.exp(s - m)
            m_sc[h] = m
            l_sc[h] = jnp.sum(pexp, axis=0, keepdims=True)
            vT = vT_ref[h * MOBA_DH:(h + 1) * MOBA_DH, pl.ds(own_start, MOBA_BLOCK)]
            acc_sc[h] = jnp.dot(vT, pexp.astype(BF16), preferred_element_type=F32)

    def past_block(jj, carry):
        start = pl.multiple_of(jj * MOBA_BLOCK, MOBA_BLOCK)
        for p in range(N_PAIRS):
            cols = slice(p * PAIR, (p + 1) * PAIR)
            k_blk = k_ref[pl.ds(start, MOBA_BLOCK), cols]
            for e in range(2):
                h = 2 * p + e
                s = jnp.dot(k_blk, head_q(p, e), preferred_element_type=F32) + past_ref[h]
                term = term_sc[h, pl.ds(jj, 1), :]
                m_old = m_sc[h]
                m_new = jnp.maximum(m_old, jnp.max(s, axis=0, keepdims=True) + term)
                pexp = jnp.exp(s - (m_new - term))
                alpha = jnp.exp(m_old - m_new)
                m_sc[h] = m_new
                l_sc[h] = alpha * l_sc[h] + jnp.sum(pexp, axis=0, keepdims=True)
                vT = vT_ref[h * MOBA_DH:(h + 1) * MOBA_DH, pl.ds(start, MOBA_BLOCK)]
                acc_sc[h] = alpha * acc_sc[h] + jnp.dot(vT, pexp.astype(BF16), preferred_element_type=F32)
        return carry

    lax.fori_loop(0, j, past_block, 0)

    for p in range(N_PAIRS):
        outT = jnp.concatenate([acc_sc[2 * p] / l_sc[2 * p], acc_sc[2 * p + 1] / l_sc[2 * p + 1]], axis=0)
        o_ref[:, p * PAIR:(p + 1) * PAIR] = outT.T.astype(BF16)


def _moba_prompt(qT16, k16, vT16, kmean, batch):
    T = k16.shape[0]
    S = T // batch
    nb = S // MOBA_BLOCK
    tables = _moba_prompt_tables()
    return pl.pallas_call(
        _moba_prompt_kernel,
        grid=(batch, nb),
        in_specs=[pl.BlockSpec((MOBA_WIDTH, MOBA_BLOCK), lambda b, j: (0, b * nb + j)),
                  pl.BlockSpec((S, MOBA_WIDTH), lambda b, j: (b, 0)),
                  pl.BlockSpec((MOBA_WIDTH, S), lambda b, j: (0, b)),
                  pl.BlockSpec((nb, MOBA_WIDTH), lambda b, j: (b, 0))]
        + [_const_spec(t.shape) for t in tables],
        out_specs=pl.BlockSpec((MOBA_BLOCK, MOBA_WIDTH), lambda b, j: (b * nb + j, 0)),
        out_shape=jax.ShapeDtypeStruct((T, MOBA_WIDTH), BF16),
        scratch_shapes=[pltpu.VMEM((MOBA_HEADS, 1, MOBA_BLOCK), F32), pltpu.VMEM((MOBA_HEADS, 1, MOBA_BLOCK), F32),
                        pltpu.VMEM((MOBA_HEADS, MOBA_DH, MOBA_BLOCK), F32),
                        pltpu.VMEM((MOBA_HEADS, nb, MOBA_BLOCK), F32)],
        compiler_params=_params("arbitrary", "arbitrary"), name="moba_prompt",
    )(qT16, k16, vT16, kmean, *tables)


def _moba_sample_tables(L, n_pages):
    past_len = n_pages * PAGE_SIZE
    slopes = _alibi_slopes()
    row_slope = np.zeros((N_PAIRS, SAMPLE_ROWS), np.float64)
    row_t = np.zeros((SAMPLE_ROWS,), np.float64)
    for p in range(N_PAIRS):
        row_slope[p, :L] = slopes[2 * p]
        row_slope[p, L:2 * L] = slopes[2 * p + 1]
    row_t[:L] = np.arange(L)
    row_t[L:2 * L] = np.arange(L)
    row_slope = jnp.asarray(row_slope, dtype=F32)[:, :, None]
    q_pos = jnp.asarray(past_len + row_t, dtype=F32)[None, :, None]
    key_pos = jnp.arange(past_len, dtype=F32)[None, None, :]
    past = -(row_slope * (q_pos - key_pos))
    new_pos = jnp.arange(SAMPLE_ROWS, dtype=F32)[None, None, :]
    dist_new = jnp.asarray(row_t, dtype=F32)[None, :, None] - new_pos
    valid = jnp.logical_and(dist_new >= 0, new_pos < L)
    new = jnp.where(valid, -(row_slope * dist_new), -jnp.inf)
    return past, new


def _moba_sample_kernel(pt_ref, q_ref, kn_ref, vn_ref, past_ref, new_ref, *refs, L, n_pages):
    k_pages, v_pages, o_ref = refs[:n_pages], refs[n_pages:2 * n_pages], refs[2 * n_pages]
    del pt_ref
    n_blk = n_pages * PAGE_SIZE // MOBA_BLOCK
    per_blk = MOBA_BLOCK // PAGE_SIZE
    page_sum = [jnp.sum(k_pages[s][0], axis=0, keepdims=True) for s in range(n_pages)]
    kmean = []
    for jj in range(n_blk):
        tot = page_sum[jj * per_blk]
        for t in range(1, per_blk):
            tot = tot + page_sum[jj * per_blk + t]
        kmean.append(tot * (1.0 / MOBA_BLOCK))
    row = lax.broadcasted_iota(jnp.int32, (SAMPLE_ROWS, PAIR), 0)
    lane = lax.broadcasted_iota(jnp.int32, (SAMPLE_ROWS, PAIR), 1)
    keep = jnp.logical_or(jnp.logical_and(row < L, lane < MOBA_DH),
                          jnp.logical_and(jnp.logical_and(row >= L, row < 2 * L), lane >= MOBA_DH))
    for p in range(N_PAIRS):
        cols = slice(p * PAIR, (p + 1) * PAIR)
        q = q_ref[0, :, cols]
        qm = jnp.where(keep, q, jnp.zeros_like(q))
        qf = qm.astype(F32)
        gate = [jnp.sum(qf * kmean[jj][:, cols].astype(BF16).astype(F32), axis=-1, keepdims=True)
                for jj in range(n_blk)]
        sel = []
        for jj in range(n_blk):
            ahead = jnp.zeros(gate[jj].shape, F32)
            for kk in range(n_blk):
                if kk == jj:
                    continue
                beats = (gate[kk] >= gate[jj]) if kk < jj else (gate[kk] > gate[jj])
                ahead = ahead + jnp.where(beats, 1.0, 0.0)
            sel.append(ahead < float(min(MOBA_TOPK, n_blk)))
        logit = []
        for s in range(n_pages):
            k_pg = k_pages[s][0][:, cols].astype(BF16)
            sc = lax.dot_general(qm, k_pg, NT, preferred_element_type=F32)
            sc = sc + past_ref[p, :, s * PAGE_SIZE:(s + 1) * PAGE_SIZE]
            logit.append(jnp.where(sel[s // per_blk], sc, -jnp.inf))
        kn = kn_ref[0, :, cols]
        s_new = lax.dot_general(qm, kn, NT, preferred_element_type=F32) + new_ref[p]
        m = jnp.max(s_new, axis=-1, keepdims=True)
        for s in range(n_pages):
            m = jnp.maximum(m, jnp.max(logit[s], axis=-1, keepdims=True))
        p_new = jnp.exp(s_new - m)
        denom = jnp.sum(p_new, axis=-1, keepdims=True)
        acc = jnp.dot(p_new.astype(BF16), vn_ref[0, :, cols], preferred_element_type=F32)
        for s in range(n_pages):
            pexp = jnp.exp(logit[s] - m)
            denom = denom + jnp.sum(pexp, axis=-1, keepdims=True)
            v_pg = v_pages[s][0][:, cols].astype(BF16)
            acc = acc + jnp.dot(pexp.astype(BF16), v_pg, preferred_element_type=F32)
        out = acc / denom
        o_ref[0, :, cols] = jnp.where(lane[:L] < MOBA_DH, out[:L], out[L:2 * L])


def _moba_sample(q16, kn16, vn16, cache_k, cache_v, page_table, L):
    db, n_pages = page_table.shape
    past, new = _moba_sample_tables(L, n_pages)
    b3 = lambda b, pt: (b, 0, 0)
    row_spec = pl.BlockSpec((1, SAMPLE_ROWS, MOBA_WIDTH), b3)

    def page_spec(s):
        return pl.BlockSpec((1, PAGE_SIZE, MOBA_WIDTH), lambda b, pt: (pt[b, s], 0, 0))

    in_specs = [row_spec, row_spec, row_spec,
                pl.BlockSpec(past.shape, lambda b, pt: (0, 0, 0)), pl.BlockSpec(new.shape, lambda b, pt: (0, 0, 0))]
    in_specs += [page_spec(s) for s in range(n_pages)] * 2
    grid_spec = pltpu.PrefetchScalarGridSpec(
        num_scalar_prefetch=1, grid=(db,), in_specs=in_specs,
        out_specs=pl.BlockSpec((1, L, MOBA_WIDTH), b3))
    return pl.pallas_call(
        functools.partial(_moba_sample_kernel, L=L, n_pages=n_pages),
        grid_spec=grid_spec,
        out_shape=jax.ShapeDtypeStruct((db, L, MOBA_WIDTH), F32),
        compiler_params=_params("arbitrary"), name="moba_sample",
    )(page_table, q16, kn16, vn16, past, new, *([cache_k] * n_pages), *([cache_v] * n_pages))


def _merge_kernel(x_ref, or_ref, om_ref, wg_ref, bg_ref, wr_ref, wm_ref, wo_ref, g_ref, b_ref, o_ref):
    x = x_ref[...]
    gates = _sigmoid(jnp.dot(x.astype(BF16), wg_ref[...], preferred_element_type=F32) + bg_ref[...])
    br = jnp.dot(or_ref[...], wr_ref[...], preferred_element_type=F32)
    bm = jnp.dot(om_ref[...], wm_ref[...], preferred_element_type=F32)
    merged = gates[:, :D_MODEL] * br + gates[:, D_MODEL:] * bm
    y = jnp.dot(merged.astype(BF16), wo_ref[...], preferred_element_type=F32)
    o_ref[...] = _layer_norm_rows(ALPHA * x + y, g_ref[...], b_ref[...])


def _merge(x, o_r, o_m, wg16, bg, wr16, wm16, wo16, g, b, *, tm):
    T = x.shape[0]
    row = lambda i: (i, 0)
    consts = [wg16, bg, wr16, wm16, wo16, g, b]
    return pl.pallas_call(
        _merge_kernel,
        grid=(T // tm,),
        in_specs=[pl.BlockSpec((tm, D_MODEL), row), pl.BlockSpec((tm, RET_WIDTH), row),
                  pl.BlockSpec((tm, MOBA_WIDTH), row)] + [_const_spec(c.shape) for c in consts],
        out_specs=pl.BlockSpec((tm, D_MODEL), row),
        out_shape=jax.ShapeDtypeStruct((T, D_MODEL), F32),
        compiler_params=_params("arbitrary"), name="merge",
    )(x, o_r, o_m, *consts)


def _moe_kernel(x_ref, wrh_ref, wrl_ref, br_ref, weg_ref, weu_ref, wed_ref, g_ref, b_ref, o_ref):
    x = x_ref[...]
    tm = x.shape[0]
    hi = x.astype(BF16)
    lo = (x - hi.astype(F32)).astype(BF16)
    logit = (jnp.dot(hi, wrh_ref[...], preferred_element_type=F32)
             + (jnp.dot(hi, wrl_ref[...], preferred_element_type=F32)
                + jnp.dot(lo, wrh_ref[...], preferred_element_type=F32))) + br_ref[...]
    lane = lax.broadcasted_iota(jnp.int32, (tm, ROUTER_LANES), 1)
    neg = -jnp.inf
    is_group = lane < N_GROUPS
    gl = jnp.where(is_group, logit, neg)
    gmax = jnp.max(gl, axis=-1, keepdims=True)
    gidx = jnp.min(jnp.where(gl == gmax, lane, ROUTER_LANES), axis=-1, keepdims=True)
    g_w = 1.0 / jnp.sum(jnp.exp(gl - gmax), axis=-1, keepdims=True)
    first = N_GROUPS + EXPERTS_PER_GROUP * gidx
    in_group = jnp.logical_and(lane >= first, lane < first + EXPERTS_PER_GROUP)
    el = jnp.where(in_group, logit, neg)
    e1 = jnp.max(el, axis=-1, keepdims=True)
    i1 = jnp.min(jnp.where(el == e1, lane, ROUTER_LANES), axis=-1, keepdims=True)
    el2 = jnp.where(lane == i1, neg, el)
    e2 = jnp.max(el2, axis=-1, keepdims=True)
    i2 = jnp.min(jnp.where(el2 == e2, lane, ROUTER_LANES), axis=-1, keepdims=True)
    t = jnp.exp(e2 - e1)
    w1 = g_w / (1.0 + t)
    w2 = g_w * t / (1.0 + t)
    comb = jnp.where(lane == i1, w1, 0.0) + jnp.where(lane == i2, w2, 0.0)
    acc = jnp.zeros((tm, D_MODEL), F32)
    for g in range(N_GROUPS):
        hg = jnp.dot(hi, weg_ref[g], preferred_element_type=F32)
        hu = jnp.dot(hi, weu_ref[g], preferred_element_type=F32)
        cexp = jnp.concatenate(
            [jnp.broadcast_to(comb[:, N_GROUPS + g * EXPERTS_PER_GROUP + e:N_GROUPS + g * EXPERTS_PER_GROUP + e + 1],
                              (tm, EXPERT_HIDDEN)) for e in range(EXPERTS_PER_GROUP)], axis=1)
        hid = (hg * _sigmoid(hg)) * hu
        acc = acc + jnp.dot((hid * cexp).astype(BF16), wed_ref[g], preferred_element_type=F32)
    o_ref[...] = _layer_norm_rows(ALPHA * x + acc, g_ref[...], b_ref[...])


def _moe(x, wr_hi, wr_lo, br, weg16, weu16, wed16, g, b, *, tm):
    T = x.shape[0]
    row = lambda i: (i, 0)
    consts = [wr_hi, wr_lo, br, weg16, weu16, wed16, g, b]
    return pl.pallas_call(
        _moe_kernel,
        grid=(T // tm,),
        in_specs=[pl.BlockSpec((tm, D_MODEL), row)] + [_const_spec(c.shape) for c in consts],
        out_specs=pl.BlockSpec((tm, D_MODEL), row),
        out_shape=jax.ShapeDtypeStruct((T, D_MODEL), F32),
        compiler_params=_params("arbitrary"), name="moe",
    )(x, *consts)


def _layer_weights(l, w_in, b_merge, gn_g, w_br_ret, w_br_moba, w_out, ln1_g, ln1_b, w_router_group,
                   b_router_group, w_router_expert, b_router_expert, w_exp_gate, w_exp_up, w_exp_down,
                   ln2_g, ln2_b):
    w = w_in[l]
    q0 = 4 * RET_WIDTH
    v0 = q0 + 2 * MOBA_WIDTH
    n_exp = N_GROUPS * EXPERTS_PER_GROUP
    w_r = jnp.concatenate(
        [w_router_group[l], w_router_expert[l].transpose(1, 0, 2).reshape(D_MODEL, n_exp),
         jnp.zeros((D_MODEL, ROUTER_LANES - N_GROUPS - n_exp), F32)], axis=1)
    w_r_hi = w_r.astype(BF16)
    w_r_lo = (w_r - w_r_hi.astype(F32)).astype(BF16)
    b_r = jnp.concatenate([b_router_group[l], b_router_expert[l].reshape(n_exp),
                           jnp.zeros((ROUTER_LANES - N_GROUPS - n_exp,), F32)])[None, :]
    by_group = lambda t: (t.reshape(N_GROUPS, EXPERTS_PER_GROUP, D_MODEL, EXPERT_HIDDEN)
                          .transpose(0, 2, 1, 3).reshape(N_GROUPS, D_MODEL, GROUP_HIDDEN).astype(BF16))
    return dict(
        w_proj=w[:, :N_PROJ].astype(BF16),
        wqT=w[:, q0:q0 + MOBA_WIDTH].T.astype(BF16),
        wvT=w[:, v0:v0 + MOBA_WIDTH].T.astype(BF16),
        w_gate=w[:, N_PROJ:].astype(BF16), b_gate=b_merge[l][None, :],
        gn=gn_g[l][None, :],
        w_br_ret=w_br_ret[l].astype(BF16), w_br_moba=w_br_moba[l].astype(BF16), w_out=w_out[l].astype(BF16),
        ln1_g=ln1_g[l][None, :], ln1_b=ln1_b[l][None, :],
        w_r_hi=w_r_hi, w_r_lo=w_r_lo, b_r=b_r,
        w_eg=by_group(w_exp_gate[l]), w_eu=by_group(w_exp_up[l]),
        w_ed=w_exp_down[l].reshape(N_GROUPS, GROUP_HIDDEN, D_MODEL).astype(BF16),
        ln2_g=ln2_g[l][None, :], ln2_b=ln2_b[l][None, :])


def _pad_rows(t, rows):
    return jnp.pad(t, ((0, 0), (0, rows - t.shape[1]), (0, 0)))


def kernel(x_prompt, x_sample, cache_k, cache_v, state_ret, page_table, w_in, b_merge, gn_g, w_br_ret, w_br_moba, w_out, ln1_g, ln1_b, w_router_group, b_router_group, w_router_expert, b_router_expert, w_exp_gate, w_exp_up, w_exp_down, ln2_g, ln2_b):
    B, S, _ = x_prompt.shape
    DB, L, _ = x_sample.shape
    Tp, Ts = B * S, DB * L
    depth = w_in.shape[0]
    n_pool = cache_k.shape[1]
    xp = x_prompt.reshape(Tp, D_MODEL)
    xs = x_sample.reshape(Ts, D_MODEL)
    tm_p = min(512, Tp)
    tm_s = min(256, Ts)
    outs = [[] for _ in range(6)]
    for l in range(depth):
        W = _layer_weights(l, w_in, b_merge, gn_g, w_br_ret, w_br_moba, w_out, ln1_g, ln1_b, w_router_group,
                           b_router_group, w_router_expert, b_router_expert, w_exp_gate, w_exp_up, w_exp_down,
                           ln2_g, ln2_b)
        (rq, rk, rv, rg, _, k_p, v_p, k16, qT16, vT16, kmean) = _in_proj(
            xp, W["w_proj"], W["wqT"], W["wvT"], transposed=True, tm=tm_p)
        o_r, s_p = _ret_prompt(rq, rk, rv, rg, W["gn"], B)
        o_m = _moba_prompt(qT16, k16, vT16, kmean, B)
        x1 = _merge(xp, o_r, o_m, W["w_gate"], W["b_gate"], W["w_br_ret"], W["w_br_moba"], W["w_out"],
                    W["ln1_g"], W["ln1_b"], tm=tm_p)
        xp = _moe(x1, W["w_r_hi"], W["w_r_lo"], W["b_r"], W["w_eg"], W["w_eu"], W["w_ed"],
                  W["ln2_g"], W["ln2_b"], tm=tm_p)
        (rq, rk, rv, rg, mq, k_s, v_s) = _in_proj(xs, W["w_proj"], None, None, transposed=False, tm=tm_s)
        r3 = lambda t: t.reshape(DB, L, t.shape[-1])
        o_r, s_s = _ret_sample(_pad_rows(r3(rq), SAMPLE_ROWS), _pad_rows(r3(rk), SAMPLE_ROWS),
                               _pad_rows(r3(rv), SAMPLE_ROWS), r3(rg), W["gn"], state_ret[l])
        mq3 = r3(mq)
        q16 = _pad_rows(jnp.concatenate([mq3, mq3], axis=1), SAMPLE_ROWS)
        o_m = _moba_sample(q16, _pad_rows(r3(k_s).astype(BF16), SAMPLE_ROWS),
                           _pad_rows(r3(v_s).astype(BF16), SAMPLE_ROWS),
                           cache_k[l].reshape(n_pool, PAGE_SIZE, MOBA_WIDTH),
                           cache_v[l].reshape(n_pool, PAGE_SIZE, MOBA_WIDTH), page_table, L)
        x1 = _merge(xs, o_r.reshape(Ts, RET_WIDTH).astype(BF16), o_m.reshape(Ts, MOBA_WIDTH).astype(BF16),
                    W["w_gate"], W["b_gate"], W["w_br_ret"], W["w_br_moba"], W["w_out"],
                    W["ln1_g"], W["ln1_b"], tm=tm_s)
        xs = _moe(x1, W["w_r_hi"], W["w_r_lo"], W["b_r"], W["w_eg"], W["w_eu"], W["w_ed"],
                  W["ln2_g"], W["ln2_b"], tm=tm_s)
        for lst, val in zip(outs, (k_p.reshape(B, S, MOBA_HEADS, MOBA_DH), v_p.reshape(B, S, MOBA_HEADS, MOBA_DH), s_p,
                                   k_s.reshape(DB, L, MOBA_HEADS, MOBA_DH), v_s.reshape(DB, L, MOBA_HEADS, MOBA_DH), s_s)):
            lst.append(val)
    kp, vp, sp, ksm, vsm, ssm = (jnp.stack(o) for o in outs)
    return (xp.reshape(B, S, D_MODEL), xs.reshape(DB, L, D_MODEL), kp, vp, sp, ksm, vsm, ssm)
```

```python
import functools

import numpy as np
import jax
import jax.numpy as jnp
from jax import lax
from jax.experimental import pallas as pl
from jax.experimental.pallas import tpu as pltpu

F32 = jnp.float32
BF16 = jnp.bfloat16

D_MODEL = 1024
RET_HEADS = 4
RET_DK = 128
RET_WIDTH = RET_HEADS * RET_DK
RET_CHUNK = 128
MOBA_HEADS = 8
MOBA_DH = 64
MOBA_WIDTH = MOBA_HEADS * MOBA_DH
MOBA_BLOCK = 256
MOBA_TOPK = 3
PAGE_SIZE = 128
N_GROUPS = 4
EXPERTS_PER_GROUP = 8
EXPERT_HIDDEN = 128
GROUP_HIDDEN = EXPERTS_PER_GROUP * EXPERT_HIDDEN
DEPTH = 2
ALPHA = (2 * DEPTH) ** 0.25
LN_EPS = 1e-5
N_PROJ = 4 * RET_WIDTH + 3 * MOBA_WIDTH

LANES = 128
PAIR = 2 * MOBA_DH
N_PAIRS = MOBA_HEADS // 2
GROUP_HEADS = 4
SAMPLE_ROWS = 16
ROUTER_LANES = 128
VMEM_LIMIT = 56 * 1024 * 1024

NT = (((1,), (1,)), ((), ()))
TN = (((0,), (0,)), ((), ()))


def _params(*sem):
    return pltpu.CompilerParams(dimension_semantics=sem, vmem_limit_bytes=VMEM_LIMIT)


def _const_spec(shape):
    nd = len(shape)
    return pl.BlockSpec(shape, lambda *_: (0,) * nd, pipeline_mode=pl.Buffered(1))


def _layer_norm_rows(z, g, b):
    mu = jnp.mean(z, axis=-1, keepdims=True)
    zc = z - mu
    var = jnp.mean(zc * zc, axis=-1, keepdims=True)
    return zc * lax.rsqrt(var + LN_EPS) * g + b


def _sigmoid(x):
    return 1.0 / (1.0 + jnp.exp(-x))


def _in_proj_kernel(x_ref, w_ref, *refs, transposed, tm):
    xb = x_ref[...].astype(BF16)

    def proj(c):
        return jnp.dot(xb, w_ref[:, c * RET_WIDTH:(c + 1) * RET_WIDTH], preferred_element_type=F32)

    if transposed:
        wT_ref, rq_ref, rk_ref, rv_ref, rg_ref, kT_ref, vT_ref, k16_ref, qT16_ref, vT16_ref, kmean_ref = refs
        projT = lambda c: lax.dot_general(wT_ref[c], xb, NT, preferred_element_type=F32)
        qT16_ref[...] = (projT(0) * MOBA_DH ** -0.5).astype(BF16)
        kT = projT(1)
        kT_ref[0] = kT
        k = kT.T
        k16_ref[...] = k.astype(BF16)
        nb = tm // MOBA_BLOCK
        kmean_ref[0] = jnp.mean(k.reshape(nb, MOBA_BLOCK, MOBA_WIDTH), axis=1)
        vT = projT(2)
        vT_ref[0] = vT
        vT16_ref[...] = vT.astype(BF16)
    else:
        rq_ref, rk_ref, rv_ref, rg_ref, mq_ref, k_ref, v_ref = refs
        mq_ref[...] = (proj(4) * MOBA_DH ** -0.5).astype(BF16)
        k_ref[...] = proj(5)
        v_ref[...] = proj(6)
    rq_ref[...] = proj(0).astype(BF16)
    rk_ref[...] = (proj(1) * RET_DK ** -0.5).astype(BF16)
    rv_ref[...] = proj(2).astype(BF16)
    rg_ref[...] = proj(3)


def _in_proj(x, w16, wT16=None, *, batch=1, tm):
    T = x.shape[0]
    transposed = wT16 is not None
    row = lambda i: (i, 0)
    tile = lambda: pl.BlockSpec((tm, RET_WIDTH), row)
    in_specs = [pl.BlockSpec((tm, D_MODEL), row), _const_spec(w16.shape)]
    args = [x, w16]
    out_shape = [jax.ShapeDtypeStruct((T, RET_WIDTH), BF16)] * 3 + [jax.ShapeDtypeStruct((T, RET_WIDTH), F32)]
    out_specs = [tile() for _ in range(4)]
    if transposed:
        S = T // batch
        per_b = S // tm
        in_specs.append(_const_spec(wT16.shape))
        args.append(wT16)
        col = lambda i: (0, i)
        by_batch = lambda i: (i // per_b, 0, i % per_b)
        out_shape += [jax.ShapeDtypeStruct((batch, MOBA_WIDTH, S), F32)] * 2 + [
            jax.ShapeDtypeStruct((T, MOBA_WIDTH), BF16),
            jax.ShapeDtypeStruct((MOBA_WIDTH, T), BF16), jax.ShapeDtypeStruct((MOBA_WIDTH, T), BF16),
            jax.ShapeDtypeStruct((T // tm, tm // MOBA_BLOCK, MOBA_WIDTH), F32)]
        out_specs += [pl.BlockSpec((1, MOBA_WIDTH, tm), by_batch), pl.BlockSpec((1, MOBA_WIDTH, tm), by_batch),
                      tile(), pl.BlockSpec((MOBA_WIDTH, tm), col), pl.BlockSpec((MOBA_WIDTH, tm), col),
                      pl.BlockSpec((1, tm // MOBA_BLOCK, MOBA_WIDTH), lambda i: (i, 0, 0))]
    else:
        out_shape += [jax.ShapeDtypeStruct((T, MOBA_WIDTH), BF16)] + [jax.ShapeDtypeStruct((T, MOBA_WIDTH), F32)] * 2
        out_specs += [tile() for _ in range(3)]
    outs = pl.pallas_call(
        functools.partial(_in_proj_kernel, transposed=transposed, tm=tm),
        grid=(T // tm,), in_specs=in_specs, out_specs=out_specs, out_shape=out_shape,
        compiler_params=_params("arbitrary"), name="in_proj",
    )(*args)
    if transposed:
        outs = list(outs)
        outs[-1] = outs[-1].reshape(T // MOBA_BLOCK, MOBA_WIDTH)
    return outs


def _ret_tables(L, rows):
    log_g = jnp.log(jnp.asarray(1.0 - 2.0 ** (-5.0 - np.arange(RET_HEADS)), dtype=F32))
    idx = jnp.arange(L, dtype=F32)
    diff = idx[:, None] - idx[None, :]
    decay = jnp.where(diff >= 0, jnp.exp(log_g[:, None, None] * jnp.maximum(diff, 0.0)), 0.0)
    qdec = jnp.exp(log_g[:, None] * (idx + 1.0))
    kdec = jnp.exp(log_g[:, None] * (L - 1.0 - idx))
    g_chunk = jnp.exp(log_g * L)
    pad = rows - L
    decay = jnp.pad(decay, ((0, 0), (0, pad), (0, pad)))
    lanes = lambda t: jnp.broadcast_to(jnp.pad(t, ((0, 0), (0, pad)))[:, :, None], (RET_HEADS, rows, LANES))
    return decay, lanes(qdec), lanes(kdec), jnp.broadcast_to(g_chunk[:, None, None], (RET_HEADS, 1, LANES))


def _ret_head(q, k, v, state, decay, qdec, kdec, g_chunk):
    scores = lax.dot_general(q, k, NT, preferred_element_type=F32) * decay
    inner = jnp.dot(scores.astype(BF16), v, preferred_element_type=F32)
    q_dec = (q.astype(F32) * qdec).astype(BF16)
    cross = jnp.dot(q_dec, state.astype(BF16), preferred_element_type=F32)
    k_dec = (k.astype(F32) * kdec).astype(BF16)
    new_state = state * g_chunk + lax.dot_general(k_dec, v, TN, preferred_element_type=F32)
    return inner + cross, new_state


def _ret_gate(o, rg, gn):
    mu = jnp.mean(o, axis=-1, keepdims=True)
    oc = o - mu
    var = jnp.mean(oc * oc, axis=-1, keepdims=True)
    return (rg * _sigmoid(rg)) * (oc * lax.rsqrt(var + LN_EPS) * gn)


def _ret_prompt_kernel(q_ref, k_ref, v_ref, rg_ref, gn_ref, decay_ref, qdec_ref, kdec_ref, gc_ref,
                       o_ref, sfin_ref, state_sc):
    c = pl.program_id(1)

    @pl.when(c == 0)
    def _():
        state_sc[...] = jnp.zeros_like(state_sc)

    for h in range(RET_HEADS):
        sl = slice(h * RET_DK, (h + 1) * RET_DK)
        o, new_state = _ret_head(q_ref[:, sl], k_ref[:, sl], v_ref[:, sl], state_sc[h],
                                 decay_ref[h], qdec_ref[h], kdec_ref[h], gc_ref[h])
        state_sc[h] = new_state
        o_ref[:, sl] = _ret_gate(o, rg_ref[:, sl], gn_ref[:, sl]).astype(BF16)

    @pl.when(c == pl.num_programs(1) - 1)
    def _():
        sfin_ref[0] = state_sc[...]


def _ret_prompt(rq, rk, rv, rg, gn, batch):
    T = rq.shape[0]
    n_c = T // batch // RET_CHUNK
    tables = _ret_tables(RET_CHUNK, RET_CHUNK)
    row = lambda b, c: (b * n_c + c, 0)
    tile = pl.BlockSpec((RET_CHUNK, RET_WIDTH), row)
    return pl.pallas_call(
        _ret_prompt_kernel,
        grid=(batch, n_c),
        in_specs=[tile, tile, tile, tile, _const_spec(gn.shape)] + [_const_spec(t.shape) for t in tables],
        out_specs=[tile, pl.BlockSpec((1, RET_HEADS, RET_DK, RET_DK), lambda b, c: (b, 0, 0, 0))],
        out_shape=[jax.ShapeDtypeStruct((T, RET_WIDTH), BF16),
                   jax.ShapeDtypeStruct((batch, RET_HEADS, RET_DK, RET_DK), F32)],
        scratch_shapes=[pltpu.VMEM((RET_HEADS, RET_DK, RET_DK), F32)],
        compiler_params=_params("arbitrary", "arbitrary"), name="ret_prompt",
    )(rq, rk, rv, rg, gn, *tables)


def _ret_sample_kernel(q_ref, k_ref, v_ref, rg_ref, gn_ref, s_ref, decay_ref, qdec_ref, kdec_ref, gc_ref,
                       o_ref, snew_ref, *, bt, L):
    for bi in range(bt):
        for h in range(RET_HEADS):
            sl = slice(h * RET_DK, (h + 1) * RET_DK)
            o, new_state = _ret_head(q_ref[bi, :, sl], k_ref[bi, :, sl], v_ref[bi, :, sl], s_ref[0, bi, h],
                                     decay_ref[h], qdec_ref[h], kdec_ref[h], gc_ref[h])
            snew_ref[bi, h] = new_state
            o_ref[bi, :, sl] = _ret_gate(o[:L], rg_ref[bi, :, sl], gn_ref[:, sl])


def _ret_sample(rq, rk, rv, rg, gn, state, layer, *, bt=8):
    db, L = rg.shape[0], rg.shape[1]
    tables = _ret_tables(L, SAMPLE_ROWS)
    b3 = lambda i: (i, 0, 0)
    qkv = pl.BlockSpec((bt, SAMPLE_ROWS, RET_WIDTH), b3)
    st = pl.BlockSpec((bt, RET_HEADS, RET_DK, RET_DK), lambda i: (i, 0, 0, 0))
    st_in = pl.BlockSpec((1, bt, RET_HEADS, RET_DK, RET_DK), lambda i: (layer, i, 0, 0, 0))
    return pl.pallas_call(
        functools.partial(_ret_sample_kernel, bt=bt, L=L),
        grid=(db // bt,),
        in_specs=[qkv, qkv, qkv, pl.BlockSpec((bt, L, RET_WIDTH), b3), _const_spec(gn.shape), st_in]
        + [_const_spec(t.shape) for t in tables],
        out_specs=[pl.BlockSpec((bt, L, RET_WIDTH), b3), st],
        out_shape=[jax.ShapeDtypeStruct((db, L, RET_WIDTH), F32),
                   jax.ShapeDtypeStruct((db, RET_HEADS, RET_DK, RET_DK), F32)],
        compiler_params=_params("arbitrary"), name="ret_sample",
    )(rq, rk, rv, rg, gn, state, *tables)


def _alibi_slopes():
    return 2.0 ** (-8.0 * np.arange(1, MOBA_HEADS + 1) / MOBA_HEADS)


def _moba_prompt_tables():
    slopes = jnp.asarray(_alibi_slopes(), dtype=F32)[:, None, None]
    kk = jnp.arange(MOBA_BLOCK, dtype=F32)[:, None]
    qq = jnp.arange(MOBA_BLOCK, dtype=F32)[None, :]
    dist = (qq - kk)[None]
    past = -(slopes * dist)
    own = jnp.where(dist >= 0, past, -jnp.inf)
    block_step = -(slopes * float(MOBA_BLOCK))
    return past, own, jnp.broadcast_to(block_step, (MOBA_HEADS, 1, MOBA_BLOCK))


def _for_blocks(n, body):
    def two(i, carry):
        body(2 * i)
        body(2 * i + 1)
        return carry

    lax.fori_loop(0, lax.shift_right_logical(n, 1), two, 0)

    @pl.when(lax.bitwise_and(n, 1) == 1)
    def _():
        body(n - 1)


def _moba_prompt_kernel(qT_ref, k_ref, vT_ref, kmean_ref, past_ref, own_ref, step_ref, o_ref,
                        q_sc, s_sc, m_sc, l_sc, acc_sc, term_sc):
    j = pl.program_id(2)
    nb = kmean_ref.shape[0]
    dh_row = lax.broadcasted_iota(jnp.int32, (PAIR, MOBA_BLOCK), 0)
    blk = lax.broadcasted_iota(jnp.int32, (nb, MOBA_BLOCK), 0)
    own_start = pl.multiple_of(j * MOBA_BLOCK, MOBA_BLOCK)
    pair_cols = lambda h: slice((h // 2) * PAIR, (h // 2 + 1) * PAIR)

    for h in range(GROUP_HEADS):
        qT = qT_ref[pair_cols(h), :]
        keep = (dh_row < MOBA_DH) if h % 2 == 0 else (dh_row >= MOBA_DH)
        qh = jnp.where(keep, qT, jnp.zeros_like(qT))
        q_sc[h] = qh
        gate = jnp.dot(kmean_ref[:, pair_cols(h)].astype(BF16), qh, preferred_element_type=F32)
        gate = jnp.where(blk < j, gate, -jnp.inf)
        sel = jnp.zeros(gate.shape, dtype=jnp.bool_)
        for _ in range(MOBA_TOPK):
            top = jnp.max(gate, axis=0, keepdims=True)
            first = jnp.min(jnp.where(gate == top, blk, nb), axis=0, keepdims=True)
            pick = jnp.logical_and(blk == first, top > -jnp.inf)
            sel = jnp.logical_or(sel, pick)
            gate = jnp.where(pick, -jnp.inf, gate)
        term_sc[h] = jnp.where(sel, (j - blk).astype(F32) * step_ref[h], jnp.where(blk == j, 0.0, -jnp.inf))
        s = jnp.dot(k_ref[pl.ds(own_start, MOBA_BLOCK), pair_cols(h)], qh, preferred_element_type=F32) + own_ref[h]
        s_sc[h, j] = s
        m_sc[h] = jnp.max(s, axis=0, keepdims=True)
        l_sc[h] = jnp.zeros_like(l_sc[h])
        acc_sc[h] = jnp.zeros_like(acc_sc[h])

    def scores(jj):
        start = pl.multiple_of(jj * MOBA_BLOCK, MOBA_BLOCK)
        for h in range(GROUP_HEADS):
            s = jnp.dot(k_ref[pl.ds(start, MOBA_BLOCK), pair_cols(h)], q_sc[h], preferred_element_type=F32) + past_ref[h]
            s_sc[h, jj] = s
            m_sc[h] = jnp.maximum(m_sc[h], jnp.max(s, axis=0, keepdims=True) + term_sc[h, pl.ds(jj, 1), :])

    _for_blocks(j, scores)

    def apply_v(jj):
        start = pl.multiple_of(jj * MOBA_BLOCK, MOBA_BLOCK)
        for h in range(GROUP_HEADS):
            pexp = jnp.exp(s_sc[h, jj] - (m_sc[h] - term_sc[h, pl.ds(jj, 1), :]))
            l_sc[h] = l_sc[h] + jnp.sum(pexp, axis=0, keepdims=True)
            vT = vT_ref[h * MOBA_DH:(h + 1) * MOBA_DH, pl.ds(start, MOBA_BLOCK)]
            acc_sc[h] = acc_sc[h] + jnp.dot(vT, pexp.astype(BF16), preferred_element_type=F32)

    _for_blocks(j + 1, apply_v)

    for p in range(GROUP_HEADS // 2):
        outT = jnp.concatenate([acc_sc[2 * p] / l_sc[2 * p], acc_sc[2 * p + 1] / l_sc[2 * p + 1]], axis=0)
        o_ref[:, p * PAIR:(p + 1) * PAIR] = outT.T.astype(BF16)


def _moba_prompt(qT16, k16, vT16, kmean, batch):
    T = k16.shape[0]
    S = T // batch
    nb = S // MOBA_BLOCK
    n_groups = MOBA_HEADS // GROUP_HEADS
    gw = GROUP_HEADS * MOBA_DH
    tables = _moba_prompt_tables()
    head_tile = lambda t: pl.BlockSpec((GROUP_HEADS,) + t.shape[1:], lambda b, g, j: (g, 0, 0))
    return pl.pallas_call(
        _moba_prompt_kernel,
        grid=(batch, n_groups, nb),
        in_specs=[pl.BlockSpec((gw, MOBA_BLOCK), lambda b, g, j: (g, b * nb + j)),
                  pl.BlockSpec((S, gw), lambda b, g, j: (b, g)),
                  pl.BlockSpec((gw, S), lambda b, g, j: (g, b)),
                  pl.BlockSpec((nb, gw), lambda b, g, j: (b, g))]
        + [head_tile(t) for t in tables],
        out_specs=pl.BlockSpec((MOBA_BLOCK, gw), lambda b, g, j: (b * nb + j, g)),
        out_shape=jax.ShapeDtypeStruct((T, MOBA_WIDTH), BF16),
        scratch_shapes=[pltpu.VMEM((GROUP_HEADS, PAIR, MOBA_BLOCK), BF16),
                        pltpu.VMEM((GROUP_HEADS, nb, MOBA_BLOCK, MOBA_BLOCK), F32),
                        pltpu.VMEM((GROUP_HEADS, 1, MOBA_BLOCK), F32), pltpu.VMEM((GROUP_HEADS, 1, MOBA_BLOCK), F32),
                        pltpu.VMEM((GROUP_HEADS, MOBA_DH, MOBA_BLOCK), F32),
                        pltpu.VMEM((GROUP_HEADS, nb, MOBA_BLOCK), F32)],
        compiler_params=_params("arbitrary", "arbitrary", "arbitrary"), name="moba_prompt",
    )(qT16, k16, vT16, kmean, *tables)


def _moba_sample_tables(L, n_pages):
    past_len = n_pages * PAGE_SIZE
    slopes = _alibi_slopes()
    row_slope = np.zeros((N_PAIRS, SAMPLE_ROWS), np.float64)
    row_t = np.zeros((SAMPLE_ROWS,), np.float64)
    for p in range(N_PAIRS):
        row_slope[p, :L] = slopes[2 * p]
        row_slope[p, L:2 * L] = slopes[2 * p + 1]
    row_t[:L] = np.arange(L)
    row_t[L:2 * L] = np.arange(L)
    row_slope = jnp.asarray(row_slope, dtype=F32)[:, :, None]
    q_pos = jnp.asarray(past_len + row_t, dtype=F32)[None, :, None]
    key_pos = jnp.arange(past_len, dtype=F32)[None, None, :]
    past = -(row_slope * (q_pos - key_pos))
    new_pos = jnp.arange(SAMPLE_ROWS, dtype=F32)[None, None, :]
    dist_new = jnp.asarray(row_t, dtype=F32)[None, :, None] - new_pos
    valid = jnp.logical_and(dist_new >= 0, new_pos < L)
    new = jnp.where(valid, -(row_slope * dist_new), -jnp.inf)
    return past, new


def _moba_sample_kernel(pt_ref, q_ref, kn_ref, vn_ref, past_ref, new_ref, *refs, L, n_pages):
    k_pages, v_pages, o_ref = refs[:n_pages], refs[n_pages:2 * n_pages], refs[2 * n_pages]
    del pt_ref
    n_blk = n_pages * PAGE_SIZE // MOBA_BLOCK
    per_blk = MOBA_BLOCK // PAGE_SIZE
    row = lax.broadcasted_iota(jnp.int32, (SAMPLE_ROWS, PAIR), 0)
    lane = lax.broadcasted_iota(jnp.int32, (SAMPLE_ROWS, PAIR), 1)
    keep = jnp.logical_or(jnp.logical_and(row < L, lane < MOBA_DH),
                          jnp.logical_and(jnp.logical_and(row >= L, row < 2 * L), lane >= MOBA_DH))
    for p in range(N_PAIRS):
        cols = slice(p * PAIR, (p + 1) * PAIR)
        q = q_ref[0, :, cols]
        qm = jnp.where(keep, q, jnp.zeros_like(q))
        raw = [jnp.dot(qm, k_pages[s][0, 0, cols, :].astype(BF16), preferred_element_type=F32)
               for s in range(n_pages)]
        gate = []
        for jj in range(n_blk):
            tot = raw[jj * per_blk]
            for t in range(1, per_blk):
                tot = tot + raw[jj * per_blk + t]
            gate.append(jnp.sum(tot, axis=-1, keepdims=True) * (1.0 / MOBA_BLOCK))
        sel = []
        for jj in range(n_blk):
            ahead = jnp.zeros(gate[jj].shape, F32)
            for kk in range(n_blk):
                if kk == jj:
                    continue
                beats = (gate[kk] >= gate[jj]) if kk < jj else (gate[kk] > gate[jj])
                ahead = ahead + jnp.where(beats, 1.0, 0.0)
            sel.append(ahead < float(min(MOBA_TOPK, n_blk)))
        logit = []
        for s in range(n_pages):
            sc = raw[s] + past_ref[p, :, s * PAGE_SIZE:(s + 1) * PAGE_SIZE]
            logit.append(jnp.where(sel[s // per_blk], sc, -jnp.inf))
        kn = kn_ref[0, :, cols]
        s_new = lax.dot_general(qm, kn, NT, preferred_element_type=F32) + new_ref[p]
        m = jnp.max(s_new, axis=-1, keepdims=True)
        for s in range(n_pages):
            m = jnp.maximum(m, jnp.max(logit[s], axis=-1, keepdims=True))
        p_new = jnp.exp(s_new - m)
        denom = jnp.sum(p_new, axis=-1, keepdims=True)
        acc = jnp.dot(p_new.astype(BF16), vn_ref[0, :, cols], preferred_element_type=F32)
        for s in range(n_pages):
            pexp = jnp.exp(logit[s] - m)
            denom = denom + jnp.sum(pexp, axis=-1, keepdims=True)
            vT_pg = v_pages[s][0, 0, cols, :].astype(BF16)
            acc = acc + lax.dot_general(pexp.astype(BF16), vT_pg, NT, preferred_element_type=F32)
        out = acc / denom
        o_ref[0, :, cols] = jnp.where(lane[:L] < MOBA_DH, out[:L], out[L:2 * L])


def _moba_sample(q16, kn16, vn16, cache_kT, cache_vT, page_table, layer, L):
    db, n_pages = page_table.shape
    past, new = _moba_sample_tables(L, n_pages)
    b3 = lambda b, pt: (b, 0, 0)
    row_spec = pl.BlockSpec((1, SAMPLE_ROWS, MOBA_WIDTH), b3)

    def page_spec(s):
        return pl.BlockSpec((1, 1, MOBA_WIDTH, PAGE_SIZE), lambda b, pt: (layer, pt[b, s], 0, 0))

    in_specs = [row_spec, row_spec, row_spec,
                pl.BlockSpec(past.shape, lambda b, pt: (0, 0, 0)), pl.BlockSpec(new.shape, lambda b, pt: (0, 0, 0))]
    in_specs += [page_spec(s) for s in range(n_pages)] * 2
    grid_spec = pltpu.PrefetchScalarGridSpec(
        num_scalar_prefetch=1, grid=(db,), in_specs=in_specs,
        out_specs=pl.BlockSpec((1, L, MOBA_WIDTH), b3))
    return pl.pallas_call(
        functools.partial(_moba_sample_kernel, L=L, n_pages=n_pages),
        grid_spec=grid_spec,
        out_shape=jax.ShapeDtypeStruct((db, L, MOBA_WIDTH), F32),
        compiler_params=_params("arbitrary"), name="moba_sample",
    )(page_table, q16, kn16, vn16, past, new, *([cache_kT] * n_pages), *([cache_vT] * n_pages))


def _merge_kernel(x_ref, or_ref, om_ref, wg_ref, bg_ref, wr_ref, wm_ref, wo_ref, g_ref, b_ref, o_ref):
    x = x_ref[...]
    gates = _sigmoid(jnp.dot(x.astype(BF16), wg_ref[...], preferred_element_type=F32) + bg_ref[...])
    br = jnp.dot(or_ref[...], wr_ref[...], preferred_element_type=F32)
    bm = jnp.dot(om_ref[...], wm_ref[...], preferred_element_type=F32)
    merged = gates[:, :D_MODEL] * br + gates[:, D_MODEL:] * bm
    y = jnp.dot(merged.astype(BF16), wo_ref[...], preferred_element_type=F32)
    o_ref[...] = _layer_norm_rows(ALPHA * x + y, g_ref[...], b_ref[...])


def _merge(x, o_r, o_m, wg16, bg, wr16, wm16, wo16, g, b, *, tm):
    T = x.shape[0]
    row = lambda i: (i, 0)
    consts = [wg16, bg, wr16, wm16, wo16, g, b]
    return pl.pallas_call(
        _merge_kernel,
        grid=(T // tm,),
        in_specs=[pl.BlockSpec((tm, D_MODEL), row), pl.BlockSpec((tm, RET_WIDTH), row),
                  pl.BlockSpec((tm, MOBA_WIDTH), row)] + [_const_spec(c.shape) for c in consts],
        out_specs=pl.BlockSpec((tm, D_MODEL), row),
        out_shape=jax.ShapeDtypeStruct((T, D_MODEL), F32),
        compiler_params=_params("arbitrary"), name="merge",
    )(x, o_r, o_m, *consts)


def _moe_kernel(x_ref, wrh_ref, wrl_ref, br_ref, weg_ref, weu_ref, wed_ref, g_ref, b_ref, o_ref):
    x = x_ref[...]
    tm = x.shape[0]
    hi = x.astype(BF16)
    lo = (x - hi.astype(F32)).astype(BF16)
    logit = (jnp.dot(hi, wrh_ref[...], preferred_element_type=F32)
             + (jnp.dot(hi, wrl_ref[...], preferred_element_type=F32)
                + jnp.dot(lo, wrh_ref[...], preferred_element_type=F32))) + br_ref[...]
    lane = lax.broadcasted_iota(jnp.int32, (tm, ROUTER_LANES), 1)
    neg = -jnp.inf
    is_group = lane < N_GROUPS
    gl = jnp.where(is_group, logit, neg)
    gmax = jnp.max(gl, axis=-1, keepdims=True)
    gidx = jnp.min(jnp.where(gl == gmax, lane, ROUTER_LANES), axis=-1, keepdims=True)
    g_w = 1.0 / jnp.sum(jnp.exp(gl - gmax), axis=-1, keepdims=True)
    first = N_GROUPS + EXPERTS_PER_GROUP * gidx
    in_group = jnp.logical_and(lane >= first, lane < first + EXPERTS_PER_GROUP)
    el = jnp.where(in_group, logit, neg)
    e1 = jnp.max(el, axis=-1, keepdims=True)
    i1 = jnp.min(jnp.where(el == e1, lane, ROUTER_LANES), axis=-1, keepdims=True)
    el2 = jnp.where(lane == i1, neg, el)
    e2 = jnp.max(el2, axis=-1, keepdims=True)
    i2 = jnp.min(jnp.where(el2 == e2, lane, ROUTER_LANES), axis=-1, keepdims=True)
    t = jnp.exp(e2 - e1)
    w1 = g_w / (1.0 + t)
    w2 = g_w * t / (1.0 + t)
    comb = jnp.where(lane == i1, w1, 0.0) + jnp.where(lane == i2, w2, 0.0)
    acc = jnp.zeros((tm, D_MODEL), F32)
    for g in range(N_GROUPS):
        hg = jnp.dot(hi, weg_ref[g], preferred_element_type=F32)
        hu = jnp.dot(hi, weu_ref[g], preferred_element_type=F32)
        cexp = jnp.concatenate(
            [jnp.broadcast_to(comb[:, N_GROUPS + g * EXPERTS_PER_GROUP + e:N_GROUPS + g * EXPERTS_PER_GROUP + e + 1],
                              (tm, EXPERT_HIDDEN)) for e in range(EXPERTS_PER_GROUP)], axis=1)
        hid = (hg * _sigmoid(hg)) * hu
        acc = acc + jnp.dot((hid * cexp).astype(BF16), wed_ref[g], preferred_element_type=F32)
    o_ref[...] = _layer_norm_rows(ALPHA * x + acc, g_ref[...], b_ref[...])


def _moe(x, wr_hi, wr_lo, br, weg16, weu16, wed16, g, b, *, tm):
    T = x.shape[0]
    row = lambda i: (i, 0)
    consts = [wr_hi, wr_lo, br, weg16, weu16, wed16, g, b]
    return pl.pallas_call(
        _moe_kernel,
        grid=(T // tm,),
        in_specs=[pl.BlockSpec((tm, D_MODEL), row)] + [_const_spec(c.shape) for c in consts],
        out_specs=pl.BlockSpec((tm, D_MODEL), row),
        out_shape=jax.ShapeDtypeStruct((T, D_MODEL), F32),
        compiler_params=_params("arbitrary"), name="moe",
    )(x, *consts)


def _layer_weights(l, w_in, b_merge, gn_g, w_br_ret, w_br_moba, w_out, ln1_g, ln1_b, w_router_group,
                   b_router_group, w_router_expert, b_router_expert, w_exp_gate, w_exp_up, w_exp_down,
                   ln2_g, ln2_b):
    w = w_in[l]
    q0 = 4 * RET_WIDTH
    n_exp = N_GROUPS * EXPERTS_PER_GROUP
    w_r = jnp.concatenate(
        [w_router_group[l], w_router_expert[l].transpose(1, 0, 2).reshape(D_MODEL, n_exp),
         jnp.zeros((D_MODEL, ROUTER_LANES - N_GROUPS - n_exp), F32)], axis=1)
    w_r_hi = w_r.astype(BF16)
    w_r_lo = (w_r - w_r_hi.astype(F32)).astype(BF16)
    b_r = jnp.concatenate([b_router_group[l], b_router_expert[l].reshape(n_exp),
                           jnp.zeros((ROUTER_LANES - N_GROUPS - n_exp,), F32)])[None, :]
    by_group = lambda t: (t.reshape(N_GROUPS, EXPERTS_PER_GROUP, D_MODEL, EXPERT_HIDDEN)
                          .transpose(0, 2, 1, 3).reshape(N_GROUPS, D_MODEL, GROUP_HIDDEN).astype(BF16))
    return dict(
        w_proj=w[:, :N_PROJ].astype(BF16),
        w_ret=w[:, :q0].astype(BF16),
        w_mobaT=w[:, q0:N_PROJ].T.reshape(3, MOBA_WIDTH, D_MODEL).astype(BF16),
        w_gate=w[:, N_PROJ:].astype(BF16), b_gate=b_merge[l][None, :],
        gn=gn_g[l][None, :],
        w_br_ret=w_br_ret[l].astype(BF16), w_br_moba=w_br_moba[l].astype(BF16), w_out=w_out[l].astype(BF16),
        ln1_g=ln1_g[l][None, :], ln1_b=ln1_b[l][None, :],
        w_r_hi=w_r_hi, w_r_lo=w_r_lo, b_r=b_r,
        w_eg=by_group(w_exp_gate[l]), w_eu=by_group(w_exp_up[l]),
        w_ed=w_exp_down[l].reshape(N_GROUPS, GROUP_HIDDEN, D_MODEL).astype(BF16),
        ln2_g=ln2_g[l][None, :], ln2_b=ln2_b[l][None, :])


def _pad_rows(t, rows):
    return jnp.pad(t, ((0, 0), (0, rows - t.shape[1]), (0, 0)))


def kernel(x_prompt, x_sample, cache_k, cache_v, state_ret, page_table, w_in, b_merge, gn_g, w_br_ret, w_br_moba, w_out, ln1_g, ln1_b, w_router_group, b_router_group, w_router_expert, b_router_expert, w_exp_gate, w_exp_up, w_exp_down, ln2_g, ln2_b):
    B, S, _ = x_prompt.shape
    DB, L, _ = x_sample.shape
    Tp, Ts = B * S, DB * L
    depth = w_in.shape[0]
    n_pool = cache_k.shape[1]
    page_major = lambda c: c.transpose(0, 1, 3, 4, 2).reshape(depth, n_pool, MOBA_WIDTH, PAGE_SIZE)
    cache_kT, cache_vT = page_major(cache_k), page_major(cache_v)
    xp = x_prompt.reshape(Tp, D_MODEL)
    xs = x_sample.reshape(Ts, D_MODEL)
    tm_p = min(512, Tp)
    tm_s = min(256, Ts)
    outs = [[] for _ in range(6)]
    for l in range(depth):
        W = _layer_weights(l, w_in, b_merge, gn_g, w_br_ret, w_br_moba, w_out, ln1_g, ln1_b, w_router_group,
                           b_router_group, w_router_expert, b_router_expert, w_exp_gate, w_exp_up, w_exp_down,
                           ln2_g, ln2_b)
        (rq, rk, rv, rg, kT_p, vT_p, k16, qT16, vT16, kmean) = _in_proj(
            xp, W["w_ret"], W["w_mobaT"], batch=B, tm=tm_p)
        o_r, s_p = _ret_prompt(rq, rk, rv, rg, W["gn"], B)
        o_m = _moba_prompt(qT16, k16, vT16, kmean, B)
        x1 = _merge(xp, o_r, o_m, W["w_gate"], W["b_gate"], W["w_br_ret"], W["w_br_moba"], W["w_out"],
                    W["ln1_g"], W["ln1_b"], tm=tm_p)
        xp = _moe(x1, W["w_r_hi"], W["w_r_lo"], W["b_r"], W["w_eg"], W["w_eu"], W["w_ed"],
                  W["ln2_g"], W["ln2_b"], tm=tm_p)
        (rq, rk, rv, rg, mq, k_s, v_s) = _in_proj(xs, W["w_proj"], tm=tm_s)
        r3 = lambda t: t.reshape(DB, L, t.shape[-1])
        o_r, s_s = _ret_sample(_pad_rows(r3(rq), SAMPLE_ROWS), _pad_rows(r3(rk), SAMPLE_ROWS),
                               _pad_rows(r3(rv), SAMPLE_ROWS), r3(rg), W["gn"], state_ret, l)
        mq3 = r3(mq)
        q16 = _pad_rows(jnp.concatenate([mq3, mq3], axis=1), SAMPLE_ROWS)
        o_m = _moba_sample(q16, _pad_rows(r3(k_s).astype(BF16), SAMPLE_ROWS),
                           _pad_rows(r3(v_s).astype(BF16), SAMPLE_ROWS),
                           cache_kT, cache_vT, page_table, l, L)
        x1 = _merge(xs, o_r.reshape(Ts, RET_WIDTH).astype(BF16), o_m.reshape(Ts, MOBA_WIDTH).astype(BF16),
                    W["w_gate"], W["b_gate"], W["w_br_ret"], W["w_br_moba"], W["w_out"],
                    W["ln1_g"], W["ln1_b"], tm=tm_s)
        xs = _moe(x1, W["w_r_hi"], W["w_r_lo"], W["b_r"], W["w_eg"], W["w_eu"], W["w_ed"],
                  W["ln2_g"], W["ln2_b"], tm=tm_s)
        for lst, val in zip(outs, (kT_p, vT_p, s_p,
                                   k_s.reshape(DB, L, MOBA_HEADS, MOBA_DH), v_s.reshape(DB, L, MOBA_HEADS, MOBA_DH), s_s)):
            lst.append(val)
    kTp, vTp, sp, ksm, vsm, ssm = (jnp.stack(o) for o in outs)
    token_major = lambda t: t.reshape(depth, B, MOBA_HEADS, MOBA_DH, S).transpose(0, 1, 4, 2, 3)
    return (xp.reshape(B, S, D_MODEL), xs.reshape(DB, L, D_MODEL), token_major(kTp), token_major(vTp), sp,
            ksm, vsm, ssm)
```

```python
import functools

import numpy as np
import jax
import jax.numpy as jnp
from jax import lax
from jax.experimental import pallas as pl
from jax.experimental.pallas import tpu as pltpu

F32 = jnp.float32
BF16 = jnp.bfloat16

D_MODEL = 1024
RET_HEADS = 4
RET_DK = 128
RET_WIDTH = RET_HEADS * RET_DK
RET_CHUNK = 128
MOBA_HEADS = 8
MOBA_DH = 64
MOBA_WIDTH = MOBA_HEADS * MOBA_DH
MOBA_BLOCK = 256
MOBA_TOPK = 3
PAGE_SIZE = 128
N_GROUPS = 4
EXPERTS_PER_GROUP = 8
EXPERT_HIDDEN = 128
GROUP_HIDDEN = EXPERTS_PER_GROUP * EXPERT_HIDDEN
DEPTH = 2
ALPHA = (2 * DEPTH) ** 0.25
LN_EPS = 1e-5
N_PROJ = 4 * RET_WIDTH + 3 * MOBA_WIDTH

LANES = 128
PAIR = 2 * MOBA_DH
N_PAIRS = MOBA_HEADS // 2
GROUP_HEADS = 4
DENOM_ROWS = 16
LOG2E = 1.4426950408889634
SAMPLE_ROWS = 16
ROUTER_LANES = 128
VMEM_LIMIT = 56 * 1024 * 1024

NT = (((1,), (1,)), ((), ()))
TN = (((0,), (0,)), ((), ()))


def _params(*sem):
    return pltpu.CompilerParams(dimension_semantics=sem, vmem_limit_bytes=VMEM_LIMIT)


def _const_spec(shape):
    nd = len(shape)
    return pl.BlockSpec(shape, lambda *_: (0,) * nd, pipeline_mode=pl.Buffered(1))


def _layer_norm_rows(z, g, b):
    mu = jnp.mean(z, axis=-1, keepdims=True)
    zc = z - mu
    var = jnp.mean(zc * zc, axis=-1, keepdims=True)
    return zc * lax.rsqrt(var + LN_EPS) * g + b


def _sigmoid(x):
    return 1.0 / (1.0 + jnp.exp(-x))


def _in_proj_kernel(x_ref, w_ref, *refs, transposed, tm):
    xb = x_ref[...].astype(BF16)

    def proj(c):
        return jnp.dot(xb, w_ref[:, c * RET_WIDTH:(c + 1) * RET_WIDTH], preferred_element_type=F32)

    if transposed:
        wT_ref, rq_ref, rk_ref, rv_ref, rg_ref, kT_ref, vT_ref, k16_ref, qT16_ref, vT16_ref, kmean_ref = refs
        projT = lambda c: lax.dot_general(wT_ref[c], xb, NT, preferred_element_type=F32)
        qT16_ref[...] = (projT(0) * MOBA_DH ** -0.5).astype(BF16)
        kT = projT(1)
        kT_ref[0] = kT
        k = kT.T
        k16_ref[...] = k.astype(BF16)
        nb = tm // MOBA_BLOCK
        kmean_ref[0] = jnp.mean(k.reshape(nb, MOBA_BLOCK, MOBA_WIDTH), axis=1)
        vT = projT(2)
        vT_ref[0] = vT
        vT16_ref[...] = vT.astype(BF16)
    else:
        rq_ref, rk_ref, rv_ref, rg_ref, mq_ref, k_ref, v_ref = refs
        mq_ref[...] = (proj(4) * MOBA_DH ** -0.5).astype(BF16)
        k_ref[...] = proj(5)
        v_ref[...] = proj(6)
    rq_ref[...] = proj(0).astype(BF16)
    rk_ref[...] = (proj(1) * RET_DK ** -0.5).astype(BF16)
    rv_ref[...] = proj(2).astype(BF16)
    rg_ref[...] = proj(3)


def _in_proj(x, w16, wT16=None, *, batch=1, tm):
    T = x.shape[0]
    transposed = wT16 is not None
    row = lambda i: (i, 0)
    tile = lambda: pl.BlockSpec((tm, RET_WIDTH), row)
    in_specs = [pl.BlockSpec((tm, D_MODEL), row), _const_spec(w16.shape)]
    args = [x, w16]
    out_shape = [jax.ShapeDtypeStruct((T, RET_WIDTH), BF16)] * 3 + [jax.ShapeDtypeStruct((T, RET_WIDTH), F32)]
    out_specs = [tile() for _ in range(4)]
    if transposed:
        S = T // batch
        per_b = S // tm
        in_specs.append(_const_spec(wT16.shape))
        args.append(wT16)
        col = lambda i: (0, i)
        by_batch = lambda i: (i // per_b, 0, i % per_b)
        out_shape += [jax.ShapeDtypeStruct((batch, MOBA_WIDTH, S), F32)] * 2 + [
            jax.ShapeDtypeStruct((T, MOBA_WIDTH), BF16),
            jax.ShapeDtypeStruct((MOBA_WIDTH, T), BF16), jax.ShapeDtypeStruct((MOBA_WIDTH, T), BF16),
            jax.ShapeDtypeStruct((T // tm, tm // MOBA_BLOCK, MOBA_WIDTH), F32)]
        out_specs += [pl.BlockSpec((1, MOBA_WIDTH, tm), by_batch), pl.BlockSpec((1, MOBA_WIDTH, tm), by_batch),
                      tile(), pl.BlockSpec((MOBA_WIDTH, tm), col), pl.BlockSpec((MOBA_WIDTH, tm), col),
                      pl.BlockSpec((1, tm // MOBA_BLOCK, MOBA_WIDTH), lambda i: (i, 0, 0))]
    else:
        out_shape += [jax.ShapeDtypeStruct((T, MOBA_WIDTH), BF16)] + [jax.ShapeDtypeStruct((T, MOBA_WIDTH), F32)] * 2
        out_specs += [tile() for _ in range(3)]
    outs = pl.pallas_call(
        functools.partial(_in_proj_kernel, transposed=transposed, tm=tm),
        grid=(T // tm,), in_specs=in_specs, out_specs=out_specs, out_shape=out_shape,
        compiler_params=_params("arbitrary"), name="in_proj",
    )(*args)
    if transposed:
        outs = list(outs)
        outs[-1] = outs[-1].reshape(T // MOBA_BLOCK, MOBA_WIDTH)
    return outs


def _ret_tables(L, rows):
    log_g = jnp.log(jnp.asarray(1.0 - 2.0 ** (-5.0 - np.arange(RET_HEADS)), dtype=F32))
    idx = jnp.arange(L, dtype=F32)
    diff = idx[:, None] - idx[None, :]
    decay = jnp.where(diff >= 0, jnp.exp(log_g[:, None, None] * jnp.maximum(diff, 0.0)), 0.0)
    qdec = jnp.exp(log_g[:, None] * (idx + 1.0))
    kdec = jnp.exp(log_g[:, None] * (L - 1.0 - idx))
    g_chunk = jnp.exp(log_g * L)
    pad = rows - L
    decay = jnp.pad(decay, ((0, 0), (0, pad), (0, pad)))
    lanes = lambda t: jnp.broadcast_to(jnp.pad(t, ((0, 0), (0, pad)))[:, :, None], (RET_HEADS, rows, LANES))
    return decay, lanes(qdec), lanes(kdec), jnp.broadcast_to(g_chunk[:, None, None], (RET_HEADS, 1, LANES))


def _ret_head(q, k, v, state, decay, qdec, kdec, g_chunk):
    scores = lax.dot_general(q, k, NT, preferred_element_type=F32) * decay
    inner = jnp.dot(scores.astype(BF16), v, preferred_element_type=F32)
    q_dec = (q.astype(F32) * qdec).astype(BF16)
    cross = jnp.dot(q_dec, state.astype(BF16), preferred_element_type=F32)
    k_dec = (k.astype(F32) * kdec).astype(BF16)
    new_state = state * g_chunk + lax.dot_general(k_dec, v, TN, preferred_element_type=F32)
    return inner + cross, new_state


def _ret_gate(o, rg, gn):
    mu = jnp.mean(o, axis=-1, keepdims=True)
    oc = o - mu
    var = jnp.mean(oc * oc, axis=-1, keepdims=True)
    return (rg * _sigmoid(rg)) * (oc * lax.rsqrt(var + LN_EPS) * gn)


def _ret_prompt_kernel(q_ref, k_ref, v_ref, rg_ref, gn_ref, decay_ref, qdec_ref, kdec_ref, gc_ref,
                       o_ref, state_ref):
    @pl.when(pl.program_id(0) == 0)
    def _():
        state_ref[...] = jnp.zeros_like(state_ref)

    for b in range(q_ref.shape[0]):
        for h in range(RET_HEADS):
            sl = slice(h * RET_DK, (h + 1) * RET_DK)
            o, new_state = _ret_head(q_ref[b, :, sl], k_ref[b, :, sl], v_ref[b, :, sl], state_ref[b, h],
                                     decay_ref[h], qdec_ref[h], kdec_ref[h], gc_ref[h])
            state_ref[b, h] = new_state
            o_ref[b, :, sl] = _ret_gate(o, rg_ref[b, :, sl], gn_ref[:, sl]).astype(BF16)


def _ret_prompt(rq, rk, rv, rg, gn, batch):
    T = rq.shape[0]
    S = T // batch
    tables = _ret_tables(RET_CHUNK, RET_CHUNK)
    by_batch = lambda t: t.reshape(batch, S, RET_WIDTH)
    tile = pl.BlockSpec((batch, RET_CHUNK, RET_WIDTH), lambda c: (0, c, 0))
    o_r, state = pl.pallas_call(
        _ret_prompt_kernel,
        grid=(S // RET_CHUNK,),
        in_specs=[tile, tile, tile, tile, _const_spec(gn.shape)] + [_const_spec(t.shape) for t in tables],
        out_specs=[tile, pl.BlockSpec((batch, RET_HEADS, RET_DK, RET_DK), lambda c: (0, 0, 0, 0))],
        out_shape=[jax.ShapeDtypeStruct((batch, S, RET_WIDTH), BF16),
                   jax.ShapeDtypeStruct((batch, RET_HEADS, RET_DK, RET_DK), F32)],
        compiler_params=_params("arbitrary"), name="ret_prompt",
    )(by_batch(rq), by_batch(rk), by_batch(rv), by_batch(rg), gn, *tables)
    return o_r.reshape(T, RET_WIDTH), state


def _ret_sample_kernel(q_ref, k_ref, v_ref, rg_ref, gn_ref, s_ref, decay_ref, qdec_ref, kdec_ref, gc_ref,
                       o_ref, snew_ref, *, bt, L):
    for bi in range(bt):
        for h in range(RET_HEADS):
            sl = slice(h * RET_DK, (h + 1) * RET_DK)
            o, new_state = _ret_head(q_ref[bi, :, sl], k_ref[bi, :, sl], v_ref[bi, :, sl], s_ref[0, bi, h],
                                     decay_ref[h], qdec_ref[h], kdec_ref[h], gc_ref[h])
            snew_ref[bi, h] = new_state
            o_ref[bi, :, sl] = _ret_gate(o[:L], rg_ref[bi, :, sl], gn_ref[:, sl])


def _ret_sample(rq, rk, rv, rg, gn, state, layer, *, bt=8):
    db, L = rg.shape[0], rg.shape[1]
    tables = _ret_tables(L, SAMPLE_ROWS)
    b3 = lambda i: (i, 0, 0)
    qkv = pl.BlockSpec((bt, SAMPLE_ROWS, RET_WIDTH), b3)
    st = pl.BlockSpec((bt, RET_HEADS, RET_DK, RET_DK), lambda i: (i, 0, 0, 0))
    st_in = pl.BlockSpec((1, bt, RET_HEADS, RET_DK, RET_DK), lambda i: (layer, i, 0, 0, 0))
    return pl.pallas_call(
        functools.partial(_ret_sample_kernel, bt=bt, L=L),
        grid=(db // bt,),
        in_specs=[qkv, qkv, qkv, pl.BlockSpec((bt, L, RET_WIDTH), b3), _const_spec(gn.shape), st_in]
        + [_const_spec(t.shape) for t in tables],
        out_specs=[pl.BlockSpec((bt, L, RET_WIDTH), b3), st],
        out_shape=[jax.ShapeDtypeStruct((db, L, RET_WIDTH), F32),
                   jax.ShapeDtypeStruct((db, RET_HEADS, RET_DK, RET_DK), F32)],
        compiler_params=_params("arbitrary"), name="ret_sample",
    )(rq, rk, rv, rg, gn, state, *tables)


def _alibi_slopes():
    return 2.0 ** (-8.0 * np.arange(1, MOBA_HEADS + 1) / MOBA_HEADS)


def _moba_prompt_tables():
    slopes = jnp.asarray(_alibi_slopes(), dtype=F32)[:, None, None]
    kk = jnp.arange(MOBA_BLOCK, dtype=F32)[:, None]
    qq = jnp.arange(MOBA_BLOCK, dtype=F32)[None, :]
    dist = (qq - kk)[None]
    past = -(slopes * dist) * LOG2E
    own = jnp.where(dist >= 0, past, -jnp.inf)
    block_step = -(slopes * float(MOBA_BLOCK)) * LOG2E
    return past, own, jnp.broadcast_to(block_step, (MOBA_HEADS, 1, MOBA_BLOCK))


def _for_blocks(n, body):
    def two(i, carry):
        body(2 * i)
        body(2 * i + 1)
        return carry

    lax.fori_loop(0, lax.shift_right_logical(n, 1), two, 0)

    @pl.when(lax.bitwise_and(n, 1) == 1)
    def _():
        body(n - 1)


def _moba_prompt_kernel(qT_ref, k_ref, vT_ref, kmean_ref, past_ref, own_ref, step_ref, o_ref,
                        q_sc, s_sc, m_sc, acc_sc, term_sc):
    j = pl.program_id(2)
    nb = kmean_ref.shape[0]
    dh_row = lax.broadcasted_iota(jnp.int32, (PAIR, MOBA_BLOCK), 0)
    blk = lax.broadcasted_iota(jnp.int32, (nb, MOBA_BLOCK), 0)
    own_start = pl.multiple_of(j * MOBA_BLOCK, MOBA_BLOCK)
    pair_cols = lambda h: slice((h // 2) * PAIR, (h // 2 + 1) * PAIR)

    for h in range(GROUP_HEADS):
        qT = qT_ref[pair_cols(h), :]
        keep = (dh_row < MOBA_DH) if h % 2 == 0 else (dh_row >= MOBA_DH)
        qh = jnp.where(keep, qT, jnp.zeros_like(qT))
        q_sc[h] = qh
        gate = jnp.dot(kmean_ref[:, pair_cols(h)].astype(BF16), qh, preferred_element_type=F32)
        gate = jnp.where(blk < j, gate, -jnp.inf)
        sel = jnp.zeros(gate.shape, dtype=jnp.bool_)
        for _ in range(MOBA_TOPK):
            top = jnp.max(gate, axis=0, keepdims=True)
            first = jnp.min(jnp.where(gate == top, blk, nb), axis=0, keepdims=True)
            pick = jnp.logical_and(blk == first, top > -jnp.inf)
            sel = jnp.logical_or(sel, pick)
            gate = jnp.where(pick, -jnp.inf, gate)
        term_sc[h] = jnp.where(sel, (j - blk).astype(F32) * step_ref[h], jnp.where(blk == j, 0.0, -jnp.inf))
        s = jnp.dot(k_ref[pl.ds(own_start, MOBA_BLOCK), pair_cols(h)], qh, preferred_element_type=F32)
        s = s * LOG2E + own_ref[h]
        s_sc[h, j] = s
        m_sc[h] = jnp.max(s, axis=0, keepdims=True)
        acc_sc[h] = jnp.zeros_like(acc_sc[h])

    def scores(jj):
        start = pl.multiple_of(jj * MOBA_BLOCK, MOBA_BLOCK)
        for h in range(GROUP_HEADS):
            s = jnp.dot(k_ref[pl.ds(start, MOBA_BLOCK), pair_cols(h)], q_sc[h], preferred_element_type=F32)
            s = s * LOG2E + past_ref[h]
            s_sc[h, jj] = s
            m_sc[h] = jnp.maximum(m_sc[h], jnp.max(s, axis=0, keepdims=True) + term_sc[h, pl.ds(jj, 1), :])

    _for_blocks(j, scores)

    ones_rows = jnp.ones((DENOM_ROWS, MOBA_BLOCK), BF16)

    def apply_v(jj):
        start = pl.multiple_of(jj * MOBA_BLOCK, MOBA_BLOCK)
        for h in range(GROUP_HEADS):
            pexp = jnp.exp2(s_sc[h, jj] - (m_sc[h] - term_sc[h, pl.ds(jj, 1), :]))
            vT = jnp.concatenate([vT_ref[h * MOBA_DH:(h + 1) * MOBA_DH, pl.ds(start, MOBA_BLOCK)], ones_rows], axis=0)
            acc_sc[h] = acc_sc[h] + jnp.dot(vT, pexp.astype(BF16), preferred_element_type=F32)

    _for_blocks(j + 1, apply_v)

    def head_out(h):
        acc = acc_sc[h]
        return acc[:MOBA_DH] / acc[MOBA_DH:MOBA_DH + 1]

    for p in range(GROUP_HEADS // 2):
        outT = jnp.concatenate([head_out(2 * p), head_out(2 * p + 1)], axis=0)
        o_ref[:, p * PAIR:(p + 1) * PAIR] = outT.T.astype(BF16)


def _moba_prompt(qT16, k16, vT16, kmean, batch):
    T = k16.shape[0]
    S = T // batch
    nb = S // MOBA_BLOCK
    n_groups = MOBA_HEADS // GROUP_HEADS
    gw = GROUP_HEADS * MOBA_DH
    tables = _moba_prompt_tables()
    head_tile = lambda t: pl.BlockSpec((GROUP_HEADS,) + t.shape[1:], lambda b, g, j: (g, 0, 0))
    return pl.pallas_call(
        _moba_prompt_kernel,
        grid=(batch, n_groups, nb),
        in_specs=[pl.BlockSpec((gw, MOBA_BLOCK), lambda b, g, j: (g, b * nb + j)),
                  pl.BlockSpec((S, gw), lambda b, g, j: (b, g)),
                  pl.BlockSpec((gw, S), lambda b, g, j: (g, b)),
                  pl.BlockSpec((nb, gw), lambda b, g, j: (b, g))]
        + [head_tile(t) for t in tables],
        out_specs=pl.BlockSpec((MOBA_BLOCK, gw), lambda b, g, j: (b * nb + j, g)),
        out_shape=jax.ShapeDtypeStruct((T, MOBA_WIDTH), BF16),
        scratch_shapes=[pltpu.VMEM((GROUP_HEADS, PAIR, MOBA_BLOCK), BF16),
                        pltpu.VMEM((GROUP_HEADS, nb, MOBA_BLOCK, MOBA_BLOCK), F32),
                        pltpu.VMEM((GROUP_HEADS, 1, MOBA_BLOCK), F32),
                        pltpu.VMEM((GROUP_HEADS, MOBA_DH + DENOM_ROWS, MOBA_BLOCK), F32),
                        pltpu.VMEM((GROUP_HEADS, nb, MOBA_BLOCK), F32)],
        compiler_params=_params("arbitrary", "arbitrary", "arbitrary"), name="moba_prompt",
    )(qT16, k16, vT16, kmean, *tables)


def _moba_sample_tables(L, n_pages):
    past_len = n_pages * PAGE_SIZE
    slopes = _alibi_slopes()
    row_slope = np.zeros((N_PAIRS, SAMPLE_ROWS), np.float64)
    row_t = np.zeros((SAMPLE_ROWS,), np.float64)
    for p in range(N_PAIRS):
        row_slope[p, :L] = slopes[2 * p]
        row_slope[p, L:2 * L] = slopes[2 * p + 1]
    row_t[:L] = np.arange(L)
    row_t[L:2 * L] = np.arange(L)
    row_slope = jnp.asarray(row_slope, dtype=F32)[:, :, None]
    q_pos = jnp.asarray(past_len + row_t, dtype=F32)[None, :, None]
    key_pos = jnp.arange(past_len, dtype=F32)[None, None, :]
    past = -(row_slope * (q_pos - key_pos))
    new_pos = jnp.arange(SAMPLE_ROWS, dtype=F32)[None, None, :]
    dist_new = jnp.asarray(row_t, dtype=F32)[None, :, None] - new_pos
    valid = jnp.logical_and(dist_new >= 0, new_pos < L)
    new = jnp.where(valid, -(row_slope * dist_new), -jnp.inf)
    return past, new


def _moba_sample_kernel(pt_ref, q_ref, kn_ref, vn_ref, past_ref, new_ref, *refs, L, n_pages):
    k_pages, v_pages, o_ref = refs[:n_pages], refs[n_pages:2 * n_pages], refs[2 * n_pages]
    del pt_ref
    n_blk = n_pages * PAGE_SIZE // MOBA_BLOCK
    per_blk = MOBA_BLOCK // PAGE_SIZE
    row = lax.broadcasted_iota(jnp.int32, (SAMPLE_ROWS, PAIR), 0)
    lane = lax.broadcasted_iota(jnp.int32, (SAMPLE_ROWS, PAIR), 1)
    keep = jnp.logical_or(jnp.logical_and(row < L, lane < MOBA_DH),
                          jnp.logical_and(jnp.logical_and(row >= L, row < 2 * L), lane >= MOBA_DH))
    for p in range(N_PAIRS):
        cols = slice(p * PAIR, (p + 1) * PAIR)
        q = q_ref[0, :, cols]
        qm = jnp.where(keep, q, jnp.zeros_like(q))
        kT = jnp.concatenate([k_pages[s][0, 0, cols, :] for s in range(n_pages)], axis=1).astype(BF16)
        raw = jnp.dot(qm, kT, preferred_element_type=F32)
        blk_lanes = lambda jj: slice(jj * MOBA_BLOCK, (jj + 1) * MOBA_BLOCK)
        gate = [jnp.sum(raw[:, blk_lanes(jj)], axis=-1, keepdims=True) * (1.0 / MOBA_BLOCK) for jj in range(n_blk)]
        sel = []
        for jj in range(n_blk):
            ahead = jnp.zeros(gate[jj].shape, F32)
            for kk in range(n_blk):
                if kk == jj:
                    continue
                beats = (gate[kk] >= gate[jj]) if kk < jj else (gate[kk] > gate[jj])
                ahead = ahead + jnp.where(beats, 1.0, 0.0)
            sel.append(ahead < float(min(MOBA_TOPK, n_blk)))
        logit = jnp.concatenate(
            [jnp.where(sel[jj], raw[:, blk_lanes(jj)] + past_ref[p, :, blk_lanes(jj)], -jnp.inf)
             for jj in range(n_blk)], axis=1)
        kn = kn_ref[0, :, cols]
        s_new = lax.dot_general(qm, kn, NT, preferred_element_type=F32) + new_ref[p]
        m = jnp.maximum(jnp.max(s_new, axis=-1, keepdims=True), jnp.max(logit, axis=-1, keepdims=True))
        p_new = jnp.exp(s_new - m)
        pexp = jnp.exp(logit - m)
        denom = jnp.sum(p_new, axis=-1, keepdims=True) + jnp.sum(pexp, axis=-1, keepdims=True)
        vT = jnp.concatenate([v_pages[s][0, 0, cols, :] for s in range(n_pages)], axis=1).astype(BF16)
        acc = (jnp.dot(p_new.astype(BF16), vn_ref[0, :, cols], preferred_element_type=F32)
               + lax.dot_general(pexp.astype(BF16), vT, NT, preferred_element_type=F32))
        out = acc / denom
        o_ref[0, :, cols] = jnp.where(lane[:L] < MOBA_DH, out[:L], out[L:2 * L])


def _moba_sample(q16, kn16, vn16, cache_kT, cache_vT, page_table, layer, L):
    db, n_pages = page_table.shape
    past, new = _moba_sample_tables(L, n_pages)
    b3 = lambda b, pt: (b, 0, 0)
    row_spec = pl.BlockSpec((1, SAMPLE_ROWS, MOBA_WIDTH), b3)

    def page_spec(s):
        return pl.BlockSpec((1, 1, MOBA_WIDTH, PAGE_SIZE), lambda b, pt: (layer, pt[b, s], 0, 0))

    in_specs = [row_spec, row_spec, row_spec,
                pl.BlockSpec(past.shape, lambda b, pt: (0, 0, 0)), pl.BlockSpec(new.shape, lambda b, pt: (0, 0, 0))]
    in_specs += [page_spec(s) for s in range(n_pages)] * 2
    grid_spec = pltpu.PrefetchScalarGridSpec(
        num_scalar_prefetch=1, grid=(db,), in_specs=in_specs,
        out_specs=pl.BlockSpec((1, L, MOBA_WIDTH), b3))
    return pl.pallas_call(
        functools.partial(_moba_sample_kernel, L=L, n_pages=n_pages),
        grid_spec=grid_spec,
        out_shape=jax.ShapeDtypeStruct((db, L, MOBA_WIDTH), F32),
        compiler_params=_params("arbitrary"), name="moba_sample",
    )(page_table, q16, kn16, vn16, past, new, *([cache_kT] * n_pages), *([cache_vT] * n_pages))


def _merge_kernel(x_ref, or_ref, om_ref, wg_ref, bg_ref, wr_ref, wm_ref, wo_ref, g_ref, b_ref, o_ref):
    x = x_ref[...]
    gates = _sigmoid(jnp.dot(x.astype(BF16), wg_ref[...], preferred_element_type=F32) + bg_ref[...])
    br = jnp.dot(or_ref[...], wr_ref[...], preferred_element_type=F32)
    bm = jnp.dot(om_ref[...], wm_ref[...], preferred_element_type=F32)
    merged = gates[:, :D_MODEL] * br + gates[:, D_MODEL:] * bm
    y = jnp.dot(merged.astype(BF16), wo_ref[...], preferred_element_type=F32)
    o_ref[...] = _layer_norm_rows(ALPHA * x + y, g_ref[...], b_ref[...])


def _merge(x, o_r, o_m, wg16, bg, wr16, wm16, wo16, g, b, *, tm):
    T = x.shape[0]
    row = lambda i: (i, 0)
    consts = [wg16, bg, wr16, wm16, wo16, g, b]
    return pl.pallas_call(
        _merge_kernel,
        grid=(T // tm,),
        in_specs=[pl.BlockSpec((tm, D_MODEL), row), pl.BlockSpec((tm, RET_WIDTH), row),
                  pl.BlockSpec((tm, MOBA_WIDTH), row)] + [_const_spec(c.shape) for c in consts],
        out_specs=pl.BlockSpec((tm, D_MODEL), row),
        out_shape=jax.ShapeDtypeStruct((T, D_MODEL), F32),
        compiler_params=_params("arbitrary"), name="merge",
    )(x, o_r, o_m, *consts)


def _moe_kernel(x_ref, wrh_ref, wrl_ref, br_ref, weg_ref, weu_ref, wed_ref, g_ref, b_ref, o_ref):
    x = x_ref[...]
    tm = x.shape[0]
    hi = x.astype(BF16)
    lo = (x - hi.astype(F32)).astype(BF16)
    logit = (jnp.dot(hi, wrh_ref[...], preferred_element_type=F32)
             + (jnp.dot(hi, wrl_ref[...], preferred_element_type=F32)
                + jnp.dot(lo, wrh_ref[...], preferred_element_type=F32))) + br_ref[...]
    lane = lax.broadcasted_iota(jnp.int32, (tm, ROUTER_LANES), 1)
    neg = -jnp.inf
    is_group = lane < N_GROUPS
    gl = jnp.where(is_group, logit, neg)
    gmax = jnp.max(gl, axis=-1, keepdims=True)
    gidx = jnp.min(jnp.where(gl == gmax, lane, ROUTER_LANES), axis=-1, keepdims=True)
    g_w = 1.0 / jnp.sum(jnp.exp(gl - gmax), axis=-1, keepdims=True)
    first = N_GROUPS + EXPERTS_PER_GROUP * gidx
    in_group = jnp.logical_and(lane >= first, lane < first + EXPERTS_PER_GROUP)
    el = jnp.where(in_group, logit, neg)
    e1 = jnp.max(el, axis=-1, keepdims=True)
    i1 = jnp.min(jnp.where(el == e1, lane, ROUTER_LANES), axis=-1, keepdims=True)
    el2 = jnp.where(lane == i1, neg, el)
    e2 = jnp.max(el2, axis=-1, keepdims=True)
    i2 = jnp.min(jnp.where(el2 == e2, lane, ROUTER_LANES), axis=-1, keepdims=True)
    t = jnp.exp(e2 - e1)
    w1 = g_w / (1.0 + t)
    w2 = g_w * t / (1.0 + t)
    comb = jnp.where(lane == i1, w1, 0.0) + jnp.where(lane == i2, w2, 0.0)
    acc = jnp.zeros((tm, D_MODEL), F32)
    for g in range(N_GROUPS):
        hg = jnp.dot(hi, weg_ref[g], preferred_element_type=F32)
        hu = jnp.dot(hi, weu_ref[g], preferred_element_type=F32)
        cexp = jnp.concatenate(
            [jnp.broadcast_to(comb[:, N_GROUPS + g * EXPERTS_PER_GROUP + e:N_GROUPS + g * EXPERTS_PER_GROUP + e + 1],
                              (tm, EXPERT_HIDDEN)) for e in range(EXPERTS_PER_GROUP)], axis=1)
        hid = (hg * _sigmoid(hg)) * hu
        acc = acc + jnp.dot((hid * cexp).astype(BF16), wed_ref[g], preferred_element_type=F32)
    o_ref[...] = _layer_norm_rows(ALPHA * x + acc, g_ref[...], b_ref[...])


def _moe(x, wr_hi, wr_lo, br, weg16, weu16, wed16, g, b, *, tm):
    T = x.shape[0]
    row = lambda i: (i, 0)
    consts = [wr_hi, wr_lo, br, weg16, weu16, wed16, g, b]
    return pl.pallas_call(
        _moe_kernel,
        grid=(T // tm,),
        in_specs=[pl.BlockSpec((tm, D_MODEL), row)] + [_const_spec(c.shape) for c in consts],
        out_specs=pl.BlockSpec((tm, D_MODEL), row),
        out_shape=jax.ShapeDtypeStruct((T, D_MODEL), F32),
        compiler_params=_params("arbitrary"), name="moe",
    )(x, *consts)


def _layer_weights(l, w_in, b_merge, gn_g, w_br_ret, w_br_moba, w_out, ln1_g, ln1_b, w_router_group,
                   b_router_group, w_router_expert, b_router_expert, w_exp_gate, w_exp_up, w_exp_down,
                   ln2_g, ln2_b):
    w = w_in[l]
    q0 = 4 * RET_WIDTH
    n_exp = N_GROUPS * EXPERTS_PER_GROUP
    w_r = jnp.concatenate(
        [w_router_group[l], w_router_expert[l].transpose(1, 0, 2).reshape(D_MODEL, n_exp),
         jnp.zeros((D_MODEL, ROUTER_LANES - N_GROUPS - n_exp), F32)], axis=1)
    w_r_hi = w_r.astype(BF16)
    w_r_lo = (w_r - w_r_hi.astype(F32)).astype(BF16)
    b_r = jnp.concatenate([b_router_group[l], b_router_expert[l].reshape(n_exp),
                           jnp.zeros((ROUTER_LANES - N_GROUPS - n_exp,), F32)])[None, :]
    by_group = lambda t: (t.reshape(N_GROUPS, EXPERTS_PER_GROUP, D_MODEL, EXPERT_HIDDEN)
                          .transpose(0, 2, 1, 3).reshape(N_GROUPS, D_MODEL, GROUP_HIDDEN).astype(BF16))
    return dict(
        w_proj=w[:, :N_PROJ].astype(BF16),
        w_ret=w[:, :q0].astype(BF16),
        w_mobaT=w[:, q0:N_PROJ].T.reshape(3, MOBA_WIDTH, D_MODEL).astype(BF16),
        w_gate=w[:, N_PROJ:].astype(BF16), b_gate=b_merge[l][None, :],
        gn=gn_g[l][None, :],
        w_br_ret=w_br_ret[l].astype(BF16), w_br_moba=w_br_moba[l].astype(BF16), w_out=w_out[l].astype(BF16),
        ln1_g=ln1_g[l][None, :], ln1_b=ln1_b[l][None, :],
        w_r_hi=w_r_hi, w_r_lo=w_r_lo, b_r=b_r,
        w_eg=by_group(w_exp_gate[l]), w_eu=by_group(w_exp_up[l]),
        w_ed=w_exp_down[l].reshape(N_GROUPS, GROUP_HIDDEN, D_MODEL).astype(BF16),
        ln2_g=ln2_g[l][None, :], ln2_b=ln2_b[l][None, :])


def _pad_rows(t, rows):
    return jnp.pad(t, ((0, 0), (0, rows - t.shape[1]), (0, 0)))


def kernel(x_prompt, x_sample, cache_k, cache_v, state_ret, page_table, w_in, b_merge, gn_g, w_br_ret, w_br_moba, w_out, ln1_g, ln1_b, w_router_group, b_router_group, w_router_expert, b_router_expert, w_exp_gate, w_exp_up, w_exp_down, ln2_g, ln2_b):
    B, S, _ = x_prompt.shape
    DB, L, _ = x_sample.shape
    Tp, Ts = B * S, DB * L
    depth = w_in.shape[0]
    n_pool = cache_k.shape[1]
    page_major = lambda c: c.transpose(0, 1, 3, 4, 2).reshape(depth, n_pool, MOBA_WIDTH, PAGE_SIZE)
    cache_kT, cache_vT = page_major(cache_k), page_major(cache_v)
    xp = x_prompt.reshape(Tp, D_MODEL)
    xs = x_sample.reshape(Ts, D_MODEL)
    tm_p = min(512, Tp)
    tm_s = min(256, Ts)
    outs = [[] for _ in range(6)]
    for l in range(depth):
        W = _layer_weights(l, w_in, b_merge, gn_g, w_br_ret, w_br_moba, w_out, ln1_g, ln1_b, w_router_group,
                           b_router_group, w_router_expert, b_router_expert, w_exp_gate, w_exp_up, w_exp_down,
                           ln2_g, ln2_b)
        (rq, rk, rv, rg, kT_p, vT_p, k16, qT16, vT16, kmean) = _in_proj(
            xp, W["w_ret"], W["w_mobaT"], batch=B, tm=tm_p)
        o_r, s_p = _ret_prompt(rq, rk, rv, rg, W["gn"], B)
        o_m = _moba_prompt(qT16, k16, vT16, kmean, B)
        x1 = _merge(xp, o_r, o_m, W["w_gate"], W["b_gate"], W["w_br_ret"], W["w_br_moba"], W["w_out"],
                    W["ln1_g"], W["ln1_b"], tm=tm_p)
        xp = _moe(x1, W["w_r_hi"], W["w_r_lo"], W["b_r"], W["w_eg"], W["w_eu"], W["w_ed"],
                  W["ln2_g"], W["ln2_b"], tm=tm_p)
        (rq, rk, rv, rg, mq, k_s, v_s) = _in_proj(xs, W["w_proj"], tm=tm_s)
        r3 = lambda t: t.reshape(DB, L, t.shape[-1])
        o_r, s_s = _ret_sample(_pad_rows(r3(rq), SAMPLE_ROWS), _pad_rows(r3(rk), SAMPLE_ROWS),
                               _pad_rows(r3(rv), SAMPLE_ROWS), r3(rg), W["gn"], state_ret, l)
        mq3 = r3(mq)
        q16 = _pad_rows(jnp.concatenate([mq3, mq3], axis=1), SAMPLE_ROWS)
        o_m = _moba_sample(q16, _pad_rows(r3(k_s).astype(BF16), SAMPLE_ROWS),
                           _pad_rows(r3(v_s).astype(BF16), SAMPLE_ROWS),
                           cache_kT, cache_vT, page_table, l, L)
        x1 = _merge(xs, o_r.reshape(Ts, RET_WIDTH).astype(BF16), o_m.reshape(Ts, MOBA_WIDTH).astype(BF16),
                    W["w_gate"], W["b_gate"], W["w_br_ret"], W["w_br_moba"], W["w_out"],
                    W["ln1_g"], W["ln1_b"], tm=tm_s)
        xs = _moe(x1, W["w_r_hi"], W["w_r_lo"], W["b_r"], W["w_eg"], W["w_eu"], W["w_ed"],
                  W["ln2_g"], W["ln2_b"], tm=tm_s)
        for lst, val in zip(outs, (kT_p, vT_p, s_p,
                                   k_s.reshape(DB, L, MOBA_HEADS, MOBA_DH), v_s.reshape(DB, L, MOBA_HEADS, MOBA_DH), s_s)):
            lst.append(val)
    kTp, vTp, sp, ksm, vsm, ssm = (jnp.stack(o) for o in outs)
    token_major = lambda t: t.reshape(depth, B, MOBA_HEADS, MOBA_DH, S).transpose(0, 1, 4, 2, 3)
    return (xp.reshape(B, S, D_MODEL), xs.reshape(DB, L, D_MODEL), token_major(kTp), token_major(vTp), sp,
            ksm, vsm, ssm)
```

```python
import functools

import numpy as np
import jax
import jax.numpy as jnp
from jax import lax
from jax.experimental import pallas as pl
from jax.experimental.pallas import tpu as pltpu

F32 = jnp.float32
BF16 = jnp.bfloat16

D_MODEL = 1024
RET_HEADS = 4
RET_DK = 128
RET_WIDTH = RET_HEADS * RET_DK
RET_CHUNK = 128
MOBA_HEADS = 8
MOBA_DH = 64
MOBA_WIDTH = MOBA_HEADS * MOBA_DH
MOBA_BLOCK = 256
MOBA_TOPK = 3
PAGE_SIZE = 128
N_GROUPS = 4
EXPERTS_PER_GROUP = 8
EXPERT_HIDDEN = 128
GROUP_HIDDEN = EXPERTS_PER_GROUP * EXPERT_HIDDEN
DEPTH = 2
ALPHA = (2 * DEPTH) ** 0.25
LN_EPS = 1e-5
N_PROJ = 4 * RET_WIDTH + 3 * MOBA_WIDTH

LANES = 128
PAIR = 2 * MOBA_DH
N_PAIRS = MOBA_HEADS // 2
GROUP_HEADS = 4
DENOM_ROWS = 16
LOG2E = 1.4426950408889634
SAMPLE_ROWS = 16
ROUTER_LANES = 128
VMEM_LIMIT =56 * 1024 * 1024

NT = (((1,), (1,)), ((), ()))
TN = (((0,), (0,)), ((), ()))


def _params(*sem):
    return pltpu.CompilerParams(dimension_semantics=sem, vmem_limit_bytes=VMEM_LIMIT)


def _const_spec(shape):
    nd = len(shape)
    return pl.BlockSpec(shape, lambda *_: (0,) * nd, pipeline_mode=pl.Buffered(1))


def _layer_norm_rows(z, g, b):
    mu = jnp.mean(z, axis=-1, keepdims=True)
    zc = z - mu
    var = jnp.mean(zc * zc, axis=-1, keepdims=True)
    return zc * lax.rsqrt(var + LN_EPS) * g + b


def _sigmoid(x):
    return 1.0 / (1.0 + jnp.exp(-x))


def _in_proj_kernel(x_ref, w_ref, *refs, transposed, tm):
    xb = x_ref[...].astype(BF16)

    def proj(c):
        return jnp.dot(xb, w_ref[:, c * RET_WIDTH:(c + 1) * RET_WIDTH], preferred_element_type=F32)

    if transposed:
        wT_ref, rq_ref, rk_ref, rv_ref, rg_ref, kT_ref, vT_ref, k16_ref, qT16_ref, vT16_ref, kmean_ref = refs
        projT = lambda c: lax.dot_general(wT_ref[c], xb, NT, preferred_element_type=F32)
        qT16_ref[...] = (projT(0) * MOBA_DH ** -0.5).astype(BF16)
        kT = projT(1)
        kT_ref[0] = kT
        k = kT.T
        k16_ref[...] = k.astype(BF16)
        nb = tm // MOBA_BLOCK
        kmean_ref[0] = jnp.mean(k.reshape(nb, MOBA_BLOCK, MOBA_WIDTH), axis=1)
        vT = projT(2)
        vT_ref[0] = vT
        vT16_ref[...] = vT.astype(BF16)
    else:
        rq_ref, rk_ref, rv_ref, rg_ref, mq_ref, k_ref, v_ref = refs
        mq_ref[...] = (proj(4) * MOBA_DH ** -0.5).astype(BF16)
        k_ref[...] = proj(5)
        v_ref[...] = proj(6)
    rq_ref[...] = proj(0).astype(BF16)
    rk_ref[...] = (proj(1) * RET_DK ** -0.5).astype(BF16)
    rv_ref[...] = proj(2).astype(BF16)
    rg_ref[...] = proj(3)


def _in_proj(x, w16, wT16=None, *, rows=None, batch=1, tm):
    T = rows or x.shape[0]
    transposed = wT16 is not None
    row = lambda i: (i, 0)
    tile = lambda: pl.BlockSpec((tm, RET_WIDTH), row)
    in_specs = [pl.BlockSpec((tm, D_MODEL), row), _const_spec(w16.shape)]
    args = [x, w16]
    out_shape = [jax.ShapeDtypeStruct((T, RET_WIDTH), BF16)] * 3 + [jax.ShapeDtypeStruct((T, RET_WIDTH), F32)]
    out_specs = [tile() for _ in range(4)]
    if transposed:
        S = T // batch
        per_b = S // tm
        in_specs.append(_const_spec(wT16.shape))
        args.append(wT16)
        col = lambda i: (0, i)
        by_batch = lambda i: (i // per_b, 0, i % per_b)
        out_shape += [jax.ShapeDtypeStruct((batch, MOBA_WIDTH, S), F32)] * 2 + [
            jax.ShapeDtypeStruct((T, MOBA_WIDTH), BF16),
            jax.ShapeDtypeStruct((MOBA_WIDTH, T), BF16), jax.ShapeDtypeStruct((MOBA_WIDTH, T), BF16),
            jax.ShapeDtypeStruct((T // tm, tm // MOBA_BLOCK, MOBA_WIDTH), F32)]
        out_specs += [pl.BlockSpec((1, MOBA_WIDTH, tm), by_batch), pl.BlockSpec((1, MOBA_WIDTH, tm), by_batch),
                      tile(), pl.BlockSpec((MOBA_WIDTH, tm), col), pl.BlockSpec((MOBA_WIDTH, tm), col),
                      pl.BlockSpec((1, tm // MOBA_BLOCK, MOBA_WIDTH), lambda i: (i, 0, 0))]
    else:
        out_shape += [jax.ShapeDtypeStruct((T, MOBA_WIDTH), BF16)] + [jax.ShapeDtypeStruct((T, MOBA_WIDTH), F32)] * 2
        out_specs += [tile() for _ in range(3)]
    outs = pl.pallas_call(
        functools.partial(_in_proj_kernel, transposed=transposed, tm=tm),
        grid=(T // tm,), in_specs=in_specs, out_specs=out_specs, out_shape=out_shape,
        compiler_params=_params("arbitrary"), name="in_proj",
    )(*args)
    if transposed:
        outs = list(outs)
        outs[-1] = outs[-1].reshape(T // MOBA_BLOCK, MOBA_WIDTH)
    return outs


def _ret_tables(L, rows):
    log_g = jnp.log(jnp.asarray(1.0 - 2.0 ** (-5.0 - np.arange(RET_HEADS)), dtype=F32))
    idx = jnp.arange(L, dtype=F32)
    diff = idx[:, None] - idx[None, :]
    decay = jnp.where(diff >= 0, jnp.exp(log_g[:, None, None] * jnp.maximum(diff, 0.0)), 0.0)
    qdec = jnp.exp(log_g[:, None] * (idx + 1.0))
    kdec = jnp.exp(log_g[:, None] * (L - 1.0 - idx))
    g_chunk = jnp.exp(log_g * L)
    pad = rows - L
    decay = jnp.pad(decay, ((0, 0), (0, pad), (0, pad)))
    lanes = lambda t: jnp.broadcast_to(jnp.pad(t, ((0, 0), (0, pad)))[:, :, None], (RET_HEADS, rows, LANES))
    return decay, lanes(qdec), lanes(kdec), jnp.broadcast_to(g_chunk[:, None, None], (RET_HEADS, 1, LANES))


def _ret_head(q, k, v, state, decay, qdec, kdec, g_chunk):
    scores = lax.dot_general(q, k, NT, preferred_element_type=F32) * decay
    inner = jnp.dot(scores.astype(BF16), v, preferred_element_type=F32)
    q_dec = (q.astype(F32) * qdec).astype(BF16)
    cross = jnp.dot(q_dec, state.astype(BF16), preferred_element_type=F32)
    k_dec = (k.astype(F32) * kdec).astype(BF16)
    new_state = state * g_chunk + lax.dot_general(k_dec, v, TN, preferred_element_type=F32)
    return inner + cross, new_state


def _ret_gate(o, rg, gn):
    mu = jnp.mean(o, axis=-1, keepdims=True)
    oc = o - mu
    var = jnp.mean(oc * oc, axis=-1, keepdims=True)
    return (rg * _sigmoid(rg)) * (oc * lax.rsqrt(var + LN_EPS) * gn)


def _ret_prompt_kernel(q_ref, k_ref, v_ref, rg_ref, gn_ref, decay_ref, qdec_ref, kdec_ref, gc_ref,
                       o_ref, state_ref):
    @pl.when(pl.program_id(0) == 0)
    def _():
        state_ref[...] = jnp.zeros_like(state_ref)

    for b in range(q_ref.shape[0]):
        for h in range(RET_HEADS):
            sl = slice(h * RET_DK, (h + 1) * RET_DK)
            o, new_state = _ret_head(q_ref[b, :, sl], k_ref[b, :, sl], v_ref[b, :, sl], state_ref[b, h],
                                     decay_ref[h], qdec_ref[h], kdec_ref[h], gc_ref[h])
            state_ref[b, h] = new_state
            o_ref[b, :, sl] = _ret_gate(o, rg_ref[b, :, sl], gn_ref[:, sl]).astype(BF16)


def _ret_prompt(rq, rk, rv, rg, gn, batch):
    T = rq.shape[0]
    S = T // batch
    tables = _ret_tables(RET_CHUNK, RET_CHUNK)
    by_batch = lambda t: t.reshape(batch, S, RET_WIDTH)
    tile = pl.BlockSpec((batch, RET_CHUNK, RET_WIDTH), lambda c: (0, c, 0))
    o_r, state = pl.pallas_call(
        _ret_prompt_kernel,
        grid=(S // RET_CHUNK,),
        in_specs=[tile, tile, tile, tile, _const_spec(gn.shape)] + [_const_spec(t.shape) for t in tables],
        out_specs=[tile, pl.BlockSpec((batch, RET_HEADS, RET_DK, RET_DK), lambda c: (0, 0, 0, 0))],
        out_shape=[jax.ShapeDtypeStruct((batch, S, RET_WIDTH), BF16),
                   jax.ShapeDtypeStruct((batch, RET_HEADS, RET_DK, RET_DK), F32)],
        compiler_params=_params("arbitrary"), name="ret_prompt",
    )(by_batch(rq), by_batch(rk), by_batch(rv), by_batch(rg), gn, *tables)
    return o_r.reshape(T, RET_WIDTH), state


def _ret_sample_kernel(q_ref, k_ref, v_ref, rg_ref, gn_ref, s_ref, decay_ref, qdec_ref, kdec_ref, gc_ref,
                       o_ref, snew_ref, *, bt, L):
    for bi in range(bt):
        for h in range(RET_HEADS):
            sl = slice(h * RET_DK, (h + 1) * RET_DK)
            o, new_state = _ret_head(q_ref[bi, :, sl], k_ref[bi, :, sl], v_ref[bi, :, sl], s_ref[0, bi, h],
                                     decay_ref[h], qdec_ref[h], kdec_ref[h], gc_ref[h])
            snew_ref[bi, h] = new_state
            o_ref[bi, :, sl] = _ret_gate(o[:L], rg_ref[bi, :, sl], gn_ref[:, sl])


def _ret_sample(rq, rk, rv, rg, gn, state, layer, *, bt=8):
    db, L = rg.shape[0], rg.shape[1]
    tables = _ret_tables(L, SAMPLE_ROWS)
    b3 = lambda i: (i, 0, 0)
    qkv = pl.BlockSpec((bt, SAMPLE_ROWS, RET_WIDTH), b3)
    st = pl.BlockSpec((bt, RET_HEADS, RET_DK, RET_DK), lambda i: (i, 0, 0, 0))
    st_in = pl.BlockSpec((1, bt, RET_HEADS, RET_DK, RET_DK), lambda i: (layer, i, 0, 0, 0))
    return pl.pallas_call(
        functools.partial(_ret_sample_kernel, bt=bt, L=L),
        grid=(db // bt,),
        in_specs=[qkv, qkv, qkv, pl.BlockSpec((bt, L, RET_WIDTH), b3), _const_spec(gn.shape), st_in]
        + [_const_spec(t.shape) for t in tables],
        out_specs=[pl.BlockSpec((bt, L, RET_WIDTH), b3), st],
        out_shape=[jax.ShapeDtypeStruct((db, L, RET_WIDTH), F32),
                   jax.ShapeDtypeStruct((db, RET_HEADS, RET_DK, RET_DK), F32)],
        compiler_params=_params("arbitrary"), name="ret_sample",
    )(rq, rk, rv, rg, gn, state, *tables)


def _alibi_slopes():
    return 2.0 ** (-8.0 * np.arange(1, MOBA_HEADS + 1) / MOBA_HEADS)


def _moba_prompt_tables():
    slopes = jnp.asarray(_alibi_slopes(), dtype=F32)[:, None, None]
    kk = jnp.arange(MOBA_BLOCK, dtype=F32)[:, None]
    qq = jnp.arange(MOBA_BLOCK, dtype=F32)[None, :]
    dist = (qq - kk)[None]
    past = -(slopes * dist) * LOG2E
    own = jnp.where(dist >= 0, past, -jnp.inf)
    block_step = -(slopes * float(MOBA_BLOCK)) * LOG2E
    return past, own, jnp.broadcast_to(block_step, (MOBA_HEADS, 1, MOBA_BLOCK))


def _for_blocks(n, body):
    def two(i, carry):
        body(2 * i)
        body(2 * i + 1)
        return carry

    lax.fori_loop(0, lax.shift_right_logical(n, 1), two, 0)

    @pl.when(lax.bitwise_and(n, 1) == 1)
    def _():
        body(n - 1)


def _moba_prompt_kernel(qT_ref, k_ref, vT_ref, kmean_ref, past_ref, own_ref, step_ref, o_ref,
                        q_sc, s_sc, m_sc, acc_sc, term_sc):
    j = pl.program_id(2)
    nb = kmean_ref.shape[0]
    dh_row = lax.broadcasted_iota(jnp.int32, (PAIR, MOBA_BLOCK), 0)
    blk = lax.broadcasted_iota(jnp.int32, (nb, MOBA_BLOCK), 0)
    own_start = pl.multiple_of(j * MOBA_BLOCK, MOBA_BLOCK)
    pair_cols = lambda h: slice((h // 2) * PAIR, (h // 2 + 1) * PAIR)

    for h in range(GROUP_HEADS):
        qT = qT_ref[pair_cols(h), :]
        keep = (dh_row < MOBA_DH) if h % 2 == 0 else (dh_row >= MOBA_DH)
        qh = jnp.where(keep, qT, jnp.zeros_like(qT))
        q_sc[h] = qh
        gate = jnp.dot(kmean_ref[:, pair_cols(h)].astype(BF16), qh, preferred_element_type=F32)
        gate = jnp.where(blk < j, gate, -jnp.inf)
        sel = jnp.zeros(gate.shape, dtype=jnp.bool_)
        for _ in range(MOBA_TOPK):
            top = jnp.max(gate, axis=0, keepdims=True)
            first = jnp.min(jnp.where(gate == top, blk, nb), axis=0, keepdims=True)
            pick = jnp.logical_and(blk == first, top > -jnp.inf)
            sel = jnp.logical_or(sel, pick)
            gate = jnp.where(pick, -jnp.inf, gate)
        term_sc[h] = jnp.where(sel, (j - blk).astype(F32) * step_ref[h], jnp.where(blk == j, 0.0, -jnp.inf))
        s = jnp.dot(k_ref[pl.ds(own_start, MOBA_BLOCK), pair_cols(h)], qh, preferred_element_type=F32)
        s = s * LOG2E + own_ref[h]
        s_sc[h, j] = s
        m_sc[h] = jnp.max(s, axis=0, keepdims=True)
        acc_sc[h] = jnp.zeros_like(acc_sc[h])

    def scores(jj):
        start = pl.multiple_of(jj * MOBA_BLOCK, MOBA_BLOCK)
        for h in range(GROUP_HEADS):
            s = jnp.dot(k_ref[pl.ds(start, MOBA_BLOCK), pair_cols(h)], q_sc[h], preferred_element_type=F32)
            s = s * LOG2E + past_ref[h]
            s_sc[h, jj] = s
            m_sc[h] = jnp.maximum(m_sc[h], jnp.max(s, axis=0, keepdims=True) + term_sc[h, pl.ds(jj, 1), :])

    _for_blocks(j, scores)

    ones_rows = jnp.ones((DENOM_ROWS, MOBA_BLOCK), BF16)

    def apply_v(jj):
        start = pl.multiple_of(jj * MOBA_BLOCK, MOBA_BLOCK)
        for h in range(GROUP_HEADS):
            pexp = jnp.exp2(s_sc[h, jj] - (m_sc[h] - term_sc[h, pl.ds(jj, 1), :]))
            vT = jnp.concatenate([vT_ref[h * MOBA_DH:(h + 1) * MOBA_DH, pl.ds(start, MOBA_BLOCK)], ones_rows], axis=0)
            acc_sc[h] = acc_sc[h] + jnp.dot(vT, pexp.astype(BF16), preferred_element_type=F32)

    _for_blocks(j + 1, apply_v)

    def head_out(h):
        acc = acc_sc[h]
        return acc[:MOBA_DH] / acc[MOBA_DH:MOBA_DH + 1]

    for p in range(GROUP_HEADS // 2):
        outT = jnp.concatenate([head_out(2 * p), head_out(2 * p + 1)], axis=0)
        o_ref[:, p * PAIR:(p + 1) * PAIR] = outT.T.astype(BF16)


def _moba_prompt(qT16, k16, vT16, kmean, batch):
    T = k16.shape[0]
    S = T // batch
    nb = S // MOBA_BLOCK
    n_groups = MOBA_HEADS // GROUP_HEADS
    gw = GROUP_HEADS * MOBA_DH
    tables = _moba_prompt_tables()
    head_tile = lambda t: pl.BlockSpec((GROUP_HEADS,) + t.shape[1:], lambda b, g, j: (g, 0, 0))
    return pl.pallas_call(
        _moba_prompt_kernel,
        grid=(batch, n_groups, nb),
        in_specs=[pl.BlockSpec((gw, MOBA_BLOCK), lambda b, g, j: (g, b * nb + j)),
                  pl.BlockSpec((S, gw), lambda b, g, j: (b, g)),
                  pl.BlockSpec((gw, S), lambda b, g, j: (g, b)),
                  pl.BlockSpec((nb, gw), lambda b, g, j: (b, g))]
        + [head_tile(t) for t in tables],
        out_specs=pl.BlockSpec((MOBA_BLOCK, gw), lambda b, g, j: (b * nb + j, g)),
        out_shape=jax.ShapeDtypeStruct((T, MOBA_WIDTH), BF16),
        scratch_shapes=[pltpu.VMEM((GROUP_HEADS, PAIR, MOBA_BLOCK), BF16),
                        pltpu.VMEM((GROUP_HEADS, nb, MOBA_BLOCK, MOBA_BLOCK), F32),
                        pltpu.VMEM((GROUP_HEADS, 1, MOBA_BLOCK), F32),
                        pltpu.VMEM((GROUP_HEADS, MOBA_DH + DENOM_ROWS, MOBA_BLOCK), F32),
                        pltpu.VMEM((GROUP_HEADS, nb, MOBA_BLOCK), F32)],
        compiler_params=_params("arbitrary", "arbitrary", "arbitrary"), name="moba_prompt",
    )(qT16, k16, vT16, kmean, *tables)


def _moba_sample_tables(L, n_pages):
    past_len = n_pages * PAGE_SIZE
    slopes = _alibi_slopes()
    row_slope = np.zeros((N_PAIRS, SAMPLE_ROWS), np.float64)
    row_t = np.zeros((SAMPLE_ROWS,), np.float64)
    for p in range(N_PAIRS):
        row_slope[p, :L] = slopes[2 * p]
        row_slope[p, L:2 * L] = slopes[2 * p + 1]
    row_t[:L] = np.arange(L)
    row_t[L:2 * L] = np.arange(L)
    row_slope = jnp.asarray(row_slope, dtype=F32)[:, :, None]
    q_pos = jnp.asarray(past_len + row_t, dtype=F32)[None, :, None]
    key_pos = jnp.arange(past_len, dtype=F32)[None, None, :]
    past = -(row_slope * (q_pos - key_pos))
    new_pos = jnp.arange(SAMPLE_ROWS, dtype=F32)[None, None, :]
    dist_new = jnp.asarray(row_t, dtype=F32)[None, :, None] - new_pos
    valid = jnp.logical_and(dist_new >= 0, new_pos < L)
    new = jnp.where(valid, -(row_slope * dist_new), -jnp.inf)
    return past, new


def _moba_sample_kernel(pt_ref, q_ref, kn_ref, vn_ref, past_ref, new_ref, *refs, L, n_pages):
    k_pages, v_pages, o_ref = refs[:n_pages], refs[n_pages:2 * n_pages], refs[2 * n_pages]
    del pt_ref
    n_blk = n_pages * PAGE_SIZE // MOBA_BLOCK
    per_blk = MOBA_BLOCK // PAGE_SIZE
    row = lax.broadcasted_iota(jnp.int32, (SAMPLE_ROWS, PAIR), 0)
    lane = lax.broadcasted_iota(jnp.int32, (SAMPLE_ROWS, PAIR), 1)
    keep = jnp.logical_or(jnp.logical_and(row < L, lane < MOBA_DH),
                          jnp.logical_and(jnp.logical_and(row >= L, row < 2 * L), lane >= MOBA_DH))
    for p in range(N_PAIRS):
        cols = slice(p * PAIR, (p + 1) * PAIR)
        q = q_ref[0, :, cols]
        qm = jnp.where(keep, q, jnp.zeros_like(q))
        kT = jnp.concatenate([k_pages[s][0, 0, cols, :] for s in range(n_pages)], axis=1).astype(BF16)
        raw = jnp.dot(qm, kT, preferred_element_type=F32)
        blk_lanes = lambda jj: slice(jj * MOBA_BLOCK, (jj + 1) * MOBA_BLOCK)
        gate = [jnp.sum(raw[:, blk_lanes(jj)], axis=-1, keepdims=True) * (1.0 / MOBA_BLOCK) for jj in range(n_blk)]
        sel = []
        for jj in range(n_blk):
            ahead = jnp.zeros(gate[jj].shape, F32)
            for kk in range(n_blk):
                if kk == jj:
                    continue
                beats = (gate[kk] >= gate[jj]) if kk < jj else (gate[kk] > gate[jj])
                ahead = ahead + jnp.where(beats, 1.0, 0.0)
            sel.append(ahead < float(min(MOBA_TOPK, n_blk)))
        logit = jnp.concatenate(
            [jnp.where(sel[jj], raw[:, blk_lanes(jj)] + past_ref[p, :, blk_lanes(jj)], -jnp.inf)
             for jj in range(n_blk)], axis=1)
        kn = kn_ref[0, :, cols]
        s_new = lax.dot_general(qm, kn, NT, preferred_element_type=F32) + new_ref[p]
        m = jnp.maximum(jnp.max(s_new, axis=-1, keepdims=True), jnp.max(logit, axis=-1, keepdims=True))
        p_new = jnp.exp(s_new - m)
        pexp = jnp.exp(logit - m)
        denom = jnp.sum(p_new, axis=-1, keepdims=True) + jnp.sum(pexp, axis=-1, keepdims=True)
        vT = jnp.concatenate([v_pages[s][0, 0, cols, :] for s in range(n_pages)], axis=1).astype(BF16)
        acc = (jnp.dot(p_new.astype(BF16), vn_ref[0, :, cols], preferred_element_type=F32)
               + lax.dot_general(pexp.astype(BF16), vT, NT, preferred_element_type=F32))
        out = acc / denom
        o_ref[0, :, cols] = jnp.where(lane[:L] < MOBA_DH, out[:L], out[L:2 * L])


def _moba_sample(q16, kn16, vn16, cache_kT, cache_vT, page_table, layer, L):
    db, n_pages = page_table.shape
    past, new = _moba_sample_tables(L, n_pages)
    b3 = lambda b, pt: (b, 0, 0)
    row_spec = pl.BlockSpec((1, SAMPLE_ROWS, MOBA_WIDTH), b3)

    def page_spec(s):
        return pl.BlockSpec((1, 1, MOBA_WIDTH, PAGE_SIZE), lambda b, pt: (layer, pt[b, s], 0, 0))

    in_specs = [row_spec, row_spec, row_spec,
                pl.BlockSpec(past.shape, lambda b, pt: (0, 0, 0)), pl.BlockSpec(new.shape, lambda b, pt: (0, 0, 0))]
    in_specs += [page_spec(s) for s in range(n_pages)] * 2
    grid_spec = pltpu.PrefetchScalarGridSpec(
        num_scalar_prefetch=1, grid=(db,), in_specs=in_specs,
        out_specs=pl.BlockSpec((1, L, MOBA_WIDTH), b3))
    return pl.pallas_call(
        functools.partial(_moba_sample_kernel, L=L, n_pages=n_pages),
        grid_spec=grid_spec,
        out_shape=jax.ShapeDtypeStruct((db, L, MOBA_WIDTH), F32),
        compiler_params=_params("arbitrary"), name="moba_sample",
    )(page_table, q16, kn16, vn16, past, new, *([cache_kT] * n_pages), *([cache_vT] * n_pages))


def _route(x, wrh_ref, wrl_ref, br_ref, group=None):
    tm = x.shape[0]
    hi = x.astype(BF16)
    lo = (x - hi.astype(F32)).astype(BF16)
    logit = (jnp.dot(hi, wrh_ref[...], preferred_element_type=F32)
             + (jnp.dot(hi, wrl_ref[...], preferred_element_type=F32)
                + jnp.dot(lo, wrh_ref[...], preferred_element_type=F32))) + br_ref[...]
    lane = lax.broadcasted_iota(jnp.int32, (tm, ROUTER_LANES), 1)
    neg = -jnp.inf
    gl = jnp.where(lane < N_GROUPS, logit, neg)
    gmax = jnp.max(gl, axis=-1, keepdims=True)
    g_sum = jnp.sum(jnp.exp(gl - gmax), axis=-1, keepdims=True)
    if group is None:
        gidx = jnp.min(jnp.where(gl == gmax, lane, ROUTER_LANES), axis=-1, keepdims=True)
        g_w = 1.0 / g_sum
    else:
        gidx = group
        g_w = jnp.exp(jnp.sum(jnp.where(lane == group, logit, 0.0), axis=-1, keepdims=True) - gmax) / g_sum
    first = N_GROUPS + EXPERTS_PER_GROUP * gidx
    in_group = jnp.logical_and(lane >= first, lane < first + EXPERTS_PER_GROUP)
    el = jnp.where(in_group, logit, neg)
    e1 = jnp.max(el, axis=-1, keepdims=True)
    i1 = jnp.min(jnp.where(el == e1, lane, ROUTER_LANES), axis=-1, keepdims=True)
    el2 = jnp.where(lane == i1, neg, el)
    e2 = jnp.max(el2, axis=-1, keepdims=True)
    i2 = jnp.min(jnp.where(el2 == e2, lane, ROUTER_LANES), axis=-1, keepdims=True)
    t = jnp.exp(e2 - e1)
    return gidx, i1, i2, g_w / (1.0 + t), g_w * t / (1.0 + t), lane


def _expert_lanes(weights, first_lane, tm):
    return jnp.concatenate([jnp.broadcast_to(weights[:, first_lane + e:first_lane + e + 1], (tm, EXPERT_HIDDEN))
                            for e in range(EXPERTS_PER_GROUP)], axis=1)


def _group_experts(xb, cexp, weg, weu, wed):
    hg = jnp.dot(xb, weg, preferred_element_type=F32)
    hu = jnp.dot(xb, weu, preferred_element_type=F32)
    hid = (hg * _sigmoid(hg)) * hu
    return jnp.dot((hid * cexp).astype(BF16), wed, preferred_element_type=F32)


def _merge_kernel(x_ref, or_ref, om_ref, wg_ref, bg_ref, wr_ref, wm_ref, wo_ref, g_ref, b_ref, *refs, with_route):
    x = x_ref[...]
    gates = _sigmoid(jnp.dot(x.astype(BF16), wg_ref[...], preferred_element_type=F32) + bg_ref[...])
    br = jnp.dot(or_ref[...], wr_ref[...], preferred_element_type=F32)
    bm = jnp.dot(om_ref[...], wm_ref[...], preferred_element_type=F32)
    merged = gates[:, :D_MODEL] * br + gates[:, D_MODEL:] * bm
    y = jnp.dot(merged.astype(BF16), wo_ref[...], preferred_element_type=F32)
    x1 = _layer_norm_rows(ALPHA * x + y, g_ref[...], b_ref[...])
    if not with_route:
        refs[0][...] = x1
        return
    wrh_ref, wrl_ref, brt_ref, o_ref, group_ref = refs
    o_ref[...] = x1
    gidx = _route(x1, wrh_ref, wrl_ref, brt_ref)[0]
    group_ref[...] = jnp.broadcast_to(gidx, group_ref.shape)


def _merge(x, o_r, o_m, wg16, bg, wr16, wm16, wo16, g, b, router=None, *, tm):
    T = o_r.shape[0]
    row = lambda i: (i, 0)
    consts = [wg16, bg, wr16, wm16, wo16, g, b] + list(router or ())
    out_shape = [jax.ShapeDtypeStruct((T, D_MODEL), F32)]
    out_specs = [pl.BlockSpec((tm, D_MODEL), row)]
    if router:
        out_shape.append(jax.ShapeDtypeStruct((T, ROUTER_LANES), jnp.int32))
        out_specs.append(pl.BlockSpec((tm, ROUTER_LANES), row))
    outs = pl.pallas_call(
        functools.partial(_merge_kernel, with_route=bool(router)),
        grid=(T // tm,),
        in_specs=[pl.BlockSpec((tm, D_MODEL), row), pl.BlockSpec((tm, RET_WIDTH), row),
                  pl.BlockSpec((tm, MOBA_WIDTH), row)] + [_const_spec(c.shape) for c in consts],
        out_specs=out_specs, out_shape=out_shape,
        compiler_params=_params("arbitrary"), name="merge",
    )(x, o_r, o_m, *consts)
    return outs if router else outs[0]


def _moe_kernel(x_ref, wrh_ref, wrl_ref, br_ref, weg_ref, weu_ref, wed_ref, g_ref, b_ref, o_ref):
    x = x_ref[...]
    tm = x.shape[0]
    gidx, i1, i2, w1, w2, lane = _route(x, wrh_ref, wrl_ref, br_ref)
    comb = jnp.where(lane == i1, w1, 0.0) + jnp.where(lane == i2, w2, 0.0)
    xb = x.astype(BF16)
    acc = jnp.zeros((tm, D_MODEL), F32)
    for g in range(N_GROUPS):
        cexp = _expert_lanes(comb, N_GROUPS + g * EXPERTS_PER_GROUP, tm)
        acc = acc + _group_experts(xb, cexp, weg_ref[g], weu_ref[g], wed_ref[g])
    o_ref[...] = _layer_norm_rows(ALPHA * x + acc, g_ref[...], b_ref[...])


def _moe(x, wr_hi, wr_lo, br, weg16, weu16, wed16, g, b, *, tm):
    T = x.shape[0]
    row = lambda i: (i, 0)
    consts = [wr_hi, wr_lo, br, weg16, weu16, wed16, g, b]
    return pl.pallas_call(
        _moe_kernel,
        grid=(T // tm,),
        in_specs=[pl.BlockSpec((tm, D_MODEL), row)] + [_const_spec(c.shape) for c in consts],
        out_specs=pl.BlockSpec((tm, D_MODEL), row),
        out_shape=jax.ShapeDtypeStruct((T, D_MODEL), F32),
        compiler_params=_params("arbitrary"), name="moe",
    )(x, *consts)


def _group_plan(gid, tm):
    T = gid.shape[0]
    groups = jnp.arange(N_GROUPS, dtype=jnp.int32)
    member = (gid[:, None] == groups[None, :]).astype(jnp.int32)
    running = jnp.cumsum(member, axis=0)
    counts = running[-1]
    rank = jnp.sum(running * member, axis=1) - 1
    padded = ((counts + tm - 1) // tm) * tm
    ends = jnp.cumsum(padded)
    starts = ends - padded
    pos = jnp.sum(starts[None, :] * member, axis=1) + rank
    n_tiles = T // tm + N_GROUPS
    rows = jnp.arange(T, dtype=jnp.int32)
    pad_dst = T + jnp.arange(tm, dtype=jnp.int32)
    src = jnp.zeros((n_tiles * tm,), jnp.int32).at[pos].set(rows)
    dst = jnp.tile(pad_dst, n_tiles).at[pos].set(rows)
    tile_start = jnp.arange(n_tiles, dtype=jnp.int32) * tm
    tile_group = jnp.minimum(jnp.sum((tile_start[:, None] >= ends[None, :]).astype(jnp.int32), axis=1), N_GROUPS - 1)
    dst = jnp.concatenate([pad_dst, dst]).reshape(n_tiles + 1, 1, tm)
    return src.reshape(n_tiles, 1, tm), dst, tile_group


def _moe_grouped_kernel(tg_ref, src_now_ref, src_next_ref, dst_prev_ref, dst_now_ref,
                        x_hbm, wrh_ref, wrl_ref, br_ref, weg_ref, weu_ref, wed_ref, g_ref, b_ref, out_hbm,
                        xbuf, obuf, gsem, ssem, *, tm):
    i = pl.program_id(0)
    last = pl.num_programs(0) - 1
    slot = lax.rem(i, 2)
    other = 1 - slot

    def fetch_rows(idx_ref, s):
        for r in range(tm):
            t = idx_ref[0, 0, r]
            pltpu.make_async_copy(x_hbm.at[pl.ds(t, 1)], xbuf.at[s, pl.ds(r, 1)], gsem.at[s]).start()

    def fetch_wait(s):
        pltpu.make_async_copy(x_hbm.at[pl.ds(0, tm)], xbuf.at[s], gsem.at[s]).wait()

    def write_rows(idx_ref, s):
        for r in range(tm):
            t = idx_ref[0, 0, r]
            pltpu.make_async_copy(obuf.at[s, pl.ds(r, 1)], out_hbm.at[pl.ds(t, 1)], ssem.at[s]).start()

    def write_wait(s):
        pltpu.make_async_copy(obuf.at[s], out_hbm.at[pl.ds(0, tm)], ssem.at[s]).wait()

    @pl.when(i == 0)
    def _():
        obuf[...] = jnp.zeros_like(obuf)
        fetch_rows(src_now_ref, 0)
        fetch_wait(0)

    fetch_rows(src_next_ref, other)
    write_rows(dst_prev_ref, other)
    x = xbuf[slot]
    group = tg_ref[i]
    _, i1, i2, w1, w2, lane = _route(x, wrh_ref, wrl_ref, br_ref, group=group)
    first = N_GROUPS + EXPERTS_PER_GROUP * group
    cexp = _expert_lanes(jnp.where(lane == i1 - first, w1, 0.0) + jnp.where(lane == i2 - first, w2, 0.0), 0, tm)
    acc = _group_experts(x.astype(BF16), cexp, weg_ref[0], weu_ref[0], wed_ref[0])
    obuf[slot] = _layer_norm_rows(ALPHA * x + acc, g_ref[...], b_ref[...])
    fetch_wait(other)
    write_wait(other)

    @pl.when(i == last)
    def _():
        write_rows(dst_now_ref, slot)
        write_wait(slot)


def _moe_grouped(x, T, src, dst, tile_group, wr_hi, wr_lo, br, weg16, weu16, wed16, g, b, *, tm):
    n_tiles = src.shape[0]
    smem_tile = lambda fn: pl.BlockSpec((1, 1, tm), fn, memory_space=pltpu.SMEM)
    by_group = lambda i, tg: (tg[i], 0, 0)
    w_spec = pl.BlockSpec((1, D_MODEL, GROUP_HIDDEN), by_group)
    fixed = lambda t: pl.BlockSpec(t.shape, lambda i, tg: (0, 0))
    grid_spec = pltpu.PrefetchScalarGridSpec(
        num_scalar_prefetch=1, grid=(n_tiles,),
        in_specs=[smem_tile(lambda i, tg: (i, 0, 0)),
                  smem_tile(lambda i, tg: (jnp.minimum(i + 1, n_tiles - 1), 0, 0)),
                  smem_tile(lambda i, tg: (i, 0, 0)), smem_tile(lambda i, tg: (i + 1, 0, 0)),
                  pl.BlockSpec(memory_space=pl.ANY), fixed(wr_hi), fixed(wr_lo), fixed(br),
                  w_spec, w_spec, pl.BlockSpec((1, GROUP_HIDDEN, D_MODEL), by_group), fixed(g), fixed(b)],
        out_specs=pl.BlockSpec(memory_space=pl.ANY),
        scratch_shapes=[pltpu.VMEM((2, tm, D_MODEL), F32), pltpu.VMEM((2, tm, D_MODEL), F32),
                        pltpu.SemaphoreType.DMA((2,)), pltpu.SemaphoreType.DMA((2,))])
    return pl.pallas_call(
        functools.partial(_moe_grouped_kernel, tm=tm),
        grid_spec=grid_spec,
        out_shape=jax.ShapeDtypeStruct((T + tm, D_MODEL), F32),
        compiler_params=_params("arbitrary"), name="moe_grouped",
    )(tile_group, src, src, dst, dst, x, wr_hi, wr_lo, br, weg16, weu16, wed16, g, b)


def _layer_weights(l, w_in, b_merge, gn_g, w_br_ret, w_br_moba, w_out, ln1_g, ln1_b, w_router_group,
                   b_router_group, w_router_expert, b_router_expert, w_exp_gate, w_exp_up, w_exp_down,
                   ln2_g, ln2_b):
    w = w_in[l]
    q0 = 4 * RET_WIDTH
    n_exp = N_GROUPS * EXPERTS_PER_GROUP
    w_r = jnp.concatenate(
        [w_router_group[l], w_router_expert[l].transpose(1, 0, 2).reshape(D_MODEL, n_exp),
         jnp.zeros((D_MODEL, ROUTER_LANES - N_GROUPS - n_exp), F32)], axis=1)
    w_r_hi = w_r.astype(BF16)
    w_r_lo = (w_r - w_r_hi.astype(F32)).astype(BF16)
    b_r = jnp.concatenate([b_router_group[l], b_router_expert[l].reshape(n_exp),
                           jnp.zeros((ROUTER_LANES - N_GROUPS - n_exp,), F32)])[None, :]
    by_group = lambda t: (t.reshape(N_GROUPS, EXPERTS_PER_GROUP, D_MODEL, EXPERT_HIDDEN)
                          .transpose(0, 2, 1, 3).reshape(N_GROUPS, D_MODEL, GROUP_HIDDEN).astype(BF16))
    return dict(
        w_proj=w[:, :N_PROJ].astype(BF16),
        w_ret=w[:, :q0].astype(BF16),
        w_mobaT=w[:, q0:N_PROJ].T.reshape(3, MOBA_WIDTH, D_MODEL).astype(BF16),
        w_gate=w[:, N_PROJ:].astype(BF16), b_gate=b_merge[l][None, :],
        gn=gn_g[l][None, :],
        w_br_ret=w_br_ret[l].astype(BF16), w_br_moba=w_br_moba[l].astype(BF16), w_out=w_out[l].astype(BF16),
        ln1_g=ln1_g[l][None, :], ln1_b=ln1_b[l][None, :],
        w_r_hi=w_r_hi, w_r_lo=w_r_lo, b_r=b_r,
        w_eg=by_group(w_exp_gate[l]), w_eu=by_group(w_exp_up[l]),
        w_ed=w_exp_down[l].reshape(N_GROUPS, GROUP_HIDDEN, D_MODEL).astype(BF16),
        ln2_g=ln2_g[l][None, :], ln2_b=ln2_b[l][None, :])


def _pad_rows(t, rows):
    return jnp.pad(t, ((0, 0), (0, rows - t.shape[1]), (0, 0)))


def kernel(x_prompt, x_sample, cache_k, cache_v, state_ret, page_table, w_in, b_merge, gn_g, w_br_ret, w_br_moba, w_out, ln1_g, ln1_b, w_router_group, b_router_group, w_router_expert, b_router_expert, w_exp_gate, w_exp_up, w_exp_down, ln2_g, ln2_b):
    B, S, _ = x_prompt.shape
    DB, L, _ = x_sample.shape
    Tp, Ts = B * S, DB * L
    depth = w_in.shape[0]
    n_pool = cache_k.shape[1]
    page_major = lambda c: c.transpose(0, 1, 3, 4, 2).reshape(depth, n_pool, MOBA_WIDTH, PAGE_SIZE)
    cache_kT, cache_vT = page_major(cache_k), page_major(cache_v)
    xp = x_prompt.reshape(Tp, D_MODEL)
    xs = x_sample.reshape(Ts, D_MODEL)
    tm_p = min(512, Tp)
    tm_s = min(256, Ts)
    tm_moe = 256
    outs = [[] for _ in range(6)]
    for l in range(depth):
        W = _layer_weights(l, w_in, b_merge, gn_g, w_br_ret, w_br_moba, w_out, ln1_g, ln1_b, w_router_group,
                           b_router_group, w_router_expert, b_router_expert, w_exp_gate, w_exp_up, w_exp_down,
                           ln2_g, ln2_b)
        (rq, rk, rv, rg, kT_p, vT_p, k16, qT16, vT16, kmean) = _in_proj(
            xp, W["w_ret"], W["w_mobaT"], rows=Tp, batch=B, tm=tm_p)
        o_r, s_p = _ret_prompt(rq, rk, rv, rg, W["gn"], B)
        o_m = _moba_prompt(qT16, k16, vT16, kmean, B)
        router = (W["w_r_hi"], W["w_r_lo"], W["b_r"])
        x1, group = _merge(xp, o_r, o_m, W["w_gate"], W["b_gate"], W["w_br_ret"], W["w_br_moba"], W["w_out"],
                           W["ln1_g"], W["ln1_b"], router=router, tm=tm_p)
        plan = _group_plan(group[:, 0], tm_moe)
        xp = _moe_grouped(x1, Tp, *plan, *router, W["w_eg"], W["w_eu"], W["w_ed"], W["ln2_g"], W["ln2_b"], tm=tm_moe)
        (rq, rk, rv, rg, mq, k_s, v_s) = _in_proj(xs, W["w_proj"], tm=tm_s)
        r3 = lambda t: t.reshape(DB, L, t.shape[-1])
        o_r, s_s = _ret_sample(_pad_rows(r3(rq), SAMPLE_ROWS), _pad_rows(r3(rk), SAMPLE_ROWS),
                               _pad_rows(r3(rv), SAMPLE_ROWS), r3(rg), W["gn"], state_ret, l)
        mq3 = r3(mq)
        q16 = _pad_rows(jnp.concatenate([mq3, mq3], axis=1), SAMPLE_ROWS)
        o_m = _moba_sample(q16, _pad_rows(r3(k_s).astype(BF16), SAMPLE_ROWS),
                           _pad_rows(r3(v_s).astype(BF16), SAMPLE_ROWS),
                           cache_kT, cache_vT, page_table, l, L)
        x1 = _merge(xs, o_r.reshape(Ts, RET_WIDTH).astype(BF16), o_m.reshape(Ts, MOBA_WIDTH).astype(BF16),
                    W["w_gate"], W["b_gate"], W["w_br_ret"], W["w_br_moba"], W["w_out"],
                    W["ln1_g"], W["ln1_b"], tm=tm_s)
        xs = _moe(x1, W["w_r_hi"], W["w_r_lo"], W["b_r"], W["w_eg"], W["w_eu"], W["w_ed"],
                  W["ln2_g"], W["ln2_b"], tm=tm_s)
        for lst, val in zip(outs, (kT_p, vT_p, s_p,
                                   k_s.reshape(DB, L, MOBA_HEADS, MOBA_DH), v_s.reshape(DB, L, MOBA_HEADS, MOBA_DH), s_s)):
            lst.append(val)
    kTp, vTp, sp, ksm, vsm, ssm = (jnp.stack(o) for o in outs)
    token_major = lambda t: t.reshape(depth, B, MOBA_HEADS, MOBA_DH, S).transpose(0, 1, 4, 2, 3)
    return (xp[:Tp].reshape(B, S, D_MODEL), xs.reshape(DB, L, D_MODEL), token_major(kTp), token_major(vTp), sp,
            ksm, vsm, ssm)
```

```python
import functools

import numpy as np
import jax
import jax.numpy as jnp
from jax import lax
from jax.experimental import pallas as pl
from jax.experimental.pallas import tpu as pltpu

F32 = jnp.float32
BF16 = jnp.bfloat16

D_MODEL = 1024
RET_HEADS = 4
RET_DK = 128
RET_WIDTH = RET_HEADS * RET_DK
RET_CHUNK = 128
MOBA_HEADS = 8
MOBA_DH = 64
MOBA_WIDTH = MOBA_HEADS * MOBA_DH
MOBA_BLOCK = 256
MOBA_TOPK = 3
PAGE_SIZE = 128
N_GROUPS = 4
EXPERTS_PER_GROUP = 8
EXPERT_HIDDEN = 128
GROUP_HIDDEN = EXPERTS_PER_GROUP * EXPERT_HIDDEN
DEPTH = 2
ALPHA = (2 * DEPTH) ** 0.25
LN_EPS = 1e-5
N_PROJ = 4 * RET_WIDTH + 3 * MOBA_WIDTH

LANES = 128
PAIR = 2 * MOBA_DH
N_PAIRS = MOBA_HEADS // 2
GROUP_HEADS = 4
DENOM_ROWS = 16
LOG2E = 1.4426950408889634
SAMPLE_ROWS = 16
ROUTER_LANES = 128
VMEM_LIMIT =56 * 1024 * 1024

NT = (((1,), (1,)), ((), ()))
TN = (((0,), (0,)), ((), ()))


def _params(*sem):
    return pltpu.CompilerParams(dimension_semantics=sem, vmem_limit_bytes=VMEM_LIMIT)


def _const_spec(shape):
    nd = len(shape)
    return pl.BlockSpec(shape, lambda *_: (0,) * nd, pipeline_mode=pl.Buffered(1))


def _layer_norm_rows(z, g, b):
    mu = jnp.mean(z, axis=-1, keepdims=True)
    zc = z - mu
    var = jnp.mean(zc * zc, axis=-1, keepdims=True)
    return zc * lax.rsqrt(var + LN_EPS) * g + b


def _sigmoid(x):
    return 1.0 / (1.0 + jnp.exp(-x))


def _in_proj_kernel(x_ref, w_ref, *refs, transposed, tm):
    xb = x_ref[...].astype(BF16)

    def proj(c):
        return jnp.dot(xb, w_ref[:, c * RET_WIDTH:(c + 1) * RET_WIDTH], preferred_element_type=F32)

    if transposed:
        wT_ref, rq_ref, rk_ref, rv_ref, rg_ref, kT_ref, vT_ref, k16_ref, qT16_ref, vT16_ref, kmean_ref = refs
        projT = lambda c: lax.dot_general(wT_ref[c], xb, NT, preferred_element_type=F32)
        qT16_ref[...] = (projT(0) * MOBA_DH ** -0.5).astype(BF16)
        kT = projT(1)
        kT_ref[0] = kT
        k = kT.T
        k16_ref[...] = k.astype(BF16)
        nb = tm // MOBA_BLOCK
        kmean_ref[0] = jnp.mean(k.reshape(nb, MOBA_BLOCK, MOBA_WIDTH), axis=1)
        vT = projT(2)
        vT_ref[0] = vT
        vT16_ref[...] = vT.astype(BF16)
    else:
        rq_ref, rk_ref, rv_ref, rg_ref, mq_ref, k_ref, v_ref = refs
        mq_ref[...] = (proj(4) * MOBA_DH ** -0.5).astype(BF16)
        k_ref[...] = proj(5)
        v_ref[...] = proj(6)
    rq_ref[...] = proj(0).astype(BF16)
    rk_ref[...] = (proj(1) * RET_DK ** -0.5).astype(BF16)
    rv_ref[...] = proj(2).astype(BF16)
    rg_ref[...] = proj(3)


def _in_proj(x, w16, wT16=None, *, rows=None, batch=1, tm):
    T = rows or x.shape[0]
    transposed = wT16 is not None
    row = lambda i: (i, 0)
    tile = lambda: pl.BlockSpec((tm, RET_WIDTH), row)
    in_specs = [pl.BlockSpec((tm, D_MODEL), row), _const_spec(w16.shape)]
    args = [x, w16]
    out_shape = [jax.ShapeDtypeStruct((T, RET_WIDTH), BF16)] * 3 + [jax.ShapeDtypeStruct((T, RET_WIDTH), F32)]
    out_specs = [tile() for _ in range(4)]
    if transposed:
        S = T // batch
        per_b = S // tm
        in_specs.append(_const_spec(wT16.shape))
        args.append(wT16)
        col = lambda i: (0, i)
        by_batch = lambda i: (i // per_b, 0, i % per_b)
        out_shape += [jax.ShapeDtypeStruct((batch, MOBA_WIDTH, S), F32)] * 2 + [
            jax.ShapeDtypeStruct((T, MOBA_WIDTH), BF16),
            jax.ShapeDtypeStruct((MOBA_WIDTH, T), BF16), jax.ShapeDtypeStruct((MOBA_WIDTH, T), BF16),
            jax.ShapeDtypeStruct((T // tm, tm // MOBA_BLOCK, MOBA_WIDTH), F32)]
        out_specs += [pl.BlockSpec((1, MOBA_WIDTH, tm), by_batch), pl.BlockSpec((1, MOBA_WIDTH, tm), by_batch),
                      tile(), pl.BlockSpec((MOBA_WIDTH, tm), col), pl.BlockSpec((MOBA_WIDTH, tm), col),
                      pl.BlockSpec((1, tm // MOBA_BLOCK, MOBA_WIDTH), lambda i: (i, 0, 0))]
    else:
        out_shape += [jax.ShapeDtypeStruct((T, MOBA_WIDTH), BF16)] + [jax.ShapeDtypeStruct((T, MOBA_WIDTH), F32)] * 2
        out_specs += [tile() for _ in range(3)]
    outs = pl.pallas_call(
        functools.partial(_in_proj_kernel, transposed=transposed, tm=tm),
        grid=(T // tm,), in_specs=in_specs, out_specs=out_specs, out_shape=out_shape,
        compiler_params=_params("arbitrary"), name="in_proj",
    )(*args)
    if transposed:
        outs = list(outs)
        outs[-1] = outs[-1].reshape(T // MOBA_BLOCK, MOBA_WIDTH)
    return outs


def _ret_tables(L, rows):
    log_g = jnp.log(jnp.asarray(1.0 - 2.0 ** (-5.0 - np.arange(RET_HEADS)), dtype=F32))
    idx = jnp.arange(L, dtype=F32)
    diff = idx[:, None] - idx[None, :]
    decay = jnp.where(diff >= 0, jnp.exp(log_g[:, None, None] * jnp.maximum(diff, 0.0)), 0.0)
    qdec = jnp.exp(log_g[:, None] * (idx + 1.0))
    kdec = jnp.exp(log_g[:, None] * (L - 1.0 - idx))
    g_chunk = jnp.exp(log_g * L)
    pad = rows - L
    decay = jnp.pad(decay, ((0, 0), (0, pad), (0, pad)))
    lanes = lambda t: jnp.broadcast_to(jnp.pad(t, ((0, 0), (0, pad)))[:, :, None], (RET_HEADS, rows, LANES))
    return decay, lanes(qdec), lanes(kdec), jnp.broadcast_to(g_chunk[:, None, None], (RET_HEADS, 1, LANES))


def _ret_head(q, k, v, state, decay, qdec, kdec, g_chunk):
    scores = lax.dot_general(q, k, NT, preferred_element_type=F32) * decay
    inner = jnp.dot(scores.astype(BF16), v, preferred_element_type=F32)
    q_dec = (q.astype(F32) * qdec).astype(BF16)
    cross = jnp.dot(q_dec, state.astype(BF16), preferred_element_type=F32)
    k_dec = (k.astype(F32) * kdec).astype(BF16)
    new_state = state * g_chunk + lax.dot_general(k_dec, v, TN, preferred_element_type=F32)
    return inner + cross, new_state


def _ret_gate(o, rg, gn):
    mu = jnp.mean(o, axis=-1, keepdims=True)
    oc = o - mu
    var = jnp.mean(oc * oc, axis=-1, keepdims=True)
    return (rg * _sigmoid(rg)) * (oc * lax.rsqrt(var + LN_EPS) * gn)


def _ret_prompt_kernel(q_ref, k_ref, v_ref, rg_ref, gn_ref, decay_ref, qdec_ref, kdec_ref, gc_ref,
                       o_ref, state_ref):
    @pl.when(pl.program_id(0) == 0)
    def _():
        state_ref[...] = jnp.zeros_like(state_ref)

    for b in range(q_ref.shape[0]):
        for h in range(RET_HEADS):
            sl = slice(h * RET_DK, (h + 1) * RET_DK)
            o, new_state = _ret_head(q_ref[b, :, sl], k_ref[b, :, sl], v_ref[b, :, sl], state_ref[b, h],
                                     decay_ref[h], qdec_ref[h], kdec_ref[h], gc_ref[h])
            state_ref[b, h] = new_state
            o_ref[b, :, sl] = _ret_gate(o, rg_ref[b, :, sl], gn_ref[:, sl]).astype(BF16)


def _ret_prompt(rq, rk, rv, rg, gn, batch):
    T = rq.shape[0]
    S = T // batch
    tables = _ret_tables(RET_CHUNK, RET_CHUNK)
    by_batch = lambda t: t.reshape(batch, S, RET_WIDTH)
    tile = pl.BlockSpec((batch, RET_CHUNK, RET_WIDTH), lambda c: (0, c, 0))
    o_r, state = pl.pallas_call(
        _ret_prompt_kernel,
        grid=(S // RET_CHUNK,),
        in_specs=[tile, tile, tile, tile, _const_spec(gn.shape)] + [_const_spec(t.shape) for t in tables],
        out_specs=[tile, pl.BlockSpec((batch, RET_HEADS, RET_DK, RET_DK), lambda c: (0, 0, 0, 0))],
        out_shape=[jax.ShapeDtypeStruct((batch, S, RET_WIDTH), BF16),
                   jax.ShapeDtypeStruct((batch, RET_HEADS, RET_DK, RET_DK), F32)],
        compiler_params=_params("arbitrary"), name="ret_prompt",
    )(by_batch(rq), by_batch(rk), by_batch(rv), by_batch(rg), gn, *tables)
    return o_r.reshape(T, RET_WIDTH), state


def _ret_sample_kernel(q_ref, k_ref, v_ref, rg_ref, gn_ref, s_ref, decay_ref, qdec_ref, kdec_ref, gc_ref,
                       o_ref, snew_ref, *, bt, L):
    for bi in range(bt):
        for h in range(RET_HEADS):
            sl = slice(h * RET_DK, (h + 1) * RET_DK)
            o, new_state = _ret_head(q_ref[bi, :, sl], k_ref[bi, :, sl], v_ref[bi, :, sl], s_ref[0, bi, h],
                                     decay_ref[h], qdec_ref[h], kdec_ref[h], gc_ref[h])
            snew_ref[bi, h] = new_state
            o_ref[bi, :, sl] = _ret_gate(o[:L], rg_ref[bi, :, sl], gn_ref[:, sl])


def _ret_sample(rq, rk, rv, rg, gn, state, layer, *, bt=8):
    db, L = rg.shape[0], rg.shape[1]
    tables = _ret_tables(L, SAMPLE_ROWS)
    b3 = lambda i: (i, 0, 0)
    qkv = pl.BlockSpec((bt, SAMPLE_ROWS, RET_WIDTH), b3)
    st = pl.BlockSpec((bt, RET_HEADS, RET_DK, RET_DK), lambda i: (i, 0, 0, 0))
    st_in = pl.BlockSpec((1, bt, RET_HEADS, RET_DK, RET_DK), lambda i: (layer, i, 0, 0, 0))
    return pl.pallas_call(
        functools.partial(_ret_sample_kernel, bt=bt, L=L),
        grid=(db // bt,),
        in_specs=[qkv, qkv, qkv, pl.BlockSpec((bt, L, RET_WIDTH), b3), _const_spec(gn.shape), st_in]
        + [_const_spec(t.shape) for t in tables],
        out_specs=[pl.BlockSpec((bt, L, RET_WIDTH), b3), st],
        out_shape=[jax.ShapeDtypeStruct((db, L, RET_WIDTH), F32),
                   jax.ShapeDtypeStruct((db, RET_HEADS, RET_DK, RET_DK), F32)],
        compiler_params=_params("arbitrary"), name="ret_sample",
    )(rq, rk, rv, rg, gn, state, *tables)


def _alibi_slopes():
    return 2.0 ** (-8.0 * np.arange(1, MOBA_HEADS + 1) / MOBA_HEADS)


def _moba_prompt_tables():
    slopes = jnp.asarray(_alibi_slopes(), dtype=F32)[:, None, None]
    kk = jnp.arange(MOBA_BLOCK, dtype=F32)[:, None]
    qq = jnp.arange(MOBA_BLOCK, dtype=F32)[None, :]
    dist = (qq - kk)[None]
    past = -(slopes * dist) * LOG2E
    own = jnp.where(dist >= 0, past, -jnp.inf)
    block_step = -(slopes * float(MOBA_BLOCK)) * LOG2E
    return past, own, jnp.broadcast_to(block_step, (MOBA_HEADS, 1, MOBA_BLOCK))


def _for_blocks(n, body):
    def two(i, carry):
        body(2 * i)
        body(2 * i + 1)
        return carry

    lax.fori_loop(0, lax.shift_right_logical(n, 1), two, 0)

    @pl.when(lax.bitwise_and(n, 1) == 1)
    def _():
        body(n - 1)


def _moba_prompt_kernel(qT_ref, k_ref, vT_ref, kmean_ref, past_ref, own_ref, step_ref, o_ref,
                        q_sc, s_sc, m_sc, acc_sc, term_sc):
    j = pl.program_id(2)
    nb = kmean_ref.shape[0]
    dh_row = lax.broadcasted_iota(jnp.int32, (PAIR, MOBA_BLOCK), 0)
    blk = lax.broadcasted_iota(jnp.int32, (nb, MOBA_BLOCK), 0)
    own_start = pl.multiple_of(j * MOBA_BLOCK, MOBA_BLOCK)
    pair_cols = lambda h: slice((h // 2) * PAIR, (h // 2 + 1) * PAIR)

    for h in range(GROUP_HEADS):
        qT = qT_ref[pair_cols(h), :]
        keep = (dh_row < MOBA_DH) if h % 2 == 0 else (dh_row >= MOBA_DH)
        qh = jnp.where(keep, qT, jnp.zeros_like(qT))
        q_sc[h] = qh
        gate = jnp.dot(kmean_ref[:, pair_cols(h)].astype(BF16), qh, preferred_element_type=F32)
        gate = jnp.where(blk < j, gate, -jnp.inf)
        sel = jnp.zeros(gate.shape, dtype=jnp.bool_)
        for _ in range(MOBA_TOPK):
            top = jnp.max(gate, axis=0, keepdims=True)
            first = jnp.min(jnp.where(gate == top, blk, nb), axis=0, keepdims=True)
            pick = jnp.logical_and(blk == first, top > -jnp.inf)
            sel = jnp.logical_or(sel, pick)
            gate = jnp.where(pick, -jnp.inf, gate)
        term_sc[h] = jnp.where(sel, (j - blk).astype(F32) * step_ref[h], jnp.where(blk == j, 0.0, -jnp.inf))
        s = jnp.dot(k_ref[pl.ds(own_start, MOBA_BLOCK), pair_cols(h)], qh, preferred_element_type=F32)
        s = s * LOG2E + own_ref[h]
        s_sc[h, j] = s
        m_sc[h] = jnp.max(s, axis=0, keepdims=True)
        acc_sc[h] = jnp.zeros_like(acc_sc[h])

    def scores(jj):
        start = pl.multiple_of(jj * MOBA_BLOCK, MOBA_BLOCK)
        for h in range(GROUP_HEADS):
            s = jnp.dot(k_ref[pl.ds(start, MOBA_BLOCK), pair_cols(h)], q_sc[h], preferred_element_type=F32)
            s = s * LOG2E + past_ref[h]
            s_sc[h, jj] = s
            m_sc[h] = jnp.maximum(m_sc[h], jnp.max(s, axis=0, keepdims=True) + term_sc[h, pl.ds(jj, 1), :])

    _for_blocks(j, scores)

    ones_rows = jnp.ones((DENOM_ROWS, MOBA_BLOCK), BF16)

    def apply_v(jj):
        start = pl.multiple_of(jj * MOBA_BLOCK, MOBA_BLOCK)
        for h in range(GROUP_HEADS):
            pexp = jnp.exp2(s_sc[h, jj] - (m_sc[h] - term_sc[h, pl.ds(jj, 1), :]))
            vT = jnp.concatenate([vT_ref[h * MOBA_DH:(h + 1) * MOBA_DH, pl.ds(start, MOBA_BLOCK)], ones_rows], axis=0)
            acc_sc[h] = acc_sc[h] + jnp.dot(vT, pexp.astype(BF16), preferred_element_type=F32)

    _for_blocks(j + 1, apply_v)

    def head_out(h):
        acc = acc_sc[h]
        return acc[:MOBA_DH] / acc[MOBA_DH:MOBA_DH + 1]

    for p in range(GROUP_HEADS // 2):
        outT = jnp.concatenate([head_out(2 * p), head_out(2 * p + 1)], axis=0)
        o_ref[:, p * PAIR:(p + 1) * PAIR] = outT.T.astype(BF16)


def _moba_prompt(qT16, k16, vT16, kmean, batch):
    T = k16.shape[0]
    S = T // batch
    nb = S // MOBA_BLOCK
    n_groups = MOBA_HEADS // GROUP_HEADS
    gw = GROUP_HEADS * MOBA_DH
    tables = _moba_prompt_tables()
    head_tile = lambda t: pl.BlockSpec((GROUP_HEADS,) + t.shape[1:], lambda b, g, j: (g, 0, 0))
    return pl.pallas_call(
        _moba_prompt_kernel,
        grid=(batch, n_groups, nb),
        in_specs=[pl.BlockSpec((gw, MOBA_BLOCK), lambda b, g, j: (g, b * nb + j)),
                  pl.BlockSpec((S, gw), lambda b, g, j: (b, g)),
                  pl.BlockSpec((gw, S), lambda b, g, j: (g, b)),
                  pl.BlockSpec((nb, gw), lambda b, g, j: (b, g))]
        + [head_tile(t) for t in tables],
        out_specs=pl.BlockSpec((MOBA_BLOCK, gw), lambda b, g, j: (b * nb + j, g)),
        out_shape=jax.ShapeDtypeStruct((T, MOBA_WIDTH), BF16),
        scratch_shapes=[pltpu.VMEM((GROUP_HEADS, PAIR, MOBA_BLOCK), BF16),
                        pltpu.VMEM((GROUP_HEADS, nb, MOBA_BLOCK, MOBA_BLOCK), F32),
                        pltpu.VMEM((GROUP_HEADS, 1, MOBA_BLOCK), F32),
                        pltpu.VMEM((GROUP_HEADS, MOBA_DH + DENOM_ROWS, MOBA_BLOCK), F32),
                        pltpu.VMEM((GROUP_HEADS, nb, MOBA_BLOCK), F32)],
        compiler_params=_params("arbitrary", "arbitrary", "arbitrary"), name="moba_prompt",
    )(qT16, k16, vT16, kmean, *tables)


def _moba_sample_tables(L, n_pages):
    past_len = n_pages * PAGE_SIZE
    slopes = _alibi_slopes()
    row_slope = np.zeros((N_PAIRS, SAMPLE_ROWS), np.float64)
    row_t = np.zeros((SAMPLE_ROWS,), np.float64)
    for p in range(N_PAIRS):
        row_slope[p, :L] = slopes[2 * p]
        row_slope[p, L:2 * L] = slopes[2 * p + 1]
    row_t[:L] = np.arange(L)
    row_t[L:2 * L] = np.arange(L)
    row_slope = jnp.asarray(row_slope, dtype=F32)[:, :, None]
    q_pos = jnp.asarray(past_len + row_t, dtype=F32)[None, :, None]
    key_pos = jnp.arange(past_len, dtype=F32)[None, None, :]
    past = -(row_slope * (q_pos - key_pos))
    new_pos = jnp.arange(SAMPLE_ROWS, dtype=F32)[None, None, :]
    dist_new = jnp.asarray(row_t, dtype=F32)[None, :, None] - new_pos
    valid = jnp.logical_and(dist_new >= 0, new_pos < L)
    new = jnp.where(valid, -(row_slope * dist_new), -jnp.inf)
    return past, new


def _moba_sample_kernel(pt_ref, q_ref, kn_ref, vn_ref, past_ref, new_ref, *refs, L, n_pages):
    k_pages, v_pages, o_ref = refs[:n_pages], refs[n_pages:2 * n_pages], refs[2 * n_pages]
    del pt_ref
    n_blk = n_pages * PAGE_SIZE // MOBA_BLOCK
    per_blk = MOBA_BLOCK // PAGE_SIZE
    row = lax.broadcasted_iota(jnp.int32, (SAMPLE_ROWS, PAIR), 0)
    lane = lax.broadcasted_iota(jnp.int32, (SAMPLE_ROWS, PAIR), 1)
    keep = jnp.logical_or(jnp.logical_and(row < L, lane < MOBA_DH),
                          jnp.logical_and(jnp.logical_and(row >= L, row < 2 * L), lane >= MOBA_DH))
    for p in range(N_PAIRS):
        cols = slice(p * PAIR, (p + 1) * PAIR)
        q = q_ref[0, :, cols]
        qm = jnp.where(keep, q, jnp.zeros_like(q))
        kT = jnp.concatenate([k_pages[s][0, 0, cols, :] for s in range(n_pages)], axis=1).astype(BF16)
        raw = jnp.dot(qm, kT, preferred_element_type=F32)
        blk_lanes = lambda jj: slice(jj * MOBA_BLOCK, (jj + 1) * MOBA_BLOCK)
        gate = [jnp.sum(raw[:, blk_lanes(jj)], axis=-1, keepdims=True) * (1.0 / MOBA_BLOCK) for jj in range(n_blk)]
        sel = []
        for jj in range(n_blk):
            ahead = jnp.zeros(gate[jj].shape, F32)
            for kk in range(n_blk):
                if kk == jj:
                    continue
                beats = (gate[kk] >= gate[jj]) if kk < jj else (gate[kk] > gate[jj])
                ahead = ahead + jnp.where(beats, 1.0, 0.0)
            sel.append(ahead < float(min(MOBA_TOPK, n_blk)))
        logit = jnp.concatenate(
            [jnp.where(sel[jj], raw[:, blk_lanes(jj)] + past_ref[p, :, blk_lanes(jj)], -jnp.inf)
             for jj in range(n_blk)], axis=1)
        kn = kn_ref[0, :, cols]
        s_new = lax.dot_general(qm, kn, NT, preferred_element_type=F32) + new_ref[p]
        m = jnp.maximum(jnp.max(s_new, axis=-1, keepdims=True), jnp.max(logit, axis=-1, keepdims=True))
        p_new = jnp.exp(s_new - m)
        pexp = jnp.exp(logit - m)
        denom = jnp.sum(p_new, axis=-1, keepdims=True) + jnp.sum(pexp, axis=-1, keepdims=True)
        vT = jnp.concatenate([v_pages[s][0, 0, cols, :] for s in range(n_pages)], axis=1).astype(BF16)
        acc = (jnp.dot(p_new.astype(BF16), vn_ref[0, :, cols], preferred_element_type=F32)
               + lax.dot_general(pexp.astype(BF16), vT, NT, preferred_element_type=F32))
        out = acc / denom
        o_ref[0, :, cols] = jnp.where(lane[:L] < MOBA_DH, out[:L], out[L:2 * L])


def _moba_sample(q16, kn16, vn16, cache_kT, cache_vT, page_table, layer, L):
    db, n_pages = page_table.shape
    past, new = _moba_sample_tables(L, n_pages)
    b3 = lambda b, pt: (b, 0, 0)
    row_spec = pl.BlockSpec((1, SAMPLE_ROWS, MOBA_WIDTH), b3)

    def page_spec(s):
        return pl.BlockSpec((1, 1, MOBA_WIDTH, PAGE_SIZE), lambda b, pt: (layer, pt[b, s], 0, 0))

    in_specs = [row_spec, row_spec, row_spec,
                pl.BlockSpec(past.shape, lambda b, pt: (0, 0, 0)), pl.BlockSpec(new.shape, lambda b, pt: (0, 0, 0))]
    in_specs += [page_spec(s) for s in range(n_pages)] * 2
    grid_spec = pltpu.PrefetchScalarGridSpec(
        num_scalar_prefetch=1, grid=(db,), in_specs=in_specs,
        out_specs=pl.BlockSpec((1, L, MOBA_WIDTH), b3))
    return pl.pallas_call(
        functools.partial(_moba_sample_kernel, L=L, n_pages=n_pages),
        grid_spec=grid_spec,
        out_shape=jax.ShapeDtypeStruct((db, L, MOBA_WIDTH), F32),
        compiler_params=_params("arbitrary"), name="moba_sample",
    )(page_table, q16, kn16, vn16, past, new, *([cache_kT] * n_pages), *([cache_vT] * n_pages))


def _route(x, wrh_ref, wrl_ref, br_ref, group=None):
    tm = x.shape[0]
    hi = x.astype(BF16)
    lo = (x - hi.astype(F32)).astype(BF16)
    logit = (jnp.dot(hi, wrh_ref[...], preferred_element_type=F32)
             + (jnp.dot(hi, wrl_ref[...], preferred_element_type=F32)
                + jnp.dot(lo, wrh_ref[...], preferred_element_type=F32))) + br_ref[...]
    lane = lax.broadcasted_iota(jnp.int32, (tm, ROUTER_LANES), 1)
    neg = -jnp.inf
    gl = jnp.where(lane < N_GROUPS, logit, neg)
    gmax = jnp.max(gl, axis=-1, keepdims=True)
    g_sum = jnp.sum(jnp.exp(gl - gmax), axis=-1, keepdims=True)
    if group is None:
        gidx = jnp.min(jnp.where(gl == gmax, lane, ROUTER_LANES), axis=-1, keepdims=True)
        g_w = 1.0 / g_sum
    else:
        gidx = group
        g_w = jnp.exp(jnp.sum(jnp.where(lane == group, logit, 0.0), axis=-1, keepdims=True) - gmax) / g_sum
    first = N_GROUPS + EXPERTS_PER_GROUP * gidx
    in_group = jnp.logical_and(lane >= first, lane < first + EXPERTS_PER_GROUP)
    el = jnp.where(in_group, logit, neg)
    e1 = jnp.max(el, axis=-1, keepdims=True)
    i1 = jnp.min(jnp.where(el == e1, lane, ROUTER_LANES), axis=-1, keepdims=True)
    el2 = jnp.where(lane == i1, neg, el)
    e2 = jnp.max(el2, axis=-1, keepdims=True)
    i2 = jnp.min(jnp.where(el2 == e2, lane, ROUTER_LANES), axis=-1, keepdims=True)
    t = jnp.exp(e2 - e1)
    return gidx, i1, i2, g_w / (1.0 + t), g_w * t / (1.0 + t), lane


def _expert_lanes(weights, first_lane, tm):
    return jnp.concatenate([jnp.broadcast_to(weights[:, first_lane + e:first_lane + e + 1], (tm, EXPERT_HIDDEN))
                            for e in range(EXPERTS_PER_GROUP)], axis=1)


def _group_experts(xb, cexp, weg, weu, wed):
    hg = jnp.dot(xb, weg, preferred_element_type=F32)
    hu = jnp.dot(xb, weu, preferred_element_type=F32)
    hid = (hg * _sigmoid(hg)) * hu
    return jnp.dot((hid * cexp).astype(BF16), wed, preferred_element_type=F32)


def _merge_kernel(x_ref, or_ref, om_ref, wg_ref, bg_ref, wr_ref, wm_ref, wo_ref, g_ref, b_ref, *refs, with_route):
    x = x_ref[...]
    gates = _sigmoid(jnp.dot(x.astype(BF16), wg_ref[...], preferred_element_type=F32) + bg_ref[...])
    br = jnp.dot(or_ref[...], wr_ref[...], preferred_element_type=F32)
    bm = jnp.dot(om_ref[...], wm_ref[...], preferred_element_type=F32)
    merged = gates[:, :D_MODEL] * br + gates[:, D_MODEL:] * bm
    y = jnp.dot(merged.astype(BF16), wo_ref[...], preferred_element_type=F32)
    x1 = _layer_norm_rows(ALPHA * x + y, g_ref[...], b_ref[...])
    if not with_route:
        refs[0][...] = x1
        return
    wrh_ref, wrl_ref, brt_ref, o_ref, group_ref = refs
    o_ref[...] = x1
    gidx = _route(x1, wrh_ref, wrl_ref, brt_ref)[0]
    group_ref[...] = jnp.broadcast_to(gidx, group_ref.shape)


def _merge(x, o_r, o_m, wg16, bg, wr16, wm16, wo16, g, b, router=None, *, tm):
    T = o_r.shape[0]
    row = lambda i: (i, 0)
    consts = [wg16, bg, wr16, wm16, wo16, g, b] + list(router or ())
    out_shape = [jax.ShapeDtypeStruct((T, D_MODEL), F32)]
    out_specs = [pl.BlockSpec((tm, D_MODEL), row)]
    if router:
        out_shape.append(jax.ShapeDtypeStruct((T, ROUTER_LANES), jnp.int32))
        out_specs.append(pl.BlockSpec((tm, ROUTER_LANES), row))
    outs = pl.pallas_call(
        functools.partial(_merge_kernel, with_route=bool(router)),
        grid=(T // tm,),
        in_specs=[pl.BlockSpec((tm, D_MODEL), row), pl.BlockSpec((tm, RET_WIDTH), row),
                  pl.BlockSpec((tm, MOBA_WIDTH), row)] + [_const_spec(c.shape) for c in consts],
        out_specs=out_specs, out_shape=out_shape,
        compiler_params=_params("arbitrary"), name="merge",
    )(x, o_r, o_m, *consts)
    return outs if router else outs[0]


def _moe_kernel(x_ref, wrh_ref, wrl_ref, br_ref, weg_ref, weu_ref, wed_ref, g_ref, b_ref, o_ref):
    x = x_ref[...]
    tm = x.shape[0]
    gidx, i1, i2, w1, w2, lane = _route(x, wrh_ref, wrl_ref, br_ref)
    comb = jnp.where(lane == i1, w1, 0.0) + jnp.where(lane == i2, w2, 0.0)
    xb = x.astype(BF16)
    acc = jnp.zeros((tm, D_MODEL), F32)
    for g in range(N_GROUPS):
        cexp = _expert_lanes(comb, N_GROUPS + g * EXPERTS_PER_GROUP, tm)
        acc = acc + _group_experts(xb, cexp, weg_ref[g], weu_ref[g], wed_ref[g])
    o_ref[...] = _layer_norm_rows(ALPHA * x + acc, g_ref[...], b_ref[...])


def _moe(x, wr_hi, wr_lo, br, weg16, weu16, wed16, g, b, *, tm):
    T = x.shape[0]
    row = lambda i: (i, 0)
    consts = [wr_hi, wr_lo, br, weg16, weu16, wed16, g, b]
    return pl.pallas_call(
        _moe_kernel,
        grid=(T // tm,),
        in_specs=[pl.BlockSpec((tm, D_MODEL), row)] + [_const_spec(c.shape) for c in consts],
        out_specs=pl.BlockSpec((tm, D_MODEL), row),
        out_shape=jax.ShapeDtypeStruct((T, D_MODEL), F32),
        compiler_params=_params("arbitrary"), name="moe",
    )(x, *consts)


def _experts_by_group_kernel(w_ref, o_ref, *, side_by_side):
    for e in range(EXPERTS_PER_GROUP):
        cols = slice(e * EXPERT_HIDDEN, (e + 1) * EXPERT_HIDDEN)
        if side_by_side:
            o_ref[0, :, cols] = w_ref[0, e].astype(BF16)
        else:
            o_ref[0, cols, :] = w_ref[0, e].astype(BF16)


def _experts_by_group(w, layer, *, side_by_side):
    return pl.pallas_call(
        functools.partial(_experts_by_group_kernel, side_by_side=side_by_side),
        grid=(N_GROUPS,),
        in_specs=[pl.BlockSpec((1, EXPERTS_PER_GROUP) + w.shape[2:], lambda g: (layer, g, 0, 0))],
        out_specs=pl.BlockSpec((1, D_MODEL, GROUP_HIDDEN) if side_by_side else (1, GROUP_HIDDEN, D_MODEL),
                               lambda g: (g, 0, 0)),
        out_shape=jax.ShapeDtypeStruct((N_GROUPS, D_MODEL, GROUP_HIDDEN) if side_by_side
                                       else (N_GROUPS, GROUP_HIDDEN, D_MODEL), BF16),
        compiler_params=_params("arbitrary"), name="experts_by_group",
    )(w)


def _group_plan(gid, tm):
    T = gid.shape[0]
    groups = jnp.arange(N_GROUPS, dtype=jnp.int32)
    member = (gid[:, None] == groups[None, :]).astype(jnp.int32)
    running = jnp.cumsum(member, axis=0)
    counts = running[-1]
    rank = jnp.sum(running * member, axis=1) - 1
    padded = ((counts + tm - 1) // tm) * tm
    ends = jnp.cumsum(padded)
    starts = ends - padded
    pos = jnp.sum(starts[None, :] * member, axis=1) + rank
    n_tiles = T // tm + N_GROUPS
    rows = jnp.arange(T, dtype=jnp.int32)
    pad_dst = T + jnp.arange(tm, dtype=jnp.int32)
    src = jnp.zeros((n_tiles * tm,), jnp.int32).at[pos].set(rows, unique_indices=True)
    tile_start = jnp.arange(n_tiles, dtype=jnp.int32) * tm
    tile_group = jnp.minimum(jnp.sum((tile_start[:, None] >= ends[None, :]).astype(jnp.int32), axis=1), N_GROUPS - 1)
    in_tile = (tile_group[:, None] == groups[None, :]).astype(jnp.int32)
    fill = jnp.sum(in_tile * (starts + counts)[None, :], axis=1)[:, None] - tile_start[:, None]
    real = (jnp.arange(tm, dtype=jnp.int32)[None, :] < fill).reshape(n_tiles * tm)
    dst = jnp.where(real, src, jnp.tile(pad_dst, n_tiles))
    dst = jnp.concatenate([pad_dst, dst]).reshape(n_tiles + 1, 1, tm)
    return src.reshape(n_tiles, 1, tm), dst, tile_group


def _moe_grouped_kernel(tg_ref, src_now_ref, src_next_ref, dst_prev_ref, dst_now_ref,
                        x_hbm, wrh_ref, wrl_ref, br_ref, weg_ref, weu_ref, wed_ref, g_ref, b_ref, out_hbm,
                        xbuf, obuf, gsem, ssem, *, tm):
    i = pl.program_id(0)
    last = pl.num_programs(0) - 1
    slot = lax.rem(i, 2)
    other = 1 - slot

    def fetch_rows(idx_ref, s):
        for r in range(tm):
            t = idx_ref[0, 0, r]
            pltpu.make_async_copy(x_hbm.at[pl.ds(t, 1)], xbuf.at[s, pl.ds(r, 1)], gsem.at[s]).start(priority=r % 2)

    def fetch_wait(s):
        pltpu.make_async_copy(x_hbm.at[pl.ds(0, tm)], xbuf.at[s], gsem.at[s]).wait()

    def write_rows(idx_ref, s):
        for r in range(tm):
            t = idx_ref[0, 0, r]
            pltpu.make_async_copy(obuf.at[s, pl.ds(r, 1)], out_hbm.at[pl.ds(t, 1)], ssem.at[s]).start(priority=r % 2)

    def write_wait(s):
        pltpu.make_async_copy(obuf.at[s], out_hbm.at[pl.ds(0, tm)], ssem.at[s]).wait()

    @pl.when(i == 0)
    def _():
        obuf[...] = jnp.zeros_like(obuf)
        fetch_rows(src_now_ref, 0)
        fetch_wait(0)

    fetch_rows(src_next_ref, other)
    write_rows(dst_prev_ref, other)
    x = xbuf[slot]
    group = tg_ref[i]
    _, i1, i2, w1, w2, lane = _route(x, wrh_ref, wrl_ref, br_ref, group=group)
    first = N_GROUPS + EXPERTS_PER_GROUP * group
    cexp = _expert_lanes(jnp.where(lane == i1 - first, w1, 0.0) + jnp.where(lane == i2 - first, w2, 0.0), 0, tm)
    acc = _group_experts(x.astype(BF16), cexp, weg_ref[0], weu_ref[0], wed_ref[0])
    obuf[slot] = _layer_norm_rows(ALPHA * x + acc, g_ref[...], b_ref[...])
    fetch_wait(other)
    write_wait(other)

    @pl.when(i == last)
    def _():
        write_rows(dst_now_ref, slot)
        write_wait(slot)


def _moe_grouped(x, T, src, dst, tile_group, wr_hi, wr_lo, br, weg16, weu16, wed16, g, b, *, tm):
    n_tiles = src.shape[0]
    smem_tile = lambda fn: pl.BlockSpec((1, 1, tm), fn, memory_space=pltpu.SMEM)
    by_group = lambda i, tg: (tg[i], 0, 0)
    w_spec = pl.BlockSpec((1, D_MODEL, GROUP_HIDDEN), by_group)
    fixed = lambda t: pl.BlockSpec(t.shape, lambda i, tg: (0, 0))
    grid_spec = pltpu.PrefetchScalarGridSpec(
        num_scalar_prefetch=1, grid=(n_tiles,),
        in_specs=[smem_tile(lambda i, tg: (i, 0, 0)),
                  smem_tile(lambda i, tg: (jnp.minimum(i + 1, n_tiles - 1), 0, 0)),
                  smem_tile(lambda i, tg: (i, 0, 0)), smem_tile(lambda i, tg: (i + 1, 0, 0)),
                  pl.BlockSpec(memory_space=pl.ANY), fixed(wr_hi), fixed(wr_lo), fixed(br),
                  w_spec, w_spec, pl.BlockSpec((1, GROUP_HIDDEN, D_MODEL), by_group), fixed(g), fixed(b)],
        out_specs=pl.BlockSpec(memory_space=pl.ANY),
        scratch_shapes=[pltpu.VMEM((2, tm, D_MODEL), F32), pltpu.VMEM((2, tm, D_MODEL), F32),
                        pltpu.SemaphoreType.DMA((2,)), pltpu.SemaphoreType.DMA((2,))])
    return pl.pallas_call(
        functools.partial(_moe_grouped_kernel, tm=tm),
        grid_spec=grid_spec,
        out_shape=jax.ShapeDtypeStruct((T + tm, D_MODEL), F32),
        compiler_params=_params("arbitrary"), name="moe_grouped",
    )(tile_group, src, src, dst, dst, x, wr_hi, wr_lo, br, weg16, weu16, wed16, g, b)


def _layer_weights(l, w_in, b_merge, gn_g, w_br_ret, w_br_moba, w_out, ln1_g, ln1_b, w_router_group,
                   b_router_group, w_router_expert, b_router_expert, w_exp_gate, w_exp_up, w_exp_down,
                   ln2_g, ln2_b):
    w = w_in[l]
    q0 = 4 * RET_WIDTH
    n_exp = N_GROUPS * EXPERTS_PER_GROUP
    w_r = jnp.concatenate(
        [w_router_group[l], w_router_expert[l].transpose(1, 0, 2).reshape(D_MODEL, n_exp),
         jnp.zeros((D_MODEL, ROUTER_LANES - N_GROUPS - n_exp), F32)], axis=1)
    w_r_hi = w_r.astype(BF16)
    w_r_lo = (w_r - w_r_hi.astype(F32)).astype(BF16)
    b_r = jnp.concatenate([b_router_group[l], b_router_expert[l].reshape(n_exp),
                           jnp.zeros((ROUTER_LANES - N_GROUPS - n_exp,), F32)])[None, :]
    return dict(
        w_proj=w[:, :N_PROJ].astype(BF16),
        w_ret=w[:, :q0].astype(BF16),
        w_mobaT=w[:, q0:N_PROJ].T.reshape(3, MOBA_WIDTH, D_MODEL).astype(BF16),
        w_gate=w[:, N_PROJ:].astype(BF16), b_gate=b_merge[l][None, :],
        gn=gn_g[l][None, :],
        w_br_ret=w_br_ret[l].astype(BF16), w_br_moba=w_br_moba[l].astype(BF16), w_out=w_out[l].astype(BF16),
        ln1_g=ln1_g[l][None, :], ln1_b=ln1_b[l][None, :],
        w_r_hi=w_r_hi, w_r_lo=w_r_lo, b_r=b_r,
        w_eg=_experts_by_group(w_exp_gate, l, side_by_side=True),
        w_eu=_experts_by_group(w_exp_up, l, side_by_side=True),
        w_ed=_experts_by_group(w_exp_down, l, side_by_side=False),
        ln2_g=ln2_g[l][None, :], ln2_b=ln2_b[l][None, :])


def _pad_rows(t, rows):
    return jnp.pad(t, ((0, 0), (0, rows - t.shape[1]), (0, 0)))


def kernel(x_prompt, x_sample, cache_k, cache_v, state_ret, page_table, w_in, b_merge, gn_g, w_br_ret, w_br_moba, w_out, ln1_g, ln1_b, w_router_group, b_router_group, w_router_expert, b_router_expert, w_exp_gate, w_exp_up, w_exp_down, ln2_g, ln2_b):
    B, S, _ = x_prompt.shape
    DB, L, _ = x_sample.shape
    Tp, Ts = B * S, DB * L
    depth = w_in.shape[0]
    n_pool = cache_k.shape[1]
    page_major = lambda c: c.transpose(0, 1, 3, 4, 2).reshape(depth, n_pool, MOBA_WIDTH, PAGE_SIZE)
    cache_kT, cache_vT = page_major(cache_k), page_major(cache_v)
    xp = x_prompt.reshape(Tp, D_MODEL)
    xs = x_sample.reshape(Ts, D_MODEL)
    tm_p = min(512, Tp)
    tm_s = min(256, Ts)
    tm_moe = 256
    outs = [[] for _ in range(6)]
    for l in range(depth):
        W = _layer_weights(l, w_in, b_merge, gn_g, w_br_ret, w_br_moba, w_out, ln1_g, ln1_b, w_router_group,
                           b_router_group, w_router_expert, b_router_expert, w_exp_gate, w_exp_up, w_exp_down,
                           ln2_g, ln2_b)
        (rq, rk, rv, rg, kT_p, vT_p, k16, qT16, vT16, kmean) = _in_proj(
            xp, W["w_ret"], W["w_mobaT"], rows=Tp, batch=B, tm=tm_p)
        o_r, s_p = _ret_prompt(rq, rk, rv, rg, W["gn"], B)
        o_m = _moba_prompt(qT16, k16, vT16, kmean, B)
        router = (W["w_r_hi"], W["w_r_lo"], W["b_r"])
        x1, group = _merge(xp, o_r, o_m, W["w_gate"], W["b_gate"], W["w_br_ret"], W["w_br_moba"], W["w_out"],
                           W["ln1_g"], W["ln1_b"], router=router, tm=tm_p)
        plan = _group_plan(group[:, 0], tm_moe)
        xp = _moe_grouped(x1, Tp, *plan, *router, W["w_eg"], W["w_eu"], W["w_ed"], W["ln2_g"], W["ln2_b"], tm=tm_moe)
        (rq, rk, rv, rg, mq, k_s, v_s) = _in_proj(xs, W["w_proj"], tm=tm_s)
        r3 = lambda t: t.reshape(DB, L, t.shape[-1])
        o_r, s_s = _ret_sample(_pad_rows(r3(rq), SAMPLE_ROWS), _pad_rows(r3(rk), SAMPLE_ROWS),
                               _pad_rows(r3(rv), SAMPLE_ROWS), r3(rg), W["gn"], state_ret, l)
        mq3 = r3(mq)
        q16 = _pad_rows(jnp.concatenate([mq3, mq3], axis=1), SAMPLE_ROWS)
        o_m = _moba_sample(q16, _pad_rows(r3(k_s).astype(BF16), SAMPLE_ROWS),
                           _pad_rows(r3(v_s).astype(BF16), SAMPLE_ROWS),
                           cache_kT, cache_vT, page_table, l, L)
        x1 = _merge(xs, o_r.reshape(Ts, RET_WIDTH).astype(BF16), o_m.reshape(Ts, MOBA_WIDTH).astype(BF16),
                    W["w_gate"], W["b_gate"], W["w_br_ret"], W["w_br_moba"], W["w_out"],
                    W["ln1_g"], W["ln1_b"], tm=tm_s)
        xs = _moe(x1, W["w_r_hi"], W["w_r_lo"], W["b_r"], W["w_eg"], W["w_eu"], W["w_ed"],
                  W["ln2_g"], W["ln2_b"], tm=tm_s)
        for lst, val in zip(outs, (kT_p, vT_p, s_p,
                                   k_s.reshape(DB, L, MOBA_HEADS, MOBA_DH), v_s.reshape(DB, L, MOBA_HEADS, MOBA_DH), s_s)):
            lst.append(val)
    kTp, vTp, sp, ksm, vsm, ssm = (jnp.stack(o) for o in outs)
    token_major = lambda t: t.reshape(depth, B, MOBA_HEADS, MOBA_DH, S).transpose(0, 1, 4, 2, 3)
    return (xp[:Tp].reshape(B, S, D_MODEL), xs.reshape(DB, L, D_MODEL), token_major(kTp), token_major(vTp), sp,
            ksm, vsm, ssm)
```

```python
import functools

import numpy as np
import jax
import jax.numpy as jnp
from jax import lax
from jax.experimental import pallas as pl
from jax.experimental.pallas import tpu as pltpu

F32 = jnp.float32
BF16 = jnp.bfloat16

D_MODEL = 1024
RET_HEADS = 4
RET_DK = 128
RET_WIDTH = RET_HEADS * RET_DK
RET_CHUNK = 128
MOBA_HEADS = 8
MOBA_DH = 64
MOBA_WIDTH = MOBA_HEADS * MOBA_DH
MOBA_BLOCK = 256
MOBA_TOPK = 3
PAGE_SIZE = 128
N_GROUPS = 4
EXPERTS_PER_GROUP = 8
EXPERT_HIDDEN = 128
GROUP_HIDDEN = EXPERTS_PER_GROUP * EXPERT_HIDDEN
DEPTH = 2
ALPHA = (2 * DEPTH) ** 0.25
LN_EPS = 1e-5
N_PROJ = 4 * RET_WIDTH + 3 * MOBA_WIDTH

LANES = 128
PAIR = 2 * MOBA_DH
N_PAIRS = MOBA_HEADS // 2
GROUP_HEADS = 4
DENOM_ROWS = 16
LOG2E = 1.4426950408889634
SAMPLE_ROWS = 16
ROUTER_LANES = 128
VMEM_LIMIT = 56 * 1024 * 1024

NT = (((1,), (1,)), ((), ()))
TN = (((0,), (0,)), ((), ()))


def _params(*sem):
    return pltpu.CompilerParams(dimension_semantics=sem, vmem_limit_bytes=VMEM_LIMIT)


def _const_spec(shape):
    nd = len(shape)
    return pl.BlockSpec(shape, lambda *_: (0,) * nd, pipeline_mode=pl.Buffered(1))


def _layer_norm_rows(z, g, b):
    mu = jnp.mean(z, axis=-1, keepdims=True)
    zc = z - mu
    var = jnp.mean(zc * zc, axis=-1, keepdims=True)
    return zc * lax.rsqrt(var + LN_EPS) * g + b


def _sigmoid(x):
    return 1.0 / (1.0 + jnp.exp(-x))


def _in_proj_kernel(x_ref, w_ref, *refs, transposed, tm):
    xb = x_ref[...].astype(BF16)

    def proj(c):
        return jnp.dot(xb, w_ref[:, c * RET_WIDTH:(c + 1) * RET_WIDTH], preferred_element_type=F32)

    if transposed:
        wT_ref, rq_ref, rk_ref, rv_ref, rg_ref, kT_ref, vT_ref, k16_ref, qT16_ref, vT16_ref, kmean_ref = refs
        projT = lambda c: lax.dot_general(wT_ref[c], xb, NT, preferred_element_type=F32)
        qT16_ref[...] = (projT(0) * MOBA_DH ** -0.5).astype(BF16)
        kT = projT(1)
        kT_ref[0] = kT
        k = kT.T
        k16_ref[...] = k.astype(BF16)
        nb = tm // MOBA_BLOCK
        kmean_ref[0] = jnp.mean(k.reshape(nb, MOBA_BLOCK, MOBA_WIDTH), axis=1)
        vT = projT(2)
        vT_ref[0] = vT
        vT16_ref[...] = vT.astype(BF16)
    else:
        rq_ref, rk_ref, rv_ref, rg_ref, mq_ref, k_ref, v_ref = refs
        mq_ref[...] = (proj(4) * MOBA_DH ** -0.5).astype(BF16)
        k_ref[...] = proj(5)
        v_ref[...] = proj(6)
    rq_ref[...] = proj(0).astype(BF16)
    rk_ref[...] = (proj(1) * RET_DK ** -0.5).astype(BF16)
    rv_ref[...] = proj(2).astype(BF16)
    rg_ref[...] = proj(3)


def _in_proj(x, w16, wT16=None, *, rows=None, batch=1, tm):
    T = rows or x.shape[0]
    transposed = wT16 is not None
    row = lambda i: (i, 0)
    tile = lambda: pl.BlockSpec((tm, RET_WIDTH), row)
    in_specs = [pl.BlockSpec((tm, D_MODEL), row), _const_spec(w16.shape)]
    args = [x, w16]
    out_shape = [jax.ShapeDtypeStruct((T, RET_WIDTH), BF16)] * 3 + [jax.ShapeDtypeStruct((T, RET_WIDTH), F32)]
    out_specs = [tile() for _ in range(4)]
    if transposed:
        S = T // batch
        per_b = S // tm
        in_specs.append(_const_spec(wT16.shape))
        args.append(wT16)
        col = lambda i: (0, i)
        by_batch = lambda i: (i // per_b, 0, i % per_b)
        out_shape += [jax.ShapeDtypeStruct((batch, MOBA_WIDTH, S), F32)] * 2 + [
            jax.ShapeDtypeStruct((T, MOBA_WIDTH), BF16),
            jax.ShapeDtypeStruct((MOBA_WIDTH, T), BF16), jax.ShapeDtypeStruct((MOBA_WIDTH, T), BF16),
            jax.ShapeDtypeStruct((T // tm, tm // MOBA_BLOCK, MOBA_WIDTH), F32)]
        out_specs += [pl.BlockSpec((1, MOBA_WIDTH, tm), by_batch), pl.BlockSpec((1, MOBA_WIDTH, tm), by_batch),
                      tile(), pl.BlockSpec((MOBA_WIDTH, tm), col), pl.BlockSpec((MOBA_WIDTH, tm), col),
                      pl.BlockSpec((1, tm // MOBA_BLOCK, MOBA_WIDTH), lambda i: (i, 0, 0))]
    else:
        out_shape += [jax.ShapeDtypeStruct((T, MOBA_WIDTH), BF16)] + [jax.ShapeDtypeStruct((T, MOBA_WIDTH), F32)] * 2
        out_specs += [tile() for _ in range(3)]
    outs = pl.pallas_call(
        functools.partial(_in_proj_kernel, transposed=transposed, tm=tm),
        grid=(T // tm,), in_specs=in_specs, out_specs=out_specs, out_shape=out_shape,
        compiler_params=_params("arbitrary"), name="in_proj",
    )(*args)
    if transposed:
        outs = list(outs)
        outs[-1] = outs[-1].reshape(T // MOBA_BLOCK, MOBA_WIDTH)
    return outs


def _ret_tables(L, rows):
    log_g = jnp.log(jnp.asarray(1.0 - 2.0 ** (-5.0 - np.arange(RET_HEADS)), dtype=F32))
    idx = jnp.arange(L, dtype=F32)
    diff = idx[:, None] - idx[None, :]
    decay = jnp.where(diff >= 0, jnp.exp(log_g[:, None, None] * jnp.maximum(diff, 0.0)), 0.0)
    qdec = jnp.exp(log_g[:, None] * (idx + 1.0))
    kdec = jnp.exp(log_g[:, None] * (L - 1.0 - idx))
    g_chunk = jnp.exp(log_g * L)
    pad = rows - L
    decay = jnp.pad(decay, ((0, 0), (0, pad), (0, pad)))
    lanes = lambda t: jnp.broadcast_to(jnp.pad(t, ((0, 0), (0, pad)))[:, :, None], (RET_HEADS, rows, LANES))
    return decay, lanes(qdec), lanes(kdec), jnp.broadcast_to(g_chunk[:, None, None], (RET_HEADS, 1, LANES))


def _ret_head(q, k, v, state, decay, qdec, kdec, g_chunk):
    scores = lax.dot_general(q, k, NT, preferred_element_type=F32) * decay
    inner = jnp.dot(scores.astype(BF16), v, preferred_element_type=F32)
    q_dec = (q.astype(F32) * qdec).astype(BF16)
    cross = jnp.dot(q_dec, state.astype(BF16), preferred_element_type=F32)
    k_dec = (k.astype(F32) * kdec).astype(BF16)
    new_state = state * g_chunk + lax.dot_general(k_dec, v, TN, preferred_element_type=F32)
    return inner + cross, new_state


def _ret_gate(o, rg, gn):
    mu = jnp.mean(o, axis=-1, keepdims=True)
    oc = o - mu
    var = jnp.mean(oc * oc, axis=-1, keepdims=True)
    return (rg * _sigmoid(rg)) * (oc * lax.rsqrt(var + LN_EPS) * gn)


def _ret_prompt_kernel(q_ref, k_ref, v_ref, rg_ref, gn_ref, decay_ref, qdec_ref, kdec_ref, gc_ref,
                       o_ref, state_ref):
    @pl.when(pl.program_id(0) == 0)
    def _():
        state_ref[...] = jnp.zeros_like(state_ref)

    for b in range(q_ref.shape[0]):
        for h in range(RET_HEADS):
            sl = slice(h * RET_DK, (h + 1) * RET_DK)
            o, new_state = _ret_head(q_ref[b, :, sl], k_ref[b, :, sl], v_ref[b, :, sl], state_ref[b, h],
                                     decay_ref[h], qdec_ref[h], kdec_ref[h], gc_ref[h])
            state_ref[b, h] = new_state
            o_ref[b, :, sl] = _ret_gate(o, rg_ref[b, :, sl], gn_ref[:, sl]).astype(BF16)


def _ret_prompt(rq, rk, rv, rg, gn, batch):
    T = rq.shape[0]
    S = T // batch
    tables = _ret_tables(RET_CHUNK, RET_CHUNK)
    by_batch = lambda t: t.reshape(batch, S, RET_WIDTH)
    tile = pl.BlockSpec((batch, RET_CHUNK, RET_WIDTH), lambda c: (0, c, 0))
    o_r, state = pl.pallas_call(
        _ret_prompt_kernel,
        grid=(S // RET_CHUNK,),
        in_specs=[tile, tile, tile, tile, _const_spec(gn.shape)] + [_const_spec(t.shape) for t in tables],
        out_specs=[tile, pl.BlockSpec((batch, RET_HEADS, RET_DK, RET_DK), lambda c: (0, 0, 0, 0))],
        out_shape=[jax.ShapeDtypeStruct((batch, S, RET_WIDTH), BF16),
                   jax.ShapeDtypeStruct((batch, RET_HEADS, RET_DK, RET_DK), F32)],
        compiler_params=_params("arbitrary"), name="ret_prompt",
    )(by_batch(rq), by_batch(rk), by_batch(rv), by_batch(rg), gn, *tables)
    return o_r.reshape(T, RET_WIDTH), state


def _ret_sample_kernel(q_ref, k_ref, v_ref, rg_ref, gn_ref, s_ref, decay_ref, qdec_ref, kdec_ref, gc_ref,
                       o_ref, snew_ref, *, bt, L):
    for bi in range(bt):
        for h in range(RET_HEADS):
            sl = slice(h * RET_DK, (h + 1) * RET_DK)
            o, new_state = _ret_head(q_ref[bi, :, sl], k_ref[bi, :, sl], v_ref[bi, :, sl], s_ref[0, bi, h],
                                     decay_ref[h], qdec_ref[h], kdec_ref[h], gc_ref[h])
            snew_ref[bi, h] = new_state
            o_ref[bi, :, sl] = _ret_gate(o[:L], rg_ref[bi, :, sl], gn_ref[:, sl])


def _ret_sample(rq, rk, rv, rg, gn, state, layer, *, bt=8):
    db, L = rg.shape[0], rg.shape[1]
    tables = _ret_tables(L, SAMPLE_ROWS)
    b3 = lambda i: (i, 0, 0)
    qkv = pl.BlockSpec((bt, SAMPLE_ROWS, RET_WIDTH), b3)
    st = pl.BlockSpec((bt, RET_HEADS, RET_DK, RET_DK), lambda i: (i, 0, 0, 0))
    st_in = pl.BlockSpec((1, bt, RET_HEADS, RET_DK, RET_DK), lambda i: (layer, i, 0, 0, 0))
    return pl.pallas_call(
        functools.partial(_ret_sample_kernel, bt=bt, L=L),
        grid=(db // bt,),
        in_specs=[qkv, qkv, qkv, pl.BlockSpec((bt, L, RET_WIDTH), b3), _const_spec(gn.shape), st_in]
        + [_const_spec(t.shape) for t in tables],
        out_specs=[pl.BlockSpec((bt, L, RET_WIDTH), b3), st],
        out_shape=[jax.ShapeDtypeStruct((db, L, RET_WIDTH), F32),
                   jax.ShapeDtypeStruct((db, RET_HEADS, RET_DK, RET_DK), F32)],
        compiler_params=_params("arbitrary"), name="ret_sample",
    )(rq, rk, rv, rg, gn, state, *tables)


def _alibi_slopes():
    return 2.0 ** (-8.0 * np.arange(1, MOBA_HEADS + 1) / MOBA_HEADS)


def _moba_prompt_tables():
    slopes = jnp.asarray(_alibi_slopes(), dtype=F32)[:, None, None]
    kk = jnp.arange(MOBA_BLOCK, dtype=F32)[:, None]
    qq = jnp.arange(MOBA_BLOCK, dtype=F32)[None, :]
    dist = (qq - kk)[None]
    past = -(slopes * dist) * LOG2E
    own = jnp.where(dist >= 0, past, -jnp.inf)
    block_step = -(slopes * float(MOBA_BLOCK)) * LOG2E
    return past, own, jnp.broadcast_to(block_step, (MOBA_HEADS, 1, MOBA_BLOCK))


def _for_blocks(n, body):
    def four(i, carry):
        for u in range(4):
            body(4 * i + u)
        return carry

    lax.fori_loop(0, lax.shift_right_logical(n, 2), four, 0)
    done = lax.bitwise_and(n, -4)

    @pl.when(lax.bitwise_and(n, 2) == 2)
    def _():
        body(done)
        body(done + 1)

    @pl.when(lax.bitwise_and(n, 1) == 1)
    def _():
        body(n - 1)


def _moba_prompt_kernel(qT_ref, k_ref, vT_ref, kmean_ref, past_ref, own_ref, step_ref, o_ref,
                        q_sc, s_sc, m_sc, acc_sc, term_sc):
    j = pl.program_id(2)
    nb = kmean_ref.shape[0]
    dh_row = lax.broadcasted_iota(jnp.int32, (PAIR, MOBA_BLOCK), 0)
    blk = lax.broadcasted_iota(jnp.int32, (nb, MOBA_BLOCK), 0)
    own_start = pl.multiple_of(j * MOBA_BLOCK, MOBA_BLOCK)
    pair_cols = lambda h: slice((h // 2) * PAIR, (h // 2 + 1) * PAIR)

    for h in range(GROUP_HEADS):
        qT = qT_ref[pair_cols(h), :]
        keep = (dh_row < MOBA_DH) if h % 2 == 0 else (dh_row >= MOBA_DH)
        qh = jnp.where(keep, qT, jnp.zeros_like(qT))
        q_sc[h] = qh
        gate = jnp.dot(kmean_ref[:, pair_cols(h)].astype(BF16), qh, preferred_element_type=F32)
        gate = jnp.where(blk < j, gate, -jnp.inf)
        sel = jnp.zeros(gate.shape, dtype=jnp.bool_)
        for _ in range(MOBA_TOPK):
            top = jnp.max(gate, axis=0, keepdims=True)
            first = jnp.min(jnp.where(gate == top, blk, nb), axis=0, keepdims=True)
            pick = jnp.logical_and(blk == first, top > -jnp.inf)
            sel = jnp.logical_or(sel, pick)
            gate = jnp.where(pick, -jnp.inf, gate)
        term_sc[h] = jnp.where(sel, (j - blk).astype(F32) * step_ref[h], jnp.where(blk == j, 0.0, -jnp.inf))
        s = jnp.dot(k_ref[pl.ds(own_start, MOBA_BLOCK), pair_cols(h)], qh, preferred_element_type=F32)
        s = s * LOG2E + own_ref[h]
        s_sc[h, j] = s
        m_sc[h] = jnp.max(s, axis=0, keepdims=True)
        acc_sc[h] = jnp.zeros_like(acc_sc[h])

    def scores(jj):
        start = pl.multiple_of(jj * MOBA_BLOCK, MOBA_BLOCK)
        for h in range(GROUP_HEADS):
            s = jnp.dot(k_ref[pl.ds(start, MOBA_BLOCK), pair_cols(h)], q_sc[h], preferred_element_type=F32)
            s = s * LOG2E + past_ref[h]
            s_sc[h, jj] = s
            m_sc[h] = jnp.maximum(m_sc[h], jnp.max(s, axis=0, keepdims=True) + term_sc[h, pl.ds(jj, 1), :])

    _for_blocks(j, scores)

    ones_rows = jnp.ones((DENOM_ROWS, MOBA_BLOCK), BF16)

    def apply_v(jj):
        start = pl.multiple_of(jj * MOBA_BLOCK, MOBA_BLOCK)
        for h in range(GROUP_HEADS):
            pexp = jnp.exp2((s_sc[h, jj] - (m_sc[h] - term_sc[h, pl.ds(jj, 1), :])).astype(BF16))
            vT = jnp.concatenate([vT_ref[h * MOBA_DH:(h + 1) * MOBA_DH, pl.ds(start, MOBA_BLOCK)], ones_rows], axis=0)
            acc_sc[h] = acc_sc[h] + jnp.dot(vT, pexp, preferred_element_type=F32)

    _for_blocks(j + 1, apply_v)

    def head_out(h):
        acc = acc_sc[h]
        return acc[:MOBA_DH] / acc[MOBA_DH:MOBA_DH + 1]

    for p in range(GROUP_HEADS // 2):
        outT = jnp.concatenate([head_out(2 * p), head_out(2 * p + 1)], axis=0)
        o_ref[:, p * PAIR:(p + 1) * PAIR] = outT.T.astype(BF16)


def _moba_prompt(qT16, k16, vT16, kmean, batch):
    T = k16.shape[0]
    S = T // batch
    nb = S // MOBA_BLOCK
    n_groups = MOBA_HEADS // GROUP_HEADS
    gw = GROUP_HEADS * MOBA_DH
    tables = _moba_prompt_tables()
    head_tile = lambda t: pl.BlockSpec((GROUP_HEADS,) + t.shape[1:], lambda b, g, j: (g, 0, 0))
    return pl.pallas_call(
        _moba_prompt_kernel,
        grid=(batch, n_groups, nb),
        in_specs=[pl.BlockSpec((gw, MOBA_BLOCK), lambda b, g, j: (g, b * nb + j)),
                  pl.BlockSpec((S, gw), lambda b, g, j: (b, g)),
                  pl.BlockSpec((gw, S), lambda b, g, j: (g, b)),
                  pl.BlockSpec((nb, gw), lambda b, g, j: (b, g))]
        + [head_tile(t) for t in tables],
        out_specs=pl.BlockSpec((MOBA_BLOCK, gw), lambda b, g, j: (b * nb + j, g)),
        out_shape=jax.ShapeDtypeStruct((T, MOBA_WIDTH), BF16),
        scratch_shapes=[pltpu.VMEM((GROUP_HEADS, PAIR, MOBA_BLOCK), BF16),
                        pltpu.VMEM((GROUP_HEADS, nb, MOBA_BLOCK, MOBA_BLOCK), F32),
                        pltpu.VMEM((GROUP_HEADS, 1, MOBA_BLOCK), F32),
                        pltpu.VMEM((GROUP_HEADS, MOBA_DH + DENOM_ROWS, MOBA_BLOCK), F32),
                        pltpu.VMEM((GROUP_HEADS, nb, MOBA_BLOCK), F32)],
        compiler_params=_params("arbitrary", "arbitrary", "arbitrary"), name="moba_prompt",
    )(qT16, k16, vT16, kmean, *tables)


def _moba_sample_tables(L, n_pages):
    past_len = n_pages * PAGE_SIZE
    slopes = _alibi_slopes()
    row_slope = np.zeros((N_PAIRS, SAMPLE_ROWS), np.float64)
    row_t = np.zeros((SAMPLE_ROWS,), np.float64)
    for p in range(N_PAIRS):
        row_slope[p, :L] = slopes[2 * p]
        row_slope[p, L:2 * L] = slopes[2 * p + 1]
    row_t[:L] = np.arange(L)
    row_t[L:2 * L] = np.arange(L)
    row_slope = jnp.asarray(row_slope, dtype=F32)[:, :, None]
    q_pos = jnp.asarray(past_len + row_t, dtype=F32)[None, :, None]
    key_pos = jnp.arange(past_len, dtype=F32)[None, None, :]
    past = -(row_slope * (q_pos - key_pos))
    new_pos = jnp.arange(SAMPLE_ROWS, dtype=F32)[None, None, :]
    dist_new = jnp.asarray(row_t, dtype=F32)[None, :, None] - new_pos
    valid = jnp.logical_and(dist_new >= 0, new_pos < L)
    new = jnp.where(valid, -(row_slope * dist_new), -jnp.inf)
    return past, new


def _moba_sample_kernel(pt_ref, q_ref, kn_ref, vn_ref, past_ref, new_ref, *refs, L, n_pages):
    k_pages, v_pages, o_ref = refs[:n_pages], refs[n_pages:2 * n_pages], refs[2 * n_pages]
    del pt_ref
    n_blk = n_pages * PAGE_SIZE // MOBA_BLOCK
    per_blk = MOBA_BLOCK // PAGE_SIZE
    row = lax.broadcasted_iota(jnp.int32, (SAMPLE_ROWS, PAIR), 0)
    lane = lax.broadcasted_iota(jnp.int32, (SAMPLE_ROWS, PAIR), 1)
    keep = jnp.logical_or(jnp.logical_and(row < L, lane < MOBA_DH),
                          jnp.logical_and(jnp.logical_and(row >= L, row < 2 * L), lane >= MOBA_DH))
    for p in range(N_PAIRS):
        cols = slice(p * PAIR, (p + 1) * PAIR)
        q = q_ref[0, :, cols]
        qm = jnp.where(keep, q, jnp.zeros_like(q))
        kT = jnp.concatenate([k_pages[s][0, 0, cols, :] for s in range(n_pages)], axis=1).astype(BF16)
        raw = jnp.dot(qm, kT, preferred_element_type=F32)
        blk_lanes = lambda jj: slice(jj * MOBA_BLOCK, (jj + 1) * MOBA_BLOCK)
        gate = [jnp.sum(raw[:, blk_lanes(jj)], axis=-1, keepdims=True) * (1.0 / MOBA_BLOCK) for jj in range(n_blk)]
        sel = []
        for jj in range(n_blk):
            ahead = jnp.zeros(gate[jj].shape, F32)
            for kk in range(n_blk):
                if kk == jj:
                    continue
                beats = (gate[kk] >= gate[jj]) if kk < jj else (gate[kk] > gate[jj])
                ahead = ahead + jnp.where(beats, 1.0, 0.0)
            sel.append(ahead < float(min(MOBA_TOPK, n_blk)))
        logit = jnp.concatenate(
            [jnp.where(sel[jj], raw[:, blk_lanes(jj)] + past_ref[p, :, blk_lanes(jj)], -jnp.inf)
             for jj in range(n_blk)], axis=1)
        kn = kn_ref[0, :, cols]
        s_new = lax.dot_general(qm, kn, NT, preferred_element_type=F32) + new_ref[p]
        m = jnp.maximum(jnp.max(s_new, axis=-1, keepdims=True), jnp.max(logit, axis=-1, keepdims=True))
        p_new = jnp.exp(s_new - m)
        pexp = jnp.exp(logit - m)
        denom = jnp.sum(p_new, axis=-1, keepdims=True) + jnp.sum(pexp, axis=-1, keepdims=True)
        vT = jnp.concatenate([v_pages[s][0, 0, cols, :] for s in range(n_pages)], axis=1).astype(BF16)
        acc = (jnp.dot(p_new.astype(BF16), vn_ref[0, :, cols], preferred_element_type=F32)
               + lax.dot_general(pexp.astype(BF16), vT, NT, preferred_element_type=F32))
        out = acc / denom
        o_ref[0, :, cols] = jnp.where(lane[:L] < MOBA_DH, out[:L], out[L:2 * L])


def _moba_sample(q16, kn16, vn16, cache_kT, cache_vT, page_table, layer, L):
    db, n_pages = page_table.shape
    past, new = _moba_sample_tables(L, n_pages)
    b3 = lambda b, pt: (b, 0, 0)
    row_spec = pl.BlockSpec((1, SAMPLE_ROWS, MOBA_WIDTH), b3)

    def page_spec(s):
        return pl.BlockSpec((1, 1, MOBA_WIDTH, PAGE_SIZE), lambda b, pt: (layer, pt[b, s], 0, 0))

    in_specs = [row_spec, row_spec, row_spec,
                pl.BlockSpec(past.shape, lambda b, pt: (0, 0, 0)), pl.BlockSpec(new.shape, lambda b, pt: (0, 0, 0))]
    in_specs += [page_spec(s) for s in range(n_pages)] * 2
    grid_spec = pltpu.PrefetchScalarGridSpec(
        num_scalar_prefetch=1, grid=(db,), in_specs=in_specs,
        out_specs=pl.BlockSpec((1, L, MOBA_WIDTH), b3))
    return pl.pallas_call(
        functools.partial(_moba_sample_kernel, L=L, n_pages=n_pages),
        grid_spec=grid_spec,
        out_shape=jax.ShapeDtypeStruct((db, L, MOBA_WIDTH), F32),
        compiler_params=_params("arbitrary"), name="moba_sample",
    )(page_table, q16, kn16, vn16, past, new, *([cache_kT] * n_pages), *([cache_vT] * n_pages))


def _route(x, wrh_ref, wrl_ref, br_ref, group=None):
    tm = x.shape[0]
    hi = x.astype(BF16)
    lo = (x - hi.astype(F32)).astype(BF16)
    logit = (jnp.dot(hi, wrh_ref[...], preferred_element_type=F32)
             + (jnp.dot(hi, wrl_ref[...], preferred_element_type=F32)
                + jnp.dot(lo, wrh_ref[...], preferred_element_type=F32))) + br_ref[...]
    lane = lax.broadcasted_iota(jnp.int32, (tm, ROUTER_LANES), 1)
    neg = -jnp.inf
    gl = jnp.where(lane < N_GROUPS, logit, neg)
    gmax = jnp.max(gl, axis=-1, keepdims=True)
    g_sum = jnp.sum(jnp.exp(gl - gmax), axis=-1, keepdims=True)
    if group is None:
        gidx = jnp.min(jnp.where(gl == gmax, lane, ROUTER_LANES), axis=-1, keepdims=True)
        g_w = 1.0 / g_sum
    else:
        gidx = group
        g_w = jnp.exp(jnp.sum(jnp.where(lane == group, logit, 0.0), axis=-1, keepdims=True) - gmax) / g_sum
    first = N_GROUPS + EXPERTS_PER_GROUP * gidx
    in_group = jnp.logical_and(lane >= first, lane < first + EXPERTS_PER_GROUP)
    el = jnp.where(in_group, logit, neg)
    e1 = jnp.max(el, axis=-1, keepdims=True)
    i1 = jnp.min(jnp.where(el == e1, lane, ROUTER_LANES), axis=-1, keepdims=True)
    el2 = jnp.where(lane == i1, neg, el)
    e2 = jnp.max(el2, axis=-1, keepdims=True)
    i2 = jnp.min(jnp.where(el2 == e2, lane, ROUTER_LANES), axis=-1, keepdims=True)
    t = jnp.exp(e2 - e1)
    return gidx, i1, i2, g_w / (1.0 + t), g_w * t / (1.0 + t), lane


def _expert_lanes(weights, first_lane, tm):
    return jnp.concatenate([jnp.broadcast_to(weights[:, first_lane + e:first_lane + e + 1], (tm, EXPERT_HIDDEN))
                            for e in range(EXPERTS_PER_GROUP)], axis=1)


def _group_experts(xb, cexp, weg, weu, wed):
    hg = jnp.dot(xb, weg, preferred_element_type=F32)
    hu = jnp.dot(xb, weu, preferred_element_type=F32)
    hid = (hg * _sigmoid(hg)) * hu
    return jnp.dot((hid * cexp).astype(BF16), wed, preferred_element_type=F32)


def _merge_kernel(x_ref, or_ref, om_ref, wg_ref, bg_ref, wr_ref, wm_ref, wo_ref, g_ref, b_ref, *refs, with_route):
    x = x_ref[...]
    gates = _sigmoid(jnp.dot(x.astype(BF16), wg_ref[...], preferred_element_type=F32) + bg_ref[...])
    br = jnp.dot(or_ref[...], wr_ref[...], preferred_element_type=F32)
    bm = jnp.dot(om_ref[...], wm_ref[...], preferred_element_type=F32)
    merged = gates[:, :D_MODEL] * br + gates[:, D_MODEL:] * bm
    y = jnp.dot(merged.astype(BF16), wo_ref[...], preferred_element_type=F32)
    x1 = _layer_norm_rows(ALPHA * x + y, g_ref[...], b_ref[...])
    if not with_route:
        refs[0][...] = x1
        return
    wrh_ref, wrl_ref, brt_ref, o_ref, group_ref = refs
    o_ref[...] = x1
    gidx = _route(x1, wrh_ref, wrl_ref, brt_ref)[0]
    group_ref[...] = jnp.broadcast_to(gidx, group_ref.shape)


def _merge(x, o_r, o_m, wg16, bg, wr16, wm16, wo16, g, b, router=None, *, tm):
    T = o_r.shape[0]
    row = lambda i: (i, 0)
    consts = [wg16, bg, wr16, wm16, wo16, g, b] + list(router or ())
    out_shape = [jax.ShapeDtypeStruct((T, D_MODEL), F32)]
    out_specs = [pl.BlockSpec((tm, D_MODEL), row)]
    if router:
        out_shape.append(jax.ShapeDtypeStruct((T, ROUTER_LANES), jnp.int32))
        out_specs.append(pl.BlockSpec((tm, ROUTER_LANES), row))
    outs = pl.pallas_call(
        functools.partial(_merge_kernel, with_route=bool(router)),
        grid=(T // tm,),
        in_specs=[pl.BlockSpec((tm, D_MODEL), row), pl.BlockSpec((tm, RET_WIDTH), row),
                  pl.BlockSpec((tm, MOBA_WIDTH), row)] + [_const_spec(c.shape) for c in consts],
        out_specs=out_specs, out_shape=out_shape,
        compiler_params=_params("arbitrary"), name="merge",
    )(x, o_r, o_m, *consts)
    return outs if router else outs[0]


def _moe_kernel(x_ref, wrh_ref, wrl_ref, br_ref, weg_ref, weu_ref, wed_ref, g_ref, b_ref, o_ref):
    x = x_ref[...]
    tm = x.shape[0]
    gidx, i1, i2, w1, w2, lane = _route(x, wrh_ref, wrl_ref, br_ref)
    comb = jnp.where(lane == i1, w1, 0.0) + jnp.where(lane == i2, w2, 0.0)
    xb = x.astype(BF16)
    acc = jnp.zeros((tm, D_MODEL), F32)
    for g in range(N_GROUPS):
        cexp = _expert_lanes(comb, N_GROUPS + g * EXPERTS_PER_GROUP, tm)
        acc = acc + _group_experts(xb, cexp, weg_ref[g], weu_ref[g], wed_ref[g])
    o_ref[...] = _layer_norm_rows(ALPHA * x + acc, g_ref[...], b_ref[...])


def _moe(x, wr_hi, wr_lo, br, weg16, weu16, wed16, g, b, *, tm):
    T = x.shape[0]
    row = lambda i: (i, 0)
    consts = [wr_hi, wr_lo, br, weg16, weu16, wed16, g, b]
    return pl.pallas_call(
        _moe_kernel,
        grid=(T // tm,),
        in_specs=[pl.BlockSpec((tm, D_MODEL), row)] + [_const_spec(c.shape) for c in consts],
        out_specs=pl.BlockSpec((tm, D_MODEL), row),
        out_shape=jax.ShapeDtypeStruct((T, D_MODEL), F32),
        compiler_params=_params("arbitrary"), name="moe",
    )(x, *consts)


def _experts_by_group_kernel(w_ref, o_ref, *, side_by_side):
    for e in range(EXPERTS_PER_GROUP):
        cols = slice(e * EXPERT_HIDDEN, (e + 1) * EXPERT_HIDDEN)
        if side_by_side:
            o_ref[0, :, cols] = w_ref[0, e].astype(BF16)
        else:
            o_ref[0, cols, :] = w_ref[0, e].astype(BF16)


def _experts_by_group(w, layer, *, side_by_side):
    return pl.pallas_call(
        functools.partial(_experts_by_group_kernel, side_by_side=side_by_side),
        grid=(N_GROUPS,),
        in_specs=[pl.BlockSpec((1, EXPERTS_PER_GROUP) + w.shape[2:], lambda g: (layer, g, 0, 0))],
        out_specs=pl.BlockSpec((1, D_MODEL, GROUP_HIDDEN) if side_by_side else (1, GROUP_HIDDEN, D_MODEL),
                               lambda g: (g, 0, 0)),
        out_shape=jax.ShapeDtypeStruct((N_GROUPS, D_MODEL, GROUP_HIDDEN) if side_by_side
                                       else (N_GROUPS, GROUP_HIDDEN, D_MODEL), BF16),
        compiler_params=_params("arbitrary"), name="experts_by_group",
    )(w)


def _group_plan(gid, tm):
    T = gid.shape[0]
    groups = jnp.arange(N_GROUPS, dtype=jnp.int32)
    member = (gid[:, None] == groups[None, :]).astype(jnp.int32)
    running = jnp.cumsum(member, axis=0)
    counts = running[-1]
    rank = jnp.sum(running * member, axis=1) - 1
    padded = ((counts + tm - 1) // tm) * tm
    ends = jnp.cumsum(padded)
    starts = ends - padded
    pos = jnp.sum(starts[None, :] * member, axis=1) + rank
    n_tiles = T // tm + N_GROUPS
    rows = jnp.arange(T, dtype=jnp.int32)
    pad_dst = T + jnp.arange(tm, dtype=jnp.int32)
    src = jnp.zeros((n_tiles * tm,), jnp.int32).at[pos].set(rows, unique_indices=True)
    tile_start = jnp.arange(n_tiles, dtype=jnp.int32) * tm
    tile_group = jnp.minimum(jnp.sum((tile_start[:, None] >= ends[None, :]).astype(jnp.int32), axis=1), N_GROUPS - 1)
    in_tile = (tile_group[:, None] == groups[None, :]).astype(jnp.int32)
    fill = jnp.sum(in_tile * (starts + counts)[None, :], axis=1)[:, None] - tile_start[:, None]
    real = (jnp.arange(tm, dtype=jnp.int32)[None, :] < fill).reshape(n_tiles * tm)
    dst = jnp.where(real, src, jnp.tile(pad_dst, n_tiles))
    dst = jnp.concatenate([pad_dst, dst]).reshape(n_tiles + 1, 1, tm)
    return src.reshape(n_tiles, 1, tm), dst, tile_group


def _moe_grouped_kernel(tg_ref, src_now_ref, src_next_ref, dst_prev_ref, dst_now_ref,
                        x_hbm, wrh_ref, wrl_ref, br_ref, weg_ref, weu_ref, wed_ref, g_ref, b_ref, out_hbm,
                        xbuf, obuf, gsem, ssem, *, tm):
    i = pl.program_id(0)
    last = pl.num_programs(0) - 1
    slot = lax.rem(i, 2)
    other = 1 - slot

    def fetch_rows(idx_ref, s, rows=range(tm)):
        for r in rows:
            t = idx_ref[0, 0, r]
            pltpu.make_async_copy(x_hbm.at[pl.ds(t, 1)], xbuf.at[s, pl.ds(r, 1)], gsem.at[s]).start(priority=r % 2)

    def fetch_wait(s):
        pltpu.make_async_copy(x_hbm.at[pl.ds(0, tm)], xbuf.at[s], gsem.at[s]).wait()

    def write_rows(idx_ref, s, rows=range(tm)):
        for r in rows:
            t = idx_ref[0, 0, r]
            pltpu.make_async_copy(obuf.at[s, pl.ds(r, 1)], out_hbm.at[pl.ds(t, 1)], ssem.at[s]).start(priority=r % 2)

    def write_wait(s):
        pltpu.make_async_copy(obuf.at[s], out_hbm.at[pl.ds(0, tm)], ssem.at[s]).wait()

    @pl.when(i == 0)
    def _():
        obuf[...] = jnp.zeros_like(obuf)
        fetch_rows(src_now_ref, 0)
        fetch_wait(0)

    fetch_rows(src_next_ref, other)
    write_rows(dst_prev_ref, other)
    x = xbuf[slot]
    group = tg_ref[i]
    _, i1, i2, w1, w2, lane = _route(x, wrh_ref, wrl_ref, br_ref, group=group)
    first = N_GROUPS + EXPERTS_PER_GROUP * group
    cexp = _expert_lanes(jnp.where(lane == i1 - first, w1, 0.0) + jnp.where(lane == i2 - first, w2, 0.0), 0, tm)
    acc = _group_experts(x.astype(BF16), cexp, weg_ref[0], weu_ref[0], wed_ref[0])
    obuf[slot] = _layer_norm_rows(ALPHA * x + acc, g_ref[...], b_ref[...])
    fetch_wait(other)
    write_wait(other)

    @pl.when(i == last)
    def _():
        write_rows(dst_now_ref, slot)
        write_wait(slot)


def _moe_grouped(x, T, src, dst, tile_group, wr_hi, wr_lo, br, weg16, weu16, wed16, g, b, *, tm):
    n_tiles = src.shape[0]
    smem_tile = lambda fn: pl.BlockSpec((1, 1, tm), fn, memory_space=pltpu.SMEM)
    by_group = lambda i, tg: (tg[i], 0, 0)
    w_spec = pl.BlockSpec((1, D_MODEL, GROUP_HIDDEN), by_group)
    fixed = lambda t: pl.BlockSpec(t.shape, lambda i, tg: (0, 0))
    grid_spec = pltpu.PrefetchScalarGridSpec(
        num_scalar_prefetch=1, grid=(n_tiles,),
        in_specs=[smem_tile(lambda i, tg: (i, 0, 0)),
                  smem_tile(lambda i, tg: (jnp.minimum(i + 1, n_tiles - 1), 0, 0)),
                  smem_tile(lambda i, tg: (i, 0, 0)), smem_tile(lambda i, tg: (i + 1, 0, 0)),
                  pl.BlockSpec(memory_space=pl.ANY), fixed(wr_hi), fixed(wr_lo), fixed(br),
                  w_spec, w_spec, pl.BlockSpec((1, GROUP_HIDDEN, D_MODEL), by_group), fixed(g), fixed(b)],
        out_specs=pl.BlockSpec(memory_space=pl.ANY),
        scratch_shapes=[pltpu.VMEM((2, tm, D_MODEL), F32), pltpu.VMEM((2, tm, D_MODEL), F32),
                        pltpu.SemaphoreType.DMA((2,)), pltpu.SemaphoreType.DMA((2,))])
    return pl.pallas_call(
        functools.partial(_moe_grouped_kernel, tm=tm),
        grid_spec=grid_spec,
        out_shape=jax.ShapeDtypeStruct((T + tm, D_MODEL), F32),
        compiler_params=_params("arbitrary"), name="moe_grouped",
    )(tile_group, src, src, dst, dst, x, wr_hi, wr_lo, br, weg16, weu16, wed16, g, b)


def _layer_weights(l, w_in, b_merge, gn_g, w_br_ret, w_br_moba, w_out, ln1_g, ln1_b, w_router_group,
                   b_router_group, w_router_expert, b_router_expert, w_exp_gate, w_exp_up, w_exp_down,
                   ln2_g, ln2_b):
    w = w_in[l]
    q0 = 4 * RET_WIDTH
    n_exp = N_GROUPS * EXPERTS_PER_GROUP
    w_r = jnp.concatenate(
        [w_router_group[l], w_router_expert[l].transpose(1, 0, 2).reshape(D_MODEL, n_exp),
         jnp.zeros((D_MODEL, ROUTER_LANES - N_GROUPS - n_exp), F32)], axis=1)
    w_r_hi = w_r.astype(BF16)
    w_r_lo = (w_r - w_r_hi.astype(F32)).astype(BF16)
    b_r = jnp.concatenate([b_router_group[l], b_router_expert[l].reshape(n_exp),
                           jnp.zeros((ROUTER_LANES - N_GROUPS - n_exp,), F32)])[None, :]
    return dict(
        w_proj=w[:, :N_PROJ].astype(BF16),
        w_ret=w[:, :q0].astype(BF16),
        w_mobaT=w[:, q0:N_PROJ].T.reshape(3, MOBA_WIDTH, D_MODEL).astype(BF16),
        w_gate=w[:, N_PROJ:].astype(BF16), b_gate=b_merge[l][None, :],
        gn=gn_g[l][None, :],
        w_br_ret=w_br_ret[l].astype(BF16), w_br_moba=w_br_moba[l].astype(BF16), w_out=w_out[l].astype(BF16),
        ln1_g=ln1_g[l][None, :], ln1_b=ln1_b[l][None, :],
        w_r_hi=w_r_hi, w_r_lo=w_r_lo, b_r=b_r,
        w_eg=_experts_by_group(w_exp_gate, l, side_by_side=True),
        w_eu=_experts_by_group(w_exp_up, l, side_by_side=True),
        w_ed=_experts_by_group(w_exp_down, l, side_by_side=False),
        ln2_g=ln2_g[l][None, :], ln2_b=ln2_b[l][None, :])


def _pad_rows(t, rows):
    return jnp.pad(t, ((0, 0), (0, rows - t.shape[1]), (0, 0)))


def kernel(x_prompt, x_sample, cache_k, cache_v, state_ret, page_table, w_in, b_merge, gn_g, w_br_ret, w_br_moba, w_out, ln1_g, ln1_b, w_router_group, b_router_group, w_router_expert, b_router_expert, w_exp_gate, w_exp_up, w_exp_down, ln2_g, ln2_b):
    B, S, _ = x_prompt.shape
    DB, L, _ = x_sample.shape
    Tp, Ts = B * S, DB * L
    depth = w_in.shape[0]
    n_pool = cache_k.shape[1]
    page_major = lambda c: c.transpose(0, 1, 3, 4, 2).reshape(depth, n_pool, MOBA_WIDTH, PAGE_SIZE)
    cache_kT, cache_vT = page_major(cache_k), page_major(cache_v)
    xp = x_prompt.reshape(Tp, D_MODEL)
    xs = x_sample.reshape(Ts, D_MODEL)
    tm_p = min(512, Tp)
    tm_s = min(256, Ts)
    tm_moe = 256
    outs = [[] for _ in range(6)]
    for l in range(depth):
        W = _layer_weights(l, w_in, b_merge, gn_g, w_br_ret, w_br_moba, w_out, ln1_g, ln1_b, w_router_group,
                           b_router_group, w_router_expert, b_router_expert, w_exp_gate, w_exp_up, w_exp_down,
                           ln2_g, ln2_b)
        (rq, rk, rv, rg, kT_p, vT_p, k16, qT16, vT16, kmean) = _in_proj(
            xp, W["w_ret"], W["w_mobaT"], rows=Tp, batch=B, tm=tm_p)
        o_r, s_p = _ret_prompt(rq, rk, rv, rg, W["gn"], B)
        o_m = _moba_prompt(qT16, k16, vT16, kmean, B)
        router = (W["w_r_hi"], W["w_r_lo"], W["b_r"])
        x1, group = _merge(xp, o_r, o_m, W["w_gate"], W["b_gate"], W["w_br_ret"], W["w_br_moba"], W["w_out"],
                           W["ln1_g"], W["ln1_b"], router=router, tm=tm_p)
        plan = _group_plan(group[:, 0], tm_moe)
        xp = _moe_grouped(x1, Tp, *plan, *router, W["w_eg"], W["w_eu"], W["w_ed"], W["ln2_g"], W["ln2_b"], tm=tm_moe)
        (rq, rk, rv, rg, mq, k_s, v_s) = _in_proj(xs, W["w_proj"], tm=tm_s)
        r3 = lambda t: t.reshape(DB, L, t.shape[-1])
        o_r, s_s = _ret_sample(_pad_rows(r3(rq), SAMPLE_ROWS), _pad_rows(r3(rk), SAMPLE_ROWS),
                               _pad_rows(r3(rv), SAMPLE_ROWS), r3(rg), W["gn"], state_ret, l)
        mq3 = r3(mq)
        q16 = _pad_rows(jnp.concatenate([mq3, mq3], axis=1), SAMPLE_ROWS)
        o_m = _moba_sample(q16, _pad_rows(r3(k_s).astype(BF16), SAMPLE_ROWS),
                           _pad_rows(r3(v_s).astype(BF16), SAMPLE_ROWS),
                           cache_kT, cache_vT, page_table, l, L)
        x1 = _merge(xs, o_r.reshape(Ts, RET_WIDTH).astype(BF16), o_m.reshape(Ts, MOBA_WIDTH).astype(BF16),
                    W["w_gate"], W["b_gate"], W["w_br_ret"], W["w_br_moba"], W["w_out"],
                    W["ln1_g"], W["ln1_b"], tm=tm_s)
        xs = _moe(x1, W["w_r_hi"], W["w_r_lo"], W["b_r"], W["w_eg"], W["w_eu"], W["w_ed"],
                  W["ln2_g"], W["ln2_b"], tm=tm_s)
        for lst, val in zip(outs, (kT_p, vT_p, s_p,
                                   k_s.reshape(DB, L, MOBA_HEADS, MOBA_DH), v_s.reshape(DB, L, MOBA_HEADS, MOBA_DH), s_s)):
            lst.append(val)
    kTp, vTp, sp, ksm, vsm, ssm = (jnp.stack(o) for o in outs)
    token_major = lambda t: t.reshape(depth, B, MOBA_HEADS, MOBA_DH, S).transpose(0, 1, 4, 2, 3)
    return (xp[:Tp].reshape(B, S, D_MODEL), xs.reshape(DB, L, D_MODEL), token_major(kTp), token_major(vTp), sp,
            ksm, vsm, ssm)
```

```python
import functools

import numpy as np
import jax
import jax.numpy as jnp
from jax import lax
from jax.experimental import pallas as pl
from jax.experimental.pallas import tpu as pltpu

F32 = jnp.float32
BF16 = jnp.bfloat16

D_MODEL = 1024
RET_HEADS = 4
RET_DK = 128
RET_WIDTH = RET_HEADS * RET_DK
RET_CHUNK = 128
MOBA_HEADS = 8
MOBA_DH = 64
MOBA_WIDTH = MOBA_HEADS * MOBA_DH
MOBA_BLOCK = 256
MOBA_TOPK = 3
PAGE_SIZE = 128
N_GROUPS = 4
EXPERTS_PER_GROUP = 8
EXPERT_HIDDEN = 128
GROUP_HIDDEN = EXPERTS_PER_GROUP * EXPERT_HIDDEN
DEPTH = 2
ALPHA = (2 * DEPTH) ** 0.25
LN_EPS = 1e-5
N_PROJ = 4 * RET_WIDTH + 3 * MOBA_WIDTH

LANES = 128
ROW_CHUNKS = D_MODEL // LANES
PAIR = 2 * MOBA_DH
N_PAIRS = MOBA_HEADS // 2
GROUP_HEADS = 4
DENOM_ROWS = 16
LOG2E = 1.4426950408889634
SAMPLE_ROWS = 16
ROUTER_LANES = 128
VMEM_LIMIT = 56 * 1024 * 1024

NT = (((1,), (1,)), ((), ()))
TN = (((0,), (0,)), ((), ()))


def _params(*sem):
    return pltpu.CompilerParams(dimension_semantics=sem, vmem_limit_bytes=VMEM_LIMIT)


def _const_spec(shape):
    nd = len(shape)
    return pl.BlockSpec(shape, lambda *_: (0,) * nd, pipeline_mode=pl.Buffered(1))


def _layer_norm_rows(z, g, b):
    mu = jnp.mean(z, axis=-1, keepdims=True)
    zc = z - mu
    var = jnp.mean(zc * zc, axis=-1, keepdims=True)
    return zc * lax.rsqrt(var + LN_EPS) * g + b


def _sigmoid(x):
    return 1.0 / (1.0 + jnp.exp(-x))


def _in_proj_kernel(x_ref, w_ref, *refs, transposed, tm):
    xb = x_ref[...].astype(BF16)

    def proj(c):
        return jnp.dot(xb, w_ref[:, c * RET_WIDTH:(c + 1) * RET_WIDTH], preferred_element_type=F32)

    if transposed:
        wT_ref, rq_ref, rk_ref, rv_ref, rg_ref, kT_ref, vT_ref, k16_ref, qT16_ref, vT16_ref, kmean_ref = refs
        projT = lambda c: lax.dot_general(wT_ref[c], xb, NT, preferred_element_type=F32)
        qT16_ref[...] = (projT(0) * MOBA_DH ** -0.5).astype(BF16)
        kT = projT(1)
        kT_ref[0] = kT
        k = kT.T
        k16_ref[...] = k.astype(BF16)
        nb = tm // MOBA_BLOCK
        kmean_ref[0] = jnp.mean(k.reshape(nb, MOBA_BLOCK, MOBA_WIDTH), axis=1)
        vT = projT(2)
        vT_ref[0] = vT
        vT16_ref[...] = vT.astype(BF16)
    else:
        rq_ref, rk_ref, rv_ref, rg_ref, mq_ref, k_ref, v_ref = refs
        mq_ref[...] = (proj(4) * MOBA_DH ** -0.5).astype(BF16)
        k_ref[...] = proj(5)
        v_ref[...] = proj(6)
    rq_ref[...] = proj(0).astype(BF16)
    rk_ref[...] = (proj(1) * RET_DK ** -0.5).astype(BF16)
    rv_ref[...] = proj(2).astype(BF16)
    rg_ref[...] = proj(3)


def _in_proj(x, w16, wT16=None, *, rows=None, batch=1, tm):
    T = rows or x.shape[0]
    transposed = wT16 is not None
    row = lambda i: (i, 0)
    tile = lambda: pl.BlockSpec((tm, RET_WIDTH), row)
    in_specs = [pl.BlockSpec((tm, D_MODEL), row), _const_spec(w16.shape)]
    args = [x, w16]
    out_shape = [jax.ShapeDtypeStruct((T, RET_WIDTH), BF16)] * 3 + [jax.ShapeDtypeStruct((T, RET_WIDTH), F32)]
    out_specs = [tile() for _ in range(4)]
    if transposed:
        S = T // batch
        per_b = S // tm
        in_specs.append(_const_spec(wT16.shape))
        args.append(wT16)
        col = lambda i: (0, i)
        by_batch = lambda i: (i // per_b, 0, i % per_b)
        out_shape += [jax.ShapeDtypeStruct((batch, MOBA_WIDTH, S), F32)] * 2 + [
            jax.ShapeDtypeStruct((T, MOBA_WIDTH), BF16),
            jax.ShapeDtypeStruct((MOBA_WIDTH, T), BF16), jax.ShapeDtypeStruct((MOBA_WIDTH, T), BF16),
            jax.ShapeDtypeStruct((T // tm, tm // MOBA_BLOCK, MOBA_WIDTH), F32)]
        out_specs += [pl.BlockSpec((1, MOBA_WIDTH, tm), by_batch), pl.BlockSpec((1, MOBA_WIDTH, tm), by_batch),
                      tile(), pl.BlockSpec((MOBA_WIDTH, tm), col), pl.BlockSpec((MOBA_WIDTH, tm), col),
                      pl.BlockSpec((1, tm // MOBA_BLOCK, MOBA_WIDTH), lambda i: (i, 0, 0))]
    else:
        out_shape += [jax.ShapeDtypeStruct((T, MOBA_WIDTH), BF16)] + [jax.ShapeDtypeStruct((T, MOBA_WIDTH), F32)] * 2
        out_specs += [tile() for _ in range(3)]
    outs = pl.pallas_call(
        functools.partial(_in_proj_kernel, transposed=transposed, tm=tm),
        grid=(T // tm,), in_specs=in_specs, out_specs=out_specs, out_shape=out_shape,
        compiler_params=_params("arbitrary"), name="in_proj",
    )(*args)
    if transposed:
        outs = list(outs)
        outs[-1] = outs[-1].reshape(T // MOBA_BLOCK, MOBA_WIDTH)
    return outs


def _ret_tables(L, rows):
    log_g = jnp.log(jnp.asarray(1.0 - 2.0 ** (-5.0 - np.arange(RET_HEADS)), dtype=F32))
    idx = jnp.arange(L, dtype=F32)
    diff = idx[:, None] - idx[None, :]
    decay = jnp.where(diff >= 0, jnp.exp(log_g[:, None, None] * jnp.maximum(diff, 0.0)), 0.0)
    qdec = jnp.exp(log_g[:, None] * (idx + 1.0))
    kdec = jnp.exp(log_g[:, None] * (L - 1.0 - idx))
    g_chunk = jnp.exp(log_g * L)
    pad = rows - L
    decay = jnp.pad(decay, ((0, 0), (0, pad), (0, pad)))
    lanes = lambda t: jnp.broadcast_to(jnp.pad(t, ((0, 0), (0, pad)))[:, :, None], (RET_HEADS, rows, LANES))
    return decay, lanes(qdec), lanes(kdec), jnp.broadcast_to(g_chunk[:, None, None], (RET_HEADS, 1, LANES))


def _ret_head(q, k, v, state, decay, qdec, kdec, g_chunk):
    scores = lax.dot_general(q, k, NT, preferred_element_type=F32) * decay
    inner = jnp.dot(scores.astype(BF16), v, preferred_element_type=F32)
    q_dec = (q.astype(F32) * qdec).astype(BF16)
    cross = jnp.dot(q_dec, state.astype(BF16), preferred_element_type=F32)
    k_dec = (k.astype(F32) * kdec).astype(BF16)
    new_state = state * g_chunk + lax.dot_general(k_dec, v, TN, preferred_element_type=F32)
    return inner + cross, new_state


def _ret_gate(o, rg, gn):
    mu = jnp.mean(o, axis=-1, keepdims=True)
    oc = o - mu
    var = jnp.mean(oc * oc, axis=-1, keepdims=True)
    return (rg * _sigmoid(rg)) * (oc * lax.rsqrt(var + LN_EPS) * gn)


def _ret_prompt_kernel(q_ref, k_ref, v_ref, rg_ref, gn_ref, decay_ref, qdec_ref, kdec_ref, gc_ref,
                       o_ref, state_ref):
    @pl.when(pl.program_id(0) == 0)
    def _():
        state_ref[...] = jnp.zeros_like(state_ref)

    for b in range(q_ref.shape[0]):
        for h in range(RET_HEADS):
            sl = slice(h * RET_DK, (h + 1) * RET_DK)
            o, new_state = _ret_head(q_ref[b, :, sl], k_ref[b, :, sl], v_ref[b, :, sl], state_ref[b, h],
                                     decay_ref[h], qdec_ref[h], kdec_ref[h], gc_ref[h])
            state_ref[b, h] = new_state
            o_ref[b, :, sl] = _ret_gate(o, rg_ref[b, :, sl], gn_ref[:, sl]).astype(BF16)


def _ret_prompt(rq, rk, rv, rg, gn, batch):
    T = rq.shape[0]
    S = T // batch
    tables = _ret_tables(RET_CHUNK, RET_CHUNK)
    by_batch = lambda t: t.reshape(batch, S, RET_WIDTH)
    tile = pl.BlockSpec((batch, RET_CHUNK, RET_WIDTH), lambda c: (0, c, 0))
    o_r, state = pl.pallas_call(
        _ret_prompt_kernel,
        grid=(S // RET_CHUNK,),
        in_specs=[tile, tile, tile, tile, _const_spec(gn.shape)] + [_const_spec(t.shape) for t in tables],
        out_specs=[tile, pl.BlockSpec((batch, RET_HEADS, RET_DK, RET_DK), lambda c: (0, 0, 0, 0))],
        out_shape=[jax.ShapeDtypeStruct((batch, S, RET_WIDTH), BF16),
                   jax.ShapeDtypeStruct((batch, RET_HEADS, RET_DK, RET_DK), F32)],
        compiler_params=_params("arbitrary"), name="ret_prompt",
    )(by_batch(rq), by_batch(rk), by_batch(rv), by_batch(rg), gn, *tables)
    return o_r.reshape(T, RET_WIDTH), state


def _ret_sample_kernel(q_ref, k_ref, v_ref, rg_ref, gn_ref, s_ref, decay_ref, qdec_ref, kdec_ref, gc_ref,
                       o_ref, snew_ref, *, bt, L):
    for bi in range(bt):
        for h in range(RET_HEADS):
            sl = slice(h * RET_DK, (h + 1) * RET_DK)
            o, new_state = _ret_head(q_ref[bi, :, sl], k_ref[bi, :, sl], v_ref[bi, :, sl], s_ref[0, bi, h],
                                     decay_ref[h], qdec_ref[h], kdec_ref[h], gc_ref[h])
            snew_ref[bi, h] = new_state
            o_ref[bi, :, sl] = _ret_gate(o[:L], rg_ref[bi, :, sl], gn_ref[:, sl])


def _ret_sample(rq, rk, rv, rg, gn, state, layer, *, bt=8):
    db, L = rg.shape[0], rg.shape[1]
    tables = _ret_tables(L, SAMPLE_ROWS)
    b3 = lambda i: (i, 0, 0)
    qkv = pl.BlockSpec((bt, SAMPLE_ROWS, RET_WIDTH), b3)
    st = pl.BlockSpec((bt, RET_HEADS, RET_DK, RET_DK), lambda i: (i, 0, 0, 0))
    st_in = pl.BlockSpec((1, bt, RET_HEADS, RET_DK, RET_DK), lambda i: (layer, i, 0, 0, 0))
    return pl.pallas_call(
        functools.partial(_ret_sample_kernel, bt=bt, L=L),
        grid=(db // bt,),
        in_specs=[qkv, qkv, qkv, pl.BlockSpec((bt, L, RET_WIDTH), b3), _const_spec(gn.shape), st_in]
        + [_const_spec(t.shape) for t in tables],
        out_specs=[pl.BlockSpec((bt, L, RET_WIDTH), b3), st],
        out_shape=[jax.ShapeDtypeStruct((db, L, RET_WIDTH), F32),
                   jax.ShapeDtypeStruct((db, RET_HEADS, RET_DK, RET_DK), F32)],
        compiler_params=_params("arbitrary"), name="ret_sample",
    )(rq, rk, rv, rg, gn, state, *tables)


def _alibi_slopes():
    return 2.0 ** (-8.0 * np.arange(1, MOBA_HEADS + 1) / MOBA_HEADS)


def _moba_prompt_tables():
    slopes = jnp.asarray(_alibi_slopes(), dtype=F32)[:, None, None]
    kk = jnp.arange(MOBA_BLOCK, dtype=F32)[:, None]
    qq = jnp.arange(MOBA_BLOCK, dtype=F32)[None, :]
    dist = (qq - kk)[None]
    past = -(slopes * dist) * LOG2E
    own = jnp.where(dist >= 0, past, -jnp.inf)
    block_step = -(slopes * float(MOBA_BLOCK)) * LOG2E
    return past, own, jnp.broadcast_to(block_step, (MOBA_HEADS, 1, MOBA_BLOCK))


def _for_blocks(n, body):
    def four(i, carry):
        for u in range(4):
            body(4 * i + u)
        return carry

    lax.fori_loop(0, lax.shift_right_logical(n, 2), four, 0)
    done = lax.bitwise_and(n, -4)

    @pl.when(lax.bitwise_and(n, 2) == 2)
    def _():
        body(done)
        body(done + 1)

    @pl.when(lax.bitwise_and(n, 1) == 1)
    def _():
        body(n - 1)


def _moba_prompt_kernel(qT_ref, k_ref, vT_ref, kmean_ref, past_ref, own_ref, step_ref, o_ref,
                        q_sc, s_sc, m_sc, acc_sc, term_sc):
    j = pl.program_id(2)
    nb = kmean_ref.shape[0]
    dh_row = lax.broadcasted_iota(jnp.int32, (PAIR, MOBA_BLOCK), 0)
    blk = lax.broadcasted_iota(jnp.int32, (nb, MOBA_BLOCK), 0)
    own_start = pl.multiple_of(j * MOBA_BLOCK, MOBA_BLOCK)
    pair_cols = lambda h: slice((h // 2) * PAIR, (h // 2 + 1) * PAIR)

    for h in range(GROUP_HEADS):
        qT = qT_ref[pair_cols(h), :]
        keep = (dh_row < MOBA_DH) if h % 2 == 0 else (dh_row >= MOBA_DH)
        qh = jnp.where(keep, qT, jnp.zeros_like(qT))
        q_sc[h] = qh
        gate = jnp.dot(kmean_ref[:, pair_cols(h)].astype(BF16), qh, preferred_element_type=F32)
        gate = jnp.where(blk < j, gate, -jnp.inf)
        sel = jnp.zeros(gate.shape, dtype=jnp.bool_)
        for _ in range(MOBA_TOPK):
            top = jnp.max(gate, axis=0, keepdims=True)
            first = jnp.min(jnp.where(gate == top, blk, nb), axis=0, keepdims=True)
            pick = jnp.logical_and(blk == first, top > -jnp.inf)
            sel = jnp.logical_or(sel, pick)
            gate = jnp.where(pick, -jnp.inf, gate)
        term_sc[h] = jnp.where(sel, (j - blk).astype(F32) * step_ref[h], jnp.where(blk == j, 0.0, -jnp.inf))
        s = jnp.dot(k_ref[pl.ds(own_start, MOBA_BLOCK), pair_cols(h)], qh, preferred_element_type=F32)
        s = s * LOG2E + own_ref[h]
        s_sc[h, j] = s
        m_sc[h] = jnp.max(s, axis=0, keepdims=True)
        acc_sc[h] = jnp.zeros_like(acc_sc[h])

    def scores(jj):
        start = pl.multiple_of(jj * MOBA_BLOCK, MOBA_BLOCK)
        for h in range(GROUP_HEADS):
            s = jnp.dot(k_ref[pl.ds(start, MOBA_BLOCK), pair_cols(h)], q_sc[h], preferred_element_type=F32)
            s = s * LOG2E + past_ref[h]
            s_sc[h, jj] = s
            m_sc[h] = jnp.maximum(m_sc[h], jnp.max(s, axis=0, keepdims=True) + term_sc[h, pl.ds(jj, 1), :])

    _for_blocks(j, scores)

    ones_rows = jnp.ones((DENOM_ROWS, MOBA_BLOCK), BF16)

    def apply_v(jj):
        start = pl.multiple_of(jj * MOBA_BLOCK, MOBA_BLOCK)
        for h in range(GROUP_HEADS):
            pexp = jnp.exp2((s_sc[h, jj] - (m_sc[h] - term_sc[h, pl.ds(jj, 1), :])).astype(BF16))
            vT = jnp.concatenate([vT_ref[h * MOBA_DH:(h + 1) * MOBA_DH, pl.ds(start, MOBA_BLOCK)], ones_rows], axis=0)
            acc_sc[h] = acc_sc[h] + jnp.dot(vT, pexp, preferred_element_type=F32)

    _for_blocks(j + 1, apply_v)

    def head_out(h):
        acc = acc_sc[h]
        return acc[:MOBA_DH] / acc[MOBA_DH:MOBA_DH + 1]

    for p in range(GROUP_HEADS // 2):
        outT = jnp.concatenate([head_out(2 * p), head_out(2 * p + 1)], axis=0)
        o_ref[:, p * PAIR:(p + 1) * PAIR] = outT.T.astype(BF16)


def _moba_prompt(qT16, k16, vT16, kmean, batch):
    T = k16.shape[0]
    S = T // batch
    nb = S // MOBA_BLOCK
    n_groups = MOBA_HEADS // GROUP_HEADS
    gw = GROUP_HEADS * MOBA_DH
    tables = _moba_prompt_tables()
    head_tile = lambda t: pl.BlockSpec((GROUP_HEADS,) + t.shape[1:], lambda b, g, j: (g, 0, 0))
    return pl.pallas_call(
        _moba_prompt_kernel,
        grid=(batch, n_groups, nb),
        in_specs=[pl.BlockSpec((gw, MOBA_BLOCK), lambda b, g, j: (g, b * nb + j)),
                  pl.BlockSpec((S, gw), lambda b, g, j: (b, g)),
                  pl.BlockSpec((gw, S), lambda b, g, j: (g, b)),
                  pl.BlockSpec((nb, gw), lambda b, g, j: (b, g))]
        + [head_tile(t) for t in tables],
        out_specs=pl.BlockSpec((MOBA_BLOCK, gw), lambda b, g, j: (b * nb + j, g)),
        out_shape=jax.ShapeDtypeStruct((T, MOBA_WIDTH), BF16),
        scratch_shapes=[pltpu.VMEM((GROUP_HEADS, PAIR, MOBA_BLOCK), BF16),
                        pltpu.VMEM((GROUP_HEADS, nb, MOBA_BLOCK, MOBA_BLOCK), F32),
                        pltpu.VMEM((GROUP_HEADS, 1, MOBA_BLOCK), F32),
                        pltpu.VMEM((GROUP_HEADS, MOBA_DH + DENOM_ROWS, MOBA_BLOCK), F32),
                        pltpu.VMEM((GROUP_HEADS, nb, MOBA_BLOCK), F32)],
        compiler_params=_params("arbitrary", "arbitrary", "arbitrary"), name="moba_prompt",
    )(qT16, k16, vT16, kmean, *tables)


def _moba_sample_tables(L, n_pages):
    past_len = n_pages * PAGE_SIZE
    slopes = _alibi_slopes()
    row_slope = np.zeros((N_PAIRS, SAMPLE_ROWS), np.float64)
    row_t = np.zeros((SAMPLE_ROWS,), np.float64)
    for p in range(N_PAIRS):
        row_slope[p, :L] = slopes[2 * p]
        row_slope[p, L:2 * L] = slopes[2 * p + 1]
    row_t[:L] = np.arange(L)
    row_t[L:2 * L] = np.arange(L)
    row_slope = jnp.asarray(row_slope, dtype=F32)[:, :, None]
    q_pos = jnp.asarray(past_len + row_t, dtype=F32)[None, :, None]
    key_pos = jnp.arange(past_len, dtype=F32)[None, None, :]
    past = -(row_slope * (q_pos - key_pos))
    new_pos = jnp.arange(SAMPLE_ROWS, dtype=F32)[None, None, :]
    dist_new = jnp.asarray(row_t, dtype=F32)[None, :, None] - new_pos
    valid = jnp.logical_and(dist_new >= 0, new_pos < L)
    new = jnp.where(valid, -(row_slope * dist_new), -jnp.inf)
    return past, new


def _moba_sample_kernel(pt_ref, q_ref, kn_ref, vn_ref, past_ref, new_ref, *refs, L, n_pages):
    k_pages, v_pages, o_ref = refs[:n_pages], refs[n_pages:2 * n_pages], refs[2 * n_pages]
    del pt_ref
    n_blk = n_pages * PAGE_SIZE // MOBA_BLOCK
    per_blk = MOBA_BLOCK // PAGE_SIZE
    row = lax.broadcasted_iota(jnp.int32, (SAMPLE_ROWS, PAIR), 0)
    lane = lax.broadcasted_iota(jnp.int32, (SAMPLE_ROWS, PAIR), 1)
    keep = jnp.logical_or(jnp.logical_and(row < L, lane < MOBA_DH),
                          jnp.logical_and(jnp.logical_and(row >= L, row < 2 * L), lane >= MOBA_DH))
    for p in range(N_PAIRS):
        cols = slice(p * PAIR, (p + 1) * PAIR)
        q = q_ref[0, :, cols]
        qm = jnp.where(keep, q, jnp.zeros_like(q))
        kT = jnp.concatenate([k_pages[s][0, 0, cols, :] for s in range(n_pages)], axis=1).astype(BF16)
        raw = jnp.dot(qm, kT, preferred_element_type=F32)
        blk_lanes = lambda jj: slice(jj * MOBA_BLOCK, (jj + 1) * MOBA_BLOCK)
        gate = [jnp.sum(raw[:, blk_lanes(jj)], axis=-1, keepdims=True) * (1.0 / MOBA_BLOCK) for jj in range(n_blk)]
        sel = []
        for jj in range(n_blk):
            ahead = jnp.zeros(gate[jj].shape, F32)
            for kk in range(n_blk):
                if kk == jj:
                    continue
                beats = (gate[kk] >= gate[jj]) if kk < jj else (gate[kk] > gate[jj])
                ahead = ahead + jnp.where(beats, 1.0, 0.0)
            sel.append(ahead < float(min(MOBA_TOPK, n_blk)))
        logit = jnp.concatenate(
            [jnp.where(sel[jj], raw[:, blk_lanes(jj)] + past_ref[p, :, blk_lanes(jj)], -jnp.inf)
             for jj in range(n_blk)], axis=1)
        kn = kn_ref[0, :, cols]
        s_new = lax.dot_general(qm, kn, NT, preferred_element_type=F32) + new_ref[p]
        m = jnp.maximum(jnp.max(s_new, axis=-1, keepdims=True), jnp.max(logit, axis=-1, keepdims=True))
        p_new = jnp.exp(s_new - m)
        pexp = jnp.exp(logit - m)
        denom = jnp.sum(p_new, axis=-1, keepdims=True) + jnp.sum(pexp, axis=-1, keepdims=True)
        vT = jnp.concatenate([v_pages[s][0, 0, cols, :] for s in range(n_pages)], axis=1).astype(BF16)
        acc = (jnp.dot(p_new.astype(BF16), vn_ref[0, :, cols], preferred_element_type=F32)
               + lax.dot_general(pexp.astype(BF16), vT, NT, preferred_element_type=F32))
        out = acc / denom
        o_ref[0, :, cols] = jnp.where(lane[:L] < MOBA_DH, out[:L], out[L:2 * L])


def _moba_sample(q16, kn16, vn16, cache_kT, cache_vT, page_table, layer, L):
    db, n_pages = page_table.shape
    past, new = _moba_sample_tables(L, n_pages)
    b3 = lambda b, pt: (b, 0, 0)
    row_spec = pl.BlockSpec((1, SAMPLE_ROWS, MOBA_WIDTH), b3)

    def page_spec(s):
        return pl.BlockSpec((1, 1, MOBA_WIDTH, PAGE_SIZE), lambda b, pt: (layer, pt[b, s], 0, 0))

    in_specs = [row_spec, row_spec, row_spec,
                pl.BlockSpec(past.shape, lambda b, pt: (0, 0, 0)), pl.BlockSpec(new.shape, lambda b, pt: (0, 0, 0))]
    in_specs += [page_spec(s) for s in range(n_pages)] * 2
    grid_spec = pltpu.PrefetchScalarGridSpec(
        num_scalar_prefetch=1, grid=(db,), in_specs=in_specs,
        out_specs=pl.BlockSpec((1, L, MOBA_WIDTH), b3))
    return pl.pallas_call(
        functools.partial(_moba_sample_kernel, L=L, n_pages=n_pages),
        grid_spec=grid_spec,
        out_shape=jax.ShapeDtypeStruct((db, L, MOBA_WIDTH), F32),
        compiler_params=_params("arbitrary"), name="moba_sample",
    )(page_table, q16, kn16, vn16, past, new, *([cache_kT] * n_pages), *([cache_vT] * n_pages))


def _route(x, wrh_ref, wrl_ref, br_ref, group=None):
    tm = x.shape[0]
    hi = x.astype(BF16)
    lo = (x - hi.astype(F32)).astype(BF16)
    logit = (jnp.dot(hi, wrh_ref[...], preferred_element_type=F32)
             + (jnp.dot(hi, wrl_ref[...], preferred_element_type=F32)
                + jnp.dot(lo, wrh_ref[...], preferred_element_type=F32))) + br_ref[...]
    lane = lax.broadcasted_iota(jnp.int32, (tm, ROUTER_LANES), 1)
    neg = -jnp.inf
    gl = jnp.where(lane < N_GROUPS, logit, neg)
    gmax = jnp.max(gl, axis=-1, keepdims=True)
    g_sum = jnp.sum(jnp.exp(gl - gmax), axis=-1, keepdims=True)
    if group is None:
        gidx = jnp.min(jnp.where(gl == gmax, lane, ROUTER_LANES), axis=-1, keepdims=True)
        g_w = 1.0 / g_sum
    else:
        gidx = group
        g_w = jnp.exp(jnp.sum(jnp.where(lane == group, logit, 0.0), axis=-1, keepdims=True) - gmax) / g_sum
    first = N_GROUPS + EXPERTS_PER_GROUP * gidx
    in_group = jnp.logical_and(lane >= first, lane < first + EXPERTS_PER_GROUP)
    el = jnp.where(in_group, logit, neg)
    e1 = jnp.max(el, axis=-1, keepdims=True)
    i1 = jnp.min(jnp.where(el == e1, lane, ROUTER_LANES), axis=-1, keepdims=True)
    el2 = jnp.where(lane == i1, neg, el)
    e2 = jnp.max(el2, axis=-1, keepdims=True)
    i2 = jnp.min(jnp.where(el2 == e2, lane, ROUTER_LANES), axis=-1, keepdims=True)
    t = jnp.exp(e2 - e1)
    return gidx, i1, i2, g_w / (1.0 + t), g_w * t / (1.0 + t), lane


def _expert_lanes(weights, first_lane, tm):
    return jnp.concatenate([jnp.broadcast_to(weights[:, first_lane + e:first_lane + e + 1], (tm, EXPERT_HIDDEN))
                            for e in range(EXPERTS_PER_GROUP)], axis=1)


def _group_experts(xb, cexp, weg, weu, wed):
    hg = jnp.dot(xb, weg, preferred_element_type=F32)
    hu = jnp.dot(xb, weu, preferred_element_type=F32)
    hid = (hg * _sigmoid(hg)) * hu
    return jnp.dot((hid * cexp).astype(BF16), wed, preferred_element_type=F32)


def _merge_kernel(x_ref, or_ref, om_ref, wg_ref, bg_ref, wr_ref, wm_ref, wo_ref, g_ref, b_ref, *refs, with_route):
    x = x_ref[...]
    gates = _sigmoid(jnp.dot(x.astype(BF16), wg_ref[...], preferred_element_type=F32) + bg_ref[...])
    br = jnp.dot(or_ref[...], wr_ref[...], preferred_element_type=F32)
    bm = jnp.dot(om_ref[...], wm_ref[...], preferred_element_type=F32)
    merged = gates[:, :D_MODEL] * br + gates[:, D_MODEL:] * bm
    y = jnp.dot(merged.astype(BF16), wo_ref[...], preferred_element_type=F32)
    x1 = _layer_norm_rows(ALPHA * x + y, g_ref[...], b_ref[...])
    if not with_route:
        refs[0][...] = x1
        return
    wrh_ref, wrl_ref, brt_ref, o_ref, group_ref = refs
    tm = x1.shape[0]
    for c in range(ROW_CHUNKS):
        o_ref[pl.ds(c, tm, stride=ROW_CHUNKS), :] = x1[:, c * LANES:(c + 1) * LANES]
    gidx = _route(x1, wrh_ref, wrl_ref, brt_ref)[0]
    group_ref[...] = jnp.broadcast_to(gidx, group_ref.shape)


def _merge(x, o_r, o_m, wg16, bg, wr16, wm16, wo16, g, b, router=None, *, tm):
    T = o_r.shape[0]
    row = lambda i: (i, 0)
    consts = [wg16, bg, wr16, wm16, wo16, g, b] + list(router or ())
    if router:
        out_shape = [jax.ShapeDtypeStruct((T * ROW_CHUNKS, LANES), F32),
                     jax.ShapeDtypeStruct((T, ROUTER_LANES), jnp.int32)]
        out_specs = [pl.BlockSpec((tm * ROW_CHUNKS, LANES), row), pl.BlockSpec((tm, ROUTER_LANES), row)]
    else:
        out_shape = [jax.ShapeDtypeStruct((T, D_MODEL), F32)]
        out_specs = [pl.BlockSpec((tm, D_MODEL), row)]
    outs = pl.pallas_call(
        functools.partial(_merge_kernel, with_route=bool(router)),
        grid=(T // tm,),
        in_specs=[pl.BlockSpec((tm, D_MODEL), row), pl.BlockSpec((tm, RET_WIDTH), row),
                  pl.BlockSpec((tm, MOBA_WIDTH), row)] + [_const_spec(c.shape) for c in consts],
        out_specs=out_specs, out_shape=out_shape,
        compiler_params=_params("arbitrary"), name="merge",
    )(x, o_r, o_m, *consts)
    return outs if router else outs[0]


def _moe_kernel(x_ref, wrh_ref, wrl_ref, br_ref, weg_ref, weu_ref, wed_ref, g_ref, b_ref, o_ref):
    x = x_ref[...]
    tm = x.shape[0]
    gidx, i1, i2, w1, w2, lane = _route(x, wrh_ref, wrl_ref, br_ref)
    comb = jnp.where(lane == i1, w1, 0.0) + jnp.where(lane == i2, w2, 0.0)
    xb = x.astype(BF16)
    acc = jnp.zeros((tm, D_MODEL), F32)
    for g in range(N_GROUPS):
        cexp = _expert_lanes(comb, N_GROUPS + g * EXPERTS_PER_GROUP, tm)
        acc = acc + _group_experts(xb, cexp, weg_ref[g], weu_ref[g], wed_ref[g])
    o_ref[...] = _layer_norm_rows(ALPHA * x + acc, g_ref[...], b_ref[...])


def _moe(x, wr_hi, wr_lo, br, weg16, weu16, wed16, g, b, *, tm):
    T = x.shape[0]
    row = lambda i: (i, 0)
    consts = [wr_hi, wr_lo, br, weg16, weu16, wed16, g, b]
    return pl.pallas_call(
        _moe_kernel,
        grid=(T // tm,),
        in_specs=[pl.BlockSpec((tm, D_MODEL), row)] + [_const_spec(c.shape) for c in consts],
        out_specs=pl.BlockSpec((tm, D_MODEL), row),
        out_shape=jax.ShapeDtypeStruct((T, D_MODEL), F32),
        compiler_params=_params("arbitrary"), name="moe",
    )(x, *consts)


def _experts_by_group_kernel(w_ref, o_ref, *, side_by_side):
    for e in range(EXPERTS_PER_GROUP):
        cols = slice(e * EXPERT_HIDDEN, (e + 1) * EXPERT_HIDDEN)
        if side_by_side:
            o_ref[0, :, cols] = w_ref[0, e].astype(BF16)
        else:
            o_ref[0, cols, :] = w_ref[0, e].astype(BF16)


def _experts_by_group(w, layer, *, side_by_side):
    return pl.pallas_call(
        functools.partial(_experts_by_group_kernel, side_by_side=side_by_side),
        grid=(N_GROUPS,),
        in_specs=[pl.BlockSpec((1, EXPERTS_PER_GROUP) + w.shape[2:], lambda g: (layer, g, 0, 0))],
        out_specs=pl.BlockSpec((1, D_MODEL, GROUP_HIDDEN) if side_by_side else (1, GROUP_HIDDEN, D_MODEL),
                               lambda g: (g, 0, 0)),
        out_shape=jax.ShapeDtypeStruct((N_GROUPS, D_MODEL, GROUP_HIDDEN) if side_by_side
                                       else (N_GROUPS, GROUP_HIDDEN, D_MODEL), BF16),
        compiler_params=_params("arbitrary"), name="experts_by_group",
    )(w)


def _group_plan(gid, tm):
    T = gid.shape[0]
    groups = jnp.arange(N_GROUPS, dtype=jnp.int32)
    member = (gid[:, None] == groups[None, :]).astype(jnp.int32)
    running = jnp.cumsum(member, axis=0)
    counts = running[-1]
    rank = jnp.sum(running * member, axis=1) - 1
    padded = ((counts + tm - 1) // tm) * tm
    ends = jnp.cumsum(padded)
    starts = ends - padded
    pos = jnp.sum(starts[None, :] * member, axis=1) + rank
    n_tiles = T // tm + N_GROUPS
    rows = jnp.arange(T, dtype=jnp.int32)
    pad_dst = T + jnp.arange(tm, dtype=jnp.int32)
    src = jnp.zeros((n_tiles * tm,), jnp.int32).at[pos].set(rows, unique_indices=True)
    tile_start = jnp.arange(n_tiles, dtype=jnp.int32) * tm
    tile_group = jnp.minimum(jnp.sum((tile_start[:, None] >= ends[None, :]).astype(jnp.int32), axis=1), N_GROUPS - 1)
    in_tile = (tile_group[:, None] == groups[None, :]).astype(jnp.int32)
    fill = jnp.sum(in_tile * (starts + counts)[None, :], axis=1)[:, None] - tile_start[:, None]
    real = (jnp.arange(tm, dtype=jnp.int32)[None, :] < fill).reshape(n_tiles * tm)
    dst = jnp.where(real, src, jnp.tile(pad_dst, n_tiles))
    dst = jnp.concatenate([pad_dst, dst]).reshape(n_tiles + 1, 1, tm)
    return src.reshape(n_tiles, 1, tm), dst, tile_group


def _moe_grouped_kernel(tg_ref, src_now_ref, src_next_ref, dst_prev_ref, dst_now_ref,
                        x_hbm, wrh_ref, wrl_ref, br_ref, weg_ref, weu_ref, wed_ref, g_ref, b_ref, out_hbm,
                        xbuf, obuf, gsem, ssem, *, tm):
    i = pl.program_id(0)
    last = pl.num_programs(0) - 1
    slot = lax.rem(i, 2)
    other = 1 - slot

    def fetch_rows(idx_ref, s, rows=range(tm)):
        for r in rows:
            t = idx_ref[0, 0, r]
            pltpu.make_async_copy(x_hbm.at[pl.ds(pl.multiple_of(t * ROW_CHUNKS, ROW_CHUNKS), ROW_CHUNKS)],
                                  xbuf.at[s, pl.ds(r * ROW_CHUNKS, ROW_CHUNKS)], gsem.at[s]).start(priority=r % 2)

    def fetch_wait(s):
        pltpu.make_async_copy(x_hbm.at[pl.ds(0, tm * ROW_CHUNKS)], xbuf.at[s], gsem.at[s]).wait()

    def write_rows(idx_ref, s, rows=range(tm)):
        for r in rows:
            t = idx_ref[0, 0, r]
            pltpu.make_async_copy(obuf.at[s, pl.ds(r, 1)], out_hbm.at[pl.ds(t, 1)], ssem.at[s]).start(priority=r % 2)

    def write_wait(s):
        pltpu.make_async_copy(obuf.at[s], out_hbm.at[pl.ds(0, tm)], ssem.at[s]).wait()

    @pl.when(i == 0)
    def _():
        obuf[...] = jnp.zeros_like(obuf)
        fetch_rows(src_now_ref, 0)
        fetch_wait(0)

    fetch_rows(src_next_ref, other)
    write_rows(dst_prev_ref, other)
    x = jnp.concatenate([xbuf[slot, pl.ds(c, tm, stride=ROW_CHUNKS), :] for c in range(ROW_CHUNKS)], axis=1)
    group = tg_ref[i]
    _, i1, i2, w1, w2, lane = _route(x, wrh_ref, wrl_ref, br_ref, group=group)
    first = N_GROUPS + EXPERTS_PER_GROUP * group
    cexp = _expert_lanes(jnp.where(lane == i1 - first, w1, 0.0) + jnp.where(lane == i2 - first, w2, 0.0), 0, tm)
    acc = _group_experts(x.astype(BF16), cexp, weg_ref[0], weu_ref[0], wed_ref[0])
    obuf[slot] = _layer_norm_rows(ALPHA * x + acc, g_ref[...], b_ref[...])
    fetch_wait(other)
    write_wait(other)

    @pl.when(i == last)
    def _():
        write_rows(dst_now_ref, slot)
        write_wait(slot)


def _moe_grouped(x, T, src, dst, tile_group, wr_hi, wr_lo, br, weg16, weu16, wed16, g, b, *, tm):
    n_tiles = src.shape[0]
    smem_tile = lambda fn: pl.BlockSpec((1, 1, tm), fn, memory_space=pltpu.SMEM)
    by_group = lambda i, tg: (tg[i], 0, 0)
    w_spec = pl.BlockSpec((1, D_MODEL, GROUP_HIDDEN), by_group)
    fixed = lambda t: pl.BlockSpec(t.shape, lambda i, tg: (0, 0))
    grid_spec = pltpu.PrefetchScalarGridSpec(
        num_scalar_prefetch=1, grid=(n_tiles,),
        in_specs=[smem_tile(lambda i, tg: (i, 0, 0)),
                  smem_tile(lambda i, tg: (jnp.minimum(i + 1, n_tiles - 1), 0, 0)),
                  smem_tile(lambda i, tg: (i, 0, 0)), smem_tile(lambda i, tg: (i + 1, 0, 0)),
                  pl.BlockSpec(memory_space=pl.ANY), fixed(wr_hi), fixed(wr_lo), fixed(br),
                  w_spec, w_spec, pl.BlockSpec((1, GROUP_HIDDEN, D_MODEL), by_group), fixed(g), fixed(b)],
        out_specs=pl.BlockSpec(memory_space=pl.ANY),
        scratch_shapes=[pltpu.VMEM((2, tm * ROW_CHUNKS, LANES), F32), pltpu.VMEM((2, tm, D_MODEL), F32),
                        pltpu.SemaphoreType.DMA((2,)), pltpu.SemaphoreType.DMA((2,))])
    return pl.pallas_call(
        functools.partial(_moe_grouped_kernel, tm=tm),
        grid_spec=grid_spec,
        out_shape=jax.ShapeDtypeStruct((T + tm, D_MODEL), F32),
        compiler_params=_params("arbitrary"), name="moe_grouped",
    )(tile_group, src, src, dst, dst, x, wr_hi, wr_lo, br, weg16, weu16, wed16, g, b)


def _layer_weights(l, w_in, b_merge, gn_g, w_br_ret, w_br_moba, w_out, ln1_g, ln1_b, w_router_group,
                   b_router_group, w_router_expert, b_router_expert, w_exp_gate, w_exp_up, w_exp_down,
                   ln2_g, ln2_b):
    w = w_in[l]
    q0 = 4 * RET_WIDTH
    n_exp = N_GROUPS * EXPERTS_PER_GROUP
    w_r = jnp.concatenate(
        [w_router_group[l], w_router_expert[l].transpose(1, 0, 2).reshape(D_MODEL, n_exp),
         jnp.zeros((D_MODEL, ROUTER_LANES - N_GROUPS - n_exp), F32)], axis=1)
    w_r_hi = w_r.astype(BF16)
    w_r_lo = (w_r - w_r_hi.astype(F32)).astype(BF16)
    b_r = jnp.concatenate([b_router_group[l], b_router_expert[l].reshape(n_exp),
                           jnp.zeros((ROUTER_LANES - N_GROUPS - n_exp,), F32)])[None, :]
    return dict(
        w_proj=w[:, :N_PROJ].astype(BF16),
        w_ret=w[:, :q0].astype(BF16),
        w_mobaT=w[:, q0:N_PROJ].T.reshape(3, MOBA_WIDTH, D_MODEL).astype(BF16),
        w_gate=w[:, N_PROJ:].astype(BF16), b_gate=b_merge[l][None, :],
        gn=gn_g[l][None, :],
        w_br_ret=w_br_ret[l].astype(BF16), w_br_moba=w_br_moba[l].astype(BF16), w_out=w_out[l].astype(BF16),
        ln1_g=ln1_g[l][None, :], ln1_b=ln1_b[l][None, :],
        w_r_hi=w_r_hi, w_r_lo=w_r_lo, b_r=b_r,
        w_eg=_experts_by_group(w_exp_gate, l, side_by_side=True),
        w_eu=_experts_by_group(w_exp_up, l, side_by_side=True),
        w_ed=_experts_by_group(w_exp_down, l, side_by_side=False),
        ln2_g=ln2_g[l][None, :], ln2_b=ln2_b[l][None, :])


def _pad_rows(t, rows):
    return jnp.pad(t, ((0, 0), (0, rows - t.shape[1]), (0, 0)))


def kernel(x_prompt, x_sample, cache_k, cache_v, state_ret, page_table, w_in, b_merge, gn_g, w_br_ret, w_br_moba, w_out, ln1_g, ln1_b, w_router_group, b_router_group, w_router_expert, b_router_expert, w_exp_gate, w_exp_up, w_exp_down, ln2_g, ln2_b):
    B, S, _ = x_prompt.shape
    DB, L, _ = x_sample.shape
    Tp, Ts = B * S, DB * L
    depth = w_in.shape[0]
    n_pool = cache_k.shape[1]
    page_major = lambda c: c.transpose(0, 1, 3, 4, 2).reshape(depth, n_pool, MOBA_WIDTH, PAGE_SIZE)
    cache_kT, cache_vT = page_major(cache_k), page_major(cache_v)
    xp = x_prompt.reshape(Tp, D_MODEL)
    xs = x_sample.reshape(Ts, D_MODEL)
    tm_p = min(512, Tp)
    tm_s = min(256, Ts)
    tm_moe = 256
    outs = [[] for _ in range(6)]
    for l in range(depth):
        W = _layer_weights(l, w_in, b_merge, gn_g, w_br_ret, w_br_moba, w_out, ln1_g, ln1_b, w_router_group,
                           b_router_group, w_router_expert, b_router_expert, w_exp_gate, w_exp_up, w_exp_down,
                           ln2_g, ln2_b)
        (rq, rk, rv, rg, kT_p, vT_p, k16, qT16, vT16, kmean) = _in_proj(
            xp, W["w_ret"], W["w_mobaT"], rows=Tp, batch=B, tm=tm_p)
        o_r, s_p = _ret_prompt(rq, rk, rv, rg, W["gn"], B)
        o_m = _moba_prompt(qT16, k16, vT16, kmean, B)
        router = (W["w_r_hi"], W["w_r_lo"], W["b_r"])
        x1, group = _merge(xp, o_r, o_m, W["w_gate"], W["b_gate"], W["w_br_ret"], W["w_br_moba"], W["w_out"],
                           W["ln1_g"], W["ln1_b"], router=router, tm=tm_p)
        plan = _group_plan(group[:, 0], tm_moe)
        xp = _moe_grouped(x1, Tp, *plan, *router, W["w_eg"], W["w_eu"], W["w_ed"], W["ln2_g"], W["ln2_b"], tm=tm_moe)
        (rq, rk, rv, rg, mq, k_s, v_s) = _in_proj(xs, W["w_proj"], tm=tm_s)
        r3 = lambda t: t.reshape(DB, L, t.shape[-1])
        o_r, s_s = _ret_sample(_pad_rows(r3(rq), SAMPLE_ROWS), _pad_rows(r3(rk), SAMPLE_ROWS),
                               _pad_rows(r3(rv), SAMPLE_ROWS), r3(rg), W["gn"], state_ret, l)
        mq3 = r3(mq)
        q16 = _pad_rows(jnp.concatenate([mq3, mq3], axis=1), SAMPLE_ROWS)
        o_m = _moba_sample(q16, _pad_rows(r3(k_s).astype(BF16), SAMPLE_ROWS),
                           _pad_rows(r3(v_s).astype(BF16), SAMPLE_ROWS),
                           cache_kT, cache_vT, page_table, l, L)
        x1 = _merge(xs, o_r.reshape(Ts, RET_WIDTH).astype(BF16), o_m.reshape(Ts, MOBA_WIDTH).astype(BF16),
                    W["w_gate"], W["b_gate"], W["w_br_ret"], W["w_br_moba"], W["w_out"],
                    W["ln1_g"], W["ln1_b"], tm=tm_s)
        xs = _moe(x1, W["w_r_hi"], W["w_r_lo"], W["b_r"], W["w_eg"], W["w_eu"], W["w_ed"],
                  W["ln2_g"], W["ln2_b"], tm=tm_s)
        for lst, val in zip(outs, (kT_p, vT_p, s_p,
                                   k_s.reshape(DB, L, MOBA_HEADS, MOBA_DH), v_s.reshape(DB, L, MOBA_HEADS, MOBA_DH), s_s)):
            lst.append(val)
    kTp, vTp, sp, ksm, vsm, ssm = (jnp.stack(o) for o in outs)
    token_major = lambda t: t.reshape(depth, B, MOBA_HEADS, MOBA_DH, S).transpose(0, 1, 4, 2, 3)
    return (xp[:Tp].reshape(B, S, D_MODEL), xs.reshape(DB, L, D_MODEL), token_major(kTp), token_major(vTp), sp,
            ksm, vsm, ssm)
```

```python
import functools

import numpy as np
import jax
import jax.numpy as jnp
from jax import lax
from jax.experimental import pallas as pl
from jax.experimental.pallas import tpu as pltpu

F32 = jnp.float32
BF16 = jnp.bfloat16

D_MODEL = 1024
RET_HEADS = 4
RET_DK = 128
RET_WIDTH = RET_HEADS * RET_DK
RET_CHUNK = 128
MOBA_HEADS = 8
MOBA_DH = 64
MOBA_WIDTH = MOBA_HEADS * MOBA_DH
MOBA_BLOCK = 256
MOBA_TOPK = 3
PAGE_SIZE = 128
N_GROUPS = 4
EXPERTS_PER_GROUP = 8
EXPERT_HIDDEN = 128
GROUP_HIDDEN = EXPERTS_PER_GROUP * EXPERT_HIDDEN
DEPTH = 2
ALPHA = (2 * DEPTH) ** 0.25
LN_EPS = 1e-5
N_PROJ = 4 * RET_WIDTH + 3 * MOBA_WIDTH

LANES = 128
ROW_CHUNKS = D_MODEL // LANES
PAIR = 2 * MOBA_DH
N_PAIRS = MOBA_HEADS // 2
GROUP_HEADS = 4
DENOM_ROWS = 16
LOG2E = 1.4426950408889634
SAMPLE_ROWS = 16
ROUTER_LANES = 128
VMEM_LIMIT = 56 * 1024 * 1024

NT = (((1,), (1,)), ((), ()))
TN = (((0,), (0,)), ((), ()))


def _params(*sem):
    return pltpu.CompilerParams(dimension_semantics=sem, vmem_limit_bytes=VMEM_LIMIT)


def _const_spec(shape):
    nd = len(shape)
    return pl.BlockSpec(shape, lambda *_: (0,) * nd, pipeline_mode=pl.Buffered(1))


def _layer_norm_rows(z, g, b):
    mu = jnp.mean(z, axis=-1, keepdims=True)
    zc = z - mu
    var = jnp.mean(zc * zc, axis=-1, keepdims=True)
    return zc * lax.rsqrt(var + LN_EPS) * g + b


def _sigmoid(x):
    return 1.0 / (1.0 + jnp.exp(-x))


def _in_proj_kernel(x_ref, w_ref, *refs, transposed, tm, carried):
    xb = x_ref[...].astype(BF16)

    def proj(c):
        return jnp.dot(xb, w_ref[:, c * RET_WIDTH:(c + 1) * RET_WIDTH], preferred_element_type=F32)

    if transposed:
        wT_ref = refs[0]
        outs = refs[3:] if carried else refs[1:]
        rq_ref, rk_ref, rv_ref, rg_ref, kT_ref, vT_ref, k16_ref, qT16_ref, vT16_ref, kmean_ref = outs
        projT = lambda c: lax.dot_general(wT_ref[c], xb, NT, preferred_element_type=F32)
        qT16_ref[...] = (projT(0) * MOBA_DH ** -0.5).astype(BF16)
        kT = projT(1)
        kT_ref[0, 0] = kT
        k = kT.T
        k16_ref[...] = k.astype(BF16)
        nb = tm // MOBA_BLOCK
        kmean_ref[0] = jnp.mean(k.reshape(nb, MOBA_BLOCK, MOBA_WIDTH), axis=1)
        vT = projT(2)
        vT_ref[0, 0] = vT
        vT16_ref[...] = vT.astype(BF16)
        for later in range(1, kT_ref.shape[0]):
            kT_ref[later, 0] = jnp.zeros_like(kT)
            vT_ref[later, 0] = jnp.zeros_like(vT)
    else:
        rq_ref, rk_ref, rv_ref, rg_ref, mq_ref, k_ref, v_ref = refs
        mq_ref[...] = (proj(4) * MOBA_DH ** -0.5).astype(BF16)
        k_ref[...] = proj(5)
        v_ref[...] = proj(6)
    rq_ref[...] = proj(0).astype(BF16)
    rk_ref[...] = (proj(1) * RET_DK ** -0.5).astype(BF16)
    rv_ref[...] = proj(2).astype(BF16)
    rg_ref[...] = proj(3)


def _in_proj(x, w16, wT16=None, *, rows=None, batch=1, tm, layer=0, depth=1, kv_all=None):
    T = rows or x.shape[0]
    transposed = wT16 is not None
    aliases = {}
    row = lambda i: (i, 0)
    tile = lambda: pl.BlockSpec((tm, RET_WIDTH), row)
    in_specs = [pl.BlockSpec((tm, D_MODEL), row), _const_spec(w16.shape)]
    args = [x, w16]
    out_shape = [jax.ShapeDtypeStruct((T, RET_WIDTH), BF16)] * 3 + [jax.ShapeDtypeStruct((T, RET_WIDTH), F32)]
    out_specs = [tile() for _ in range(4)]
    if transposed:
        S = T // batch
        per_b = S // tm
        in_specs.append(_const_spec(wT16.shape))
        args.append(wT16)
        col = lambda i: (0, i)
        if kv_all is None:
            kv_spec = pl.BlockSpec((depth, 1, MOBA_WIDTH, tm), lambda i: (0, i // per_b, 0, i % per_b))
        else:
            kv_spec = pl.BlockSpec((1, 1, MOBA_WIDTH, tm), lambda i: (layer, i // per_b, 0, i % per_b))
            aliases = {len(args): len(out_shape), len(args) + 1: len(out_shape) + 1}
            in_specs += [pl.BlockSpec(memory_space=pl.ANY)] * 2
            args += list(kv_all)
        out_shape += [jax.ShapeDtypeStruct((depth, batch, MOBA_WIDTH, S), F32)] * 2 + [
            jax.ShapeDtypeStruct((T, MOBA_WIDTH), BF16),
            jax.ShapeDtypeStruct((MOBA_WIDTH, T), BF16), jax.ShapeDtypeStruct((MOBA_WIDTH, T), BF16),
            jax.ShapeDtypeStruct((T // tm, tm // MOBA_BLOCK, MOBA_WIDTH), F32)]
        out_specs += [kv_spec, kv_spec,
                      tile(), pl.BlockSpec((MOBA_WIDTH, tm), col), pl.BlockSpec((MOBA_WIDTH, tm), col),
                      pl.BlockSpec((1, tm // MOBA_BLOCK, MOBA_WIDTH), lambda i: (i, 0, 0))]
    else:
        out_shape += [jax.ShapeDtypeStruct((T, MOBA_WIDTH), BF16)] + [jax.ShapeDtypeStruct((T, MOBA_WIDTH), F32)] * 2
        out_specs += [tile() for _ in range(3)]
    outs = pl.pallas_call(
        functools.partial(_in_proj_kernel, transposed=transposed, tm=tm, carried=bool(aliases)),
        grid=(T // tm,), in_specs=in_specs, out_specs=out_specs, out_shape=out_shape,
        input_output_aliases=aliases,
        compiler_params=_params("arbitrary"), name="in_proj",
    )(*args)
    if transposed:
        outs = list(outs)
        outs[-1] = outs[-1].reshape(T // MOBA_BLOCK, MOBA_WIDTH)
    return outs


def _ret_tables(L, rows):
    log_g = jnp.log(jnp.asarray(1.0 - 2.0 ** (-5.0 - np.arange(RET_HEADS)), dtype=F32))
    idx = jnp.arange(L, dtype=F32)
    diff = idx[:, None] - idx[None, :]
    decay = jnp.where(diff >= 0, jnp.exp(log_g[:, None, None] * jnp.maximum(diff, 0.0)), 0.0)
    qdec = jnp.exp(log_g[:, None] * (idx + 1.0))
    kdec = jnp.exp(log_g[:, None] * (L - 1.0 - idx))
    g_chunk = jnp.exp(log_g * L)
    pad = rows - L
    decay = jnp.pad(decay, ((0, 0), (0, pad), (0, pad)))
    lanes = lambda t: jnp.broadcast_to(jnp.pad(t, ((0, 0), (0, pad)))[:, :, None], (RET_HEADS, rows, LANES))
    return decay, lanes(qdec), lanes(kdec), jnp.broadcast_to(g_chunk[:, None, None], (RET_HEADS, 1, LANES))


def _ret_head(q, k, v, state, decay, qdec, kdec, g_chunk):
    scores = lax.dot_general(q, k, NT, preferred_element_type=F32) * decay
    inner = jnp.dot(scores.astype(BF16), v, preferred_element_type=F32)
    q_dec = (q.astype(F32) * qdec).astype(BF16)
    cross = jnp.dot(q_dec, state.astype(BF16), preferred_element_type=F32)
    k_dec = (k.astype(F32) * kdec).astype(BF16)
    new_state = state * g_chunk + lax.dot_general(k_dec, v, TN, preferred_element_type=F32)
    return inner + cross, new_state


def _ret_gate(o, rg, gn):
    mu = jnp.mean(o, axis=-1, keepdims=True)
    oc = o - mu
    var = jnp.mean(oc * oc, axis=-1, keepdims=True)
    return (rg * _sigmoid(rg)) * (oc * lax.rsqrt(var + LN_EPS) * gn)


def _ret_prompt_kernel(q_ref, k_ref, v_ref, rg_ref, gn_ref, decay_ref, qdec_ref, kdec_ref, gc_ref,
                       o_ref, state_ref):
    @pl.when(pl.program_id(0) == 0)
    def _():
        state_ref[...] = jnp.zeros_like(state_ref)

    for b in range(q_ref.shape[0]):
        for h in range(RET_HEADS):
            sl = slice(h * RET_DK, (h + 1) * RET_DK)
            o, new_state = _ret_head(q_ref[b, :, sl], k_ref[b, :, sl], v_ref[b, :, sl], state_ref[b, h],
                                     decay_ref[h], qdec_ref[h], kdec_ref[h], gc_ref[h])
            state_ref[b, h] = new_state
            o_ref[b, :, sl] = _ret_gate(o, rg_ref[b, :, sl], gn_ref[:, sl]).astype(BF16)


def _ret_prompt(rq, rk, rv, rg, gn, batch):
    T = rq.shape[0]
    S = T // batch
    tables = _ret_tables(RET_CHUNK, RET_CHUNK)
    by_batch = lambda t: t.reshape(batch, S, RET_WIDTH)
    tile = pl.BlockSpec((batch, RET_CHUNK, RET_WIDTH), lambda c: (0, c, 0))
    o_r, state = pl.pallas_call(
        _ret_prompt_kernel,
        grid=(S // RET_CHUNK,),
        in_specs=[tile, tile, tile, tile, _const_spec(gn.shape)] + [_const_spec(t.shape) for t in tables],
        out_specs=[tile, pl.BlockSpec((batch, RET_HEADS, RET_DK, RET_DK), lambda c: (0, 0, 0, 0))],
        out_shape=[jax.ShapeDtypeStruct((batch, S, RET_WIDTH), BF16),
                   jax.ShapeDtypeStruct((batch, RET_HEADS, RET_DK, RET_DK), F32)],
        compiler_params=_params("arbitrary"), name="ret_prompt",
    )(by_batch(rq), by_batch(rk), by_batch(rv), by_batch(rg), gn, *tables)
    return o_r.reshape(T, RET_WIDTH), state


def _ret_sample_kernel(q_ref, k_ref, v_ref, rg_ref, gn_ref, s_ref, decay_ref, qdec_ref, kdec_ref, gc_ref,
                       o_ref, snew_ref, *, bt, L):
    for bi in range(bt):
        for h in range(RET_HEADS):
            sl = slice(h * RET_DK, (h + 1) * RET_DK)
            o, new_state = _ret_head(q_ref[bi, :, sl], k_ref[bi, :, sl], v_ref[bi, :, sl], s_ref[0, bi, h],
                                     decay_ref[h], qdec_ref[h], kdec_ref[h], gc_ref[h])
            snew_ref[bi, h] = new_state
            o_ref[bi, :, sl] = _ret_gate(o[:L], rg_ref[bi, :, sl], gn_ref[:, sl])


def _ret_sample(rq, rk, rv, rg, gn, state, layer, *, bt=8):
    db, L = rg.shape[0], rg.shape[1]
    tables = _ret_tables(L, SAMPLE_ROWS)
    b3 = lambda i: (i, 0, 0)
    qkv = pl.BlockSpec((bt, SAMPLE_ROWS, RET_WIDTH), b3)
    st = pl.BlockSpec((bt, RET_HEADS, RET_DK, RET_DK), lambda i: (i, 0, 0, 0))
    st_in = pl.BlockSpec((1, bt, RET_HEADS, RET_DK, RET_DK), lambda i: (layer, i, 0, 0, 0))
    return pl.pallas_call(
        functools.partial(_ret_sample_kernel, bt=bt, L=L),
        grid=(db // bt,),
        in_specs=[qkv, qkv, qkv, pl.BlockSpec((bt, L, RET_WIDTH), b3), _const_spec(gn.shape), st_in]
        + [_const_spec(t.shape) for t in tables],
        out_specs=[pl.BlockSpec((bt, L, RET_WIDTH), b3), st],
        out_shape=[jax.ShapeDtypeStruct((db, L, RET_WIDTH), F32),
                   jax.ShapeDtypeStruct((db, RET_HEADS, RET_DK, RET_DK), F32)],
        compiler_params=_params("arbitrary"), name="ret_sample",
    )(rq, rk, rv, rg, gn, state, *tables)


def _alibi_slopes():
    return 2.0 ** (-8.0 * np.arange(1, MOBA_HEADS + 1) / MOBA_HEADS)


def _moba_prompt_tables():
    slopes = jnp.asarray(_alibi_slopes(), dtype=F32)[:, None, None]
    kk = jnp.arange(MOBA_BLOCK, dtype=F32)[:, None]
    qq = jnp.arange(MOBA_BLOCK, dtype=F32)[None, :]
    dist = (qq - kk)[None]
    past = -(slopes * dist) * LOG2E
    own = jnp.where(dist >= 0, past, -jnp.inf)
    block_step = -(slopes * float(MOBA_BLOCK)) * LOG2E
    return past, own, jnp.broadcast_to(block_step, (MOBA_HEADS, 1, MOBA_BLOCK))


def _for_blocks(n, body):
    def four(i, carry):
        for u in range(4):
            body(4 * i + u)
        return carry

    lax.fori_loop(0, lax.shift_right_logical(n, 2), four, 0)
    done = lax.bitwise_and(n, -4)

    @pl.when(lax.bitwise_and(n, 2) == 2)
    def _():
        body(done)
        body(done + 1)

    @pl.when(lax.bitwise_and(n, 1) == 1)
    def _():
        body(n - 1)


def _moba_prompt_kernel(qT_ref, k_ref, vT_ref, kmean_ref, past_ref, own_ref, step_ref, o_ref,
                        q_sc, s_sc, m_sc, acc_sc, term_sc):
    j = pl.program_id(2)
    nb = kmean_ref.shape[0]
    dh_row = lax.broadcasted_iota(jnp.int32, (PAIR, MOBA_BLOCK), 0)
    blk = lax.broadcasted_iota(jnp.int32, (nb, MOBA_BLOCK), 0)
    own_start = pl.multiple_of(j * MOBA_BLOCK, MOBA_BLOCK)
    pair_cols = lambda h: slice((h // 2) * PAIR, (h // 2 + 1) * PAIR)

    for h in range(GROUP_HEADS):
        qT = qT_ref[pair_cols(h), :]
        keep = (dh_row < MOBA_DH) if h % 2 == 0 else (dh_row >= MOBA_DH)
        qh = jnp.where(keep, qT, jnp.zeros_like(qT))
        q_sc[h] = qh
        gate = jnp.dot(kmean_ref[:, pair_cols(h)].astype(BF16), qh, preferred_element_type=F32)
        gate = jnp.where(blk < j, gate, -jnp.inf)
        sel = jnp.zeros(gate.shape, dtype=jnp.bool_)
        for _ in range(MOBA_TOPK):
            top = jnp.max(gate, axis=0, keepdims=True)
            first = jnp.min(jnp.where(gate == top, blk, nb), axis=0, keepdims=True)
            pick = jnp.logical_and(blk == first, top > -jnp.inf)
            sel = jnp.logical_or(sel, pick)
            gate = jnp.where(pick, -jnp.inf, gate)
        term_sc[h] = jnp.where(sel, (j - blk).astype(F32) * step_ref[h], jnp.where(blk == j, 0.0, -jnp.inf))
        s = jnp.dot(k_ref[pl.ds(own_start, MOBA_BLOCK), pair_cols(h)], qh, preferred_element_type=F32)
        s = s * LOG2E + own_ref[h]
        s_sc[h, j] = s
        m_sc[h] = jnp.max(s, axis=0, keepdims=True)
        acc_sc[h] = jnp.zeros_like(acc_sc[h])

    def scores(jj):
        start = pl.multiple_of(jj * MOBA_BLOCK, MOBA_BLOCK)
        for h in range(GROUP_HEADS):
            s = jnp.dot(k_ref[pl.ds(start, MOBA_BLOCK), pair_cols(h)], q_sc[h], preferred_element_type=F32)
            s = s * LOG2E + past_ref[h]
            s_sc[h, jj] = s
            m_sc[h] = jnp.maximum(m_sc[h], jnp.max(s, axis=0, keepdims=True) + term_sc[h, pl.ds(jj, 1), :])

    _for_blocks(j, scores)

    ones_rows = jnp.ones((DENOM_ROWS, MOBA_BLOCK), BF16)

    def apply_v(jj):
        start = pl.multiple_of(jj * MOBA_BLOCK, MOBA_BLOCK)
        for h in range(GROUP_HEADS):
            pexp = jnp.exp2((s_sc[h, jj] - (m_sc[h] - term_sc[h, pl.ds(jj, 1), :])).astype(BF16))
            vT = jnp.concatenate([vT_ref[h * MOBA_DH:(h + 1) * MOBA_DH, pl.ds(start, MOBA_BLOCK)], ones_rows], axis=0)
            acc_sc[h] = acc_sc[h] + jnp.dot(vT, pexp, preferred_element_type=F32)

    _for_blocks(j + 1, apply_v)

    def head_out(h):
        acc = acc_sc[h]
        return acc[:MOBA_DH] / acc[MOBA_DH:MOBA_DH + 1]

    for p in range(GROUP_HEADS // 2):
        outT = jnp.concatenate([head_out(2 * p), head_out(2 * p + 1)], axis=0)
        o_ref[:, p * PAIR:(p + 1) * PAIR] = outT.T.astype(BF16)


def _moba_prompt(qT16, k16, vT16, kmean, batch):
    T = k16.shape[0]
    S = T // batch
    nb = S // MOBA_BLOCK
    n_groups = MOBA_HEADS // GROUP_HEADS
    gw = GROUP_HEADS * MOBA_DH
    tables = _moba_prompt_tables()
    head_tile = lambda t: pl.BlockSpec((GROUP_HEADS,) + t.shape[1:], lambda b, g, j: (g, 0, 0))
    return pl.pallas_call(
        _moba_prompt_kernel,
        grid=(batch, n_groups, nb),
        in_specs=[pl.BlockSpec((gw, MOBA_BLOCK), lambda b, g, j: (g, b * nb + j)),
                  pl.BlockSpec((S, gw), lambda b, g, j: (b, g)),
                  pl.BlockSpec((gw, S), lambda b, g, j: (g, b)),
                  pl.BlockSpec((nb, gw), lambda b, g, j: (b, g))]
        + [head_tile(t) for t in tables],
        out_specs=pl.BlockSpec((MOBA_BLOCK, gw), lambda b, g, j: (b * nb + j, g)),
        out_shape=jax.ShapeDtypeStruct((T, MOBA_WIDTH), BF16),
        scratch_shapes=[pltpu.VMEM((GROUP_HEADS, PAIR, MOBA_BLOCK), BF16),
                        pltpu.VMEM((GROUP_HEADS, nb, MOBA_BLOCK, MOBA_BLOCK), F32),
                        pltpu.VMEM((GROUP_HEADS, 1, MOBA_BLOCK), F32),
                        pltpu.VMEM((GROUP_HEADS, MOBA_DH + DENOM_ROWS, MOBA_BLOCK), F32),
                        pltpu.VMEM((GROUP_HEADS, nb, MOBA_BLOCK), F32)],
        compiler_params=_params("arbitrary", "arbitrary", "arbitrary"), name="moba_prompt",
    )(qT16, k16, vT16, kmean, *tables)


def _moba_sample_tables(L, n_pages):
    past_len = n_pages * PAGE_SIZE
    slopes = _alibi_slopes()
    row_slope = np.zeros((N_PAIRS, SAMPLE_ROWS), np.float64)
    row_t = np.zeros((SAMPLE_ROWS,), np.float64)
    for p in range(N_PAIRS):
        row_slope[p, :L] = slopes[2 * p]
        row_slope[p, L:2 * L] = slopes[2 * p + 1]
    row_t[:L] = np.arange(L)
    row_t[L:2 * L] = np.arange(L)
    row_slope = jnp.asarray(row_slope, dtype=F32)[:, :, None]
    q_pos = jnp.asarray(past_len + row_t, dtype=F32)[None, :, None]
    key_pos = jnp.arange(past_len, dtype=F32)[None, None, :]
    past = -(row_slope * (q_pos - key_pos))
    new_pos = jnp.arange(SAMPLE_ROWS, dtype=F32)[None, None, :]
    dist_new = jnp.asarray(row_t, dtype=F32)[None, :, None] - new_pos
    valid = jnp.logical_and(dist_new >= 0, new_pos < L)
    new = jnp.where(valid, -(row_slope * dist_new), -jnp.inf)
    return past, new


def _moba_sample_kernel(pt_ref, q_ref, kn_ref, vn_ref, past_ref, new_ref, *refs, L, n_pages):
    k_pages, v_pages, o_ref = refs[:n_pages], refs[n_pages:2 * n_pages], refs[2 * n_pages]
    del pt_ref
    n_blk = n_pages * PAGE_SIZE // MOBA_BLOCK
    per_blk = MOBA_BLOCK // PAGE_SIZE
    row = lax.broadcasted_iota(jnp.int32, (SAMPLE_ROWS, PAIR), 0)
    lane = lax.broadcasted_iota(jnp.int32, (SAMPLE_ROWS, PAIR), 1)
    keep = jnp.logical_or(jnp.logical_and(row < L, lane < MOBA_DH),
                          jnp.logical_and(jnp.logical_and(row >= L, row < 2 * L), lane >= MOBA_DH))
    for p in range(N_PAIRS):
        cols = slice(p * PAIR, (p + 1) * PAIR)
        q = q_ref[0, :, cols]
        qm = jnp.where(keep, q, jnp.zeros_like(q))
        kT = jnp.concatenate([k_pages[s][0, 0, cols, :] for s in range(n_pages)], axis=1).astype(BF16)
        raw = jnp.dot(qm, kT, preferred_element_type=F32)
        blk_lanes = lambda jj: slice(jj * MOBA_BLOCK, (jj + 1) * MOBA_BLOCK)
        gate = [jnp.sum(raw[:, blk_lanes(jj)], axis=-1, keepdims=True) * (1.0 / MOBA_BLOCK) for jj in range(n_blk)]
        sel = []
        for jj in range(n_blk):
            ahead = jnp.zeros(gate[jj].shape, F32)
            for kk in range(n_blk):
                if kk == jj:
                    continue
                beats = (gate[kk] >= gate[jj]) if kk < jj else (gate[kk] > gate[jj])
                ahead = ahead + jnp.where(beats, 1.0, 0.0)
            sel.append(ahead < float(min(MOBA_TOPK, n_blk)))
        logit = jnp.concatenate(
            [jnp.where(sel[jj], raw[:, blk_lanes(jj)] + past_ref[p, :, blk_lanes(jj)], -jnp.inf)
             for jj in range(n_blk)], axis=1)
        kn = kn_ref[0, :, cols]
        s_new = lax.dot_general(qm, kn, NT, preferred_element_type=F32) + new_ref[p]
        m = jnp.maximum(jnp.max(s_new, axis=-1, keepdims=True), jnp.max(logit, axis=-1, keepdims=True))
        p_new = jnp.exp(s_new - m)
        pexp = jnp.exp(logit - m)
        denom = jnp.sum(p_new, axis=-1, keepdims=True) + jnp.sum(pexp, axis=-1, keepdims=True)
        vT = jnp.concatenate([v_pages[s][0, 0, cols, :] for s in range(n_pages)], axis=1).astype(BF16)
        acc = (jnp.dot(p_new.astype(BF16), vn_ref[0, :, cols], preferred_element_type=F32)
               + lax.dot_general(pexp.astype(BF16), vT, NT, preferred_element_type=F32))
        out = acc / denom
        o_ref[0, :, cols] = jnp.where(lane[:L] < MOBA_DH, out[:L], out[L:2 * L])


def _moba_sample(q16, kn16, vn16, cache_kT, cache_vT, page_table, layer, L):
    db, n_pages = page_table.shape
    past, new = _moba_sample_tables(L, n_pages)
    b3 = lambda b, pt: (b, 0, 0)
    row_spec = pl.BlockSpec((1, SAMPLE_ROWS, MOBA_WIDTH), b3)

    def page_spec(s):
        return pl.BlockSpec((1, 1, MOBA_WIDTH, PAGE_SIZE), lambda b, pt: (layer, pt[b, s], 0, 0))

    in_specs = [row_spec, row_spec, row_spec,
                pl.BlockSpec(past.shape, lambda b, pt: (0, 0, 0)), pl.BlockSpec(new.shape, lambda b, pt: (0, 0, 0))]
    in_specs += [page_spec(s) for s in range(n_pages)] * 2
    grid_spec = pltpu.PrefetchScalarGridSpec(
        num_scalar_prefetch=1, grid=(db,), in_specs=in_specs,
        out_specs=pl.BlockSpec((1, L, MOBA_WIDTH), b3))
    return pl.pallas_call(
        functools.partial(_moba_sample_kernel, L=L, n_pages=n_pages),
        grid_spec=grid_spec,
        out_shape=jax.ShapeDtypeStruct((db, L, MOBA_WIDTH), F32),
        compiler_params=_params("arbitrary"), name="moba_sample",
    )(page_table, q16, kn16, vn16, past, new, *([cache_kT] * n_pages), *([cache_vT] * n_pages))


def _route(x, wrh_ref, wrl_ref, br_ref, group=None):
    tm = x.shape[0]
    hi = x.astype(BF16)
    lo = (x - hi.astype(F32)).astype(BF16)
    logit = (jnp.dot(hi, wrh_ref[...], preferred_element_type=F32)
             + (jnp.dot(hi, wrl_ref[...], preferred_element_type=F32)
                + jnp.dot(lo, wrh_ref[...], preferred_element_type=F32))) + br_ref[...]
    lane = lax.broadcasted_iota(jnp.int32, (tm, ROUTER_LANES), 1)
    neg = -jnp.inf
    gl = jnp.where(lane < N_GROUPS, logit, neg)
    gmax = jnp.max(gl, axis=-1, keepdims=True)
    g_sum = jnp.sum(jnp.exp(gl - gmax), axis=-1, keepdims=True)
    if group is None:
        gidx = jnp.min(jnp.where(gl == gmax, lane, ROUTER_LANES), axis=-1, keepdims=True)
        g_w = 1.0 / g_sum
    else:
        gidx = group
        g_w = jnp.exp(jnp.sum(jnp.where(lane == group, logit, 0.0), axis=-1, keepdims=True) - gmax) / g_sum
    first = N_GROUPS + EXPERTS_PER_GROUP * gidx
    in_group = jnp.logical_and(lane >= first, lane < first + EXPERTS_PER_GROUP)
    el = jnp.where(in_group, logit, neg)
    e1 = jnp.max(el, axis=-1, keepdims=True)
    i1 = jnp.min(jnp.where(el == e1, lane, ROUTER_LANES), axis=-1, keepdims=True)
    el2 = jnp.where(lane == i1, neg, el)
    e2 = jnp.max(el2, axis=-1, keepdims=True)
    i2 = jnp.min(jnp.where(el2 == e2, lane, ROUTER_LANES), axis=-1, keepdims=True)
    t = jnp.exp(e2 - e1)
    return gidx, i1, i2, g_w / (1.0 + t), g_w * t / (1.0 + t), lane


def _expert_lanes(weights, first_lane, tm):
    return jnp.concatenate([jnp.broadcast_to(weights[:, first_lane + e:first_lane + e + 1], (tm, EXPERT_HIDDEN))
                            for e in range(EXPERTS_PER_GROUP)], axis=1)


def _group_experts(xb, cexp, weg, weu, wed):
    hg = jnp.dot(xb, weg, preferred_element_type=F32)
    hu = jnp.dot(xb, weu, preferred_element_type=F32)
    hid = (hg * _sigmoid(hg)) * hu
    return jnp.dot((hid * cexp).astype(BF16), wed, preferred_element_type=F32)


def _merge_kernel(x_ref, or_ref, om_ref, wg_ref, bg_ref, wr_ref, wm_ref, wo_ref, g_ref, b_ref, *refs, with_route):
    x = x_ref[...]
    gates = _sigmoid(jnp.dot(x.astype(BF16), wg_ref[...], preferred_element_type=F32) + bg_ref[...])
    br = jnp.dot(or_ref[...], wr_ref[...], preferred_element_type=F32)
    bm = jnp.dot(om_ref[...], wm_ref[...], preferred_element_type=F32)
    merged = gates[:, :D_MODEL] * br + gates[:, D_MODEL:] * bm
    y = jnp.dot(merged.astype(BF16), wo_ref[...], preferred_element_type=F32)
    x1 = _layer_norm_rows(ALPHA * x + y, g_ref[...], b_ref[...])
    if not with_route:
        refs[0][...] = x1
        return
    wrh_ref, wrl_ref, brt_ref, o_ref, group_ref = refs
    tm = x1.shape[0]
    for c in range(ROW_CHUNKS):
        o_ref[pl.ds(c, tm, stride=ROW_CHUNKS), :] = x1[:, c * LANES:(c + 1) * LANES]
    gidx = _route(x1, wrh_ref, wrl_ref, brt_ref)[0]
    group_ref[...] = jnp.broadcast_to(gidx, group_ref.shape)


def _merge(x, o_r, o_m, wg16, bg, wr16, wm16, wo16, g, b, router=None, *, tm):
    T = o_r.shape[0]
    row = lambda i: (i, 0)
    consts = [wg16, bg, wr16, wm16, wo16, g, b] + list(router or ())
    if router:
        out_shape = [jax.ShapeDtypeStruct((T * ROW_CHUNKS, LANES), F32),
                     jax.ShapeDtypeStruct((T, ROUTER_LANES), jnp.int32)]
        out_specs = [pl.BlockSpec((tm * ROW_CHUNKS, LANES), row), pl.BlockSpec((tm, ROUTER_LANES), row)]
    else:
        out_shape = [jax.ShapeDtypeStruct((T, D_MODEL), F32)]
        out_specs = [pl.BlockSpec((tm, D_MODEL), row)]
    outs = pl.pallas_call(
        functools.partial(_merge_kernel, with_route=bool(router)),
        grid=(T // tm,),
        in_specs=[pl.BlockSpec((tm, D_MODEL), row), pl.BlockSpec((tm, RET_WIDTH), row),
                  pl.BlockSpec((tm, MOBA_WIDTH), row)] + [_const_spec(c.shape) for c in consts],
        out_specs=out_specs, out_shape=out_shape,
        compiler_params=_params("arbitrary"), name="merge",
    )(x, o_r, o_m, *consts)
    return outs if router else outs[0]


def _moe_kernel(x_ref, wrh_ref, wrl_ref, br_ref, weg_ref, weu_ref, wed_ref, g_ref, b_ref, o_ref):
    x = x_ref[...]
    tm = x.shape[0]
    gidx, i1, i2, w1, w2, lane = _route(x, wrh_ref, wrl_ref, br_ref)
    comb = jnp.where(lane == i1, w1, 0.0) + jnp.where(lane == i2, w2, 0.0)
    xb = x.astype(BF16)
    acc = jnp.zeros((tm, D_MODEL), F32)
    for g in range(N_GROUPS):
        cexp = _expert_lanes(comb, N_GROUPS + g * EXPERTS_PER_GROUP, tm)
        acc = acc + _group_experts(xb, cexp, weg_ref[g], weu_ref[g], wed_ref[g])
    o_ref[...] = _layer_norm_rows(ALPHA * x + acc, g_ref[...], b_ref[...])


def _moe(x, wr_hi, wr_lo, br, weg16, weu16, wed16, g, b, *, tm):
    T = x.shape[0]
    row = lambda i: (i, 0)
    consts = [wr_hi, wr_lo, br, weg16, weu16, wed16, g, b]
    return pl.pallas_call(
        _moe_kernel,
        grid=(T // tm,),
        in_specs=[pl.BlockSpec((tm, D_MODEL), row)] + [_const_spec(c.shape) for c in consts],
        out_specs=pl.BlockSpec((tm, D_MODEL), row),
        out_shape=jax.ShapeDtypeStruct((T, D_MODEL), F32),
        compiler_params=_params("arbitrary"), name="moe",
    )(x, *consts)


def _experts_by_group_kernel(w_ref, o_ref, *, side_by_side):
    for e in range(EXPERTS_PER_GROUP):
        cols = slice(e * EXPERT_HIDDEN, (e + 1) * EXPERT_HIDDEN)
        if side_by_side:
            o_ref[0, :, cols] = w_ref[0, e].astype(BF16)
        else:
            o_ref[0, cols, :] = w_ref[0, e].astype(BF16)


def _experts_by_group(w, layer, *, side_by_side):
    return pl.pallas_call(
        functools.partial(_experts_by_group_kernel, side_by_side=side_by_side),
        grid=(N_GROUPS,),
        in_specs=[pl.BlockSpec((1, EXPERTS_PER_GROUP) + w.shape[2:], lambda g: (layer, g, 0, 0))],
        out_specs=pl.BlockSpec((1, D_MODEL, GROUP_HIDDEN) if side_by_side else (1, GROUP_HIDDEN, D_MODEL),
                               lambda g: (g, 0, 0)),
        out_shape=jax.ShapeDtypeStruct((N_GROUPS, D_MODEL, GROUP_HIDDEN) if side_by_side
                                       else (N_GROUPS, GROUP_HIDDEN, D_MODEL), BF16),
        compiler_params=_params("arbitrary"), name="experts_by_group",
    )(w)


def _group_plan(gid, tm):
    T = gid.shape[0]
    groups = jnp.arange(N_GROUPS, dtype=jnp.int32)
    member = (gid[:, None] == groups[None, :]).astype(jnp.int32)
    running = jnp.cumsum(member, axis=0)
    counts = running[-1]
    rank = jnp.sum(running * member, axis=1) - 1
    padded = ((counts + tm - 1) // tm) * tm
    ends = jnp.cumsum(padded)
    starts = ends - padded
    pos = jnp.sum(starts[None, :] * member, axis=1) + rank
    n_tiles = T // tm + N_GROUPS
    rows = jnp.arange(T, dtype=jnp.int32)
    pad_dst = T + jnp.arange(tm, dtype=jnp.int32)
    src = jnp.zeros((n_tiles * tm,), jnp.int32).at[pos].set(rows, unique_indices=True)
    tile_start = jnp.arange(n_tiles, dtype=jnp.int32) * tm
    tile_group = jnp.minimum(jnp.sum((tile_start[:, None] >= ends[None, :]).astype(jnp.int32), axis=1), N_GROUPS - 1)
    in_tile = (tile_group[:, None] == groups[None, :]).astype(jnp.int32)
    fill = jnp.sum(in_tile * (starts + counts)[None, :], axis=1)[:, None] - tile_start[:, None]
    real = (jnp.arange(tm, dtype=jnp.int32)[None, :] < fill).reshape(n_tiles * tm)
    dst = jnp.where(real, src, jnp.tile(pad_dst, n_tiles))
    dst = jnp.concatenate([pad_dst, dst]).reshape(n_tiles + 1, 1, tm)
    return src.reshape(n_tiles, 1, tm), dst, tile_group


def _moe_grouped_kernel(tg_ref, src_now_ref, src_next_ref, dst_prev_ref, dst_now_ref,
                        x_hbm, wrh_ref, wrl_ref, br_ref, weg_ref, weu_ref, wed_ref, g_ref, b_ref, out_hbm,
                        xbuf, obuf, gsem, ssem, *, tm):
    i = pl.program_id(0)
    last = pl.num_programs(0) - 1
    slot = lax.rem(i, 2)
    other = 1 - slot

    def fetch_rows(idx_ref, s, rows=range(tm)):
        for r in rows:
            t = idx_ref[0, 0, r]
            pltpu.make_async_copy(x_hbm.at[pl.ds(pl.multiple_of(t * ROW_CHUNKS, ROW_CHUNKS), ROW_CHUNKS)],
                                  xbuf.at[s, pl.ds(r * ROW_CHUNKS, ROW_CHUNKS)], gsem.at[s]).start(priority=r % 2)

    def fetch_wait(s):
        pltpu.make_async_copy(x_hbm.at[pl.ds(0, tm * ROW_CHUNKS)], xbuf.at[s], gsem.at[s]).wait()

    def write_rows(idx_ref, s, rows=range(tm)):
        for r in rows:
            t = idx_ref[0, 0, r]
            pltpu.make_async_copy(obuf.at[s, pl.ds(r, 1)], out_hbm.at[pl.ds(t, 1)], ssem.at[s]).start(priority=r % 2)

    def write_wait(s):
        pltpu.make_async_copy(obuf.at[s], out_hbm.at[pl.ds(0, tm)], ssem.at[s]).wait()

    @pl.when(i == 0)
    def _():
        obuf[...] = jnp.zeros_like(obuf)
        fetch_rows(src_now_ref, 0)
        fetch_wait(0)

    fetch_rows(src_next_ref, other)
    write_rows(dst_prev_ref, other)
    x = jnp.concatenate([xbuf[slot, pl.ds(c, tm, stride=ROW_CHUNKS), :] for c in range(ROW_CHUNKS)], axis=1)
    group = tg_ref[i]
    _, i1, i2, w1, w2, lane = _route(x, wrh_ref, wrl_ref, br_ref, group=group)
    first = N_GROUPS + EXPERTS_PER_GROUP * group
    cexp = _expert_lanes(jnp.where(lane == i1 - first, w1, 0.0) + jnp.where(lane == i2 - first, w2, 0.0), 0, tm)
    acc = _group_experts(x.astype(BF16), cexp, weg_ref[0], weu_ref[0], wed_ref[0])
    obuf[slot] = _layer_norm_rows(ALPHA * x + acc, g_ref[...], b_ref[...])
    fetch_wait(other)
    write_wait(other)

    @pl.when(i == last)
    def _():
        write_rows(dst_now_ref, slot)
        write_wait(slot)


def _moe_grouped(x, T, src, dst, tile_group, wr_hi, wr_lo, br, weg16, weu16, wed16, g, b, *, tm):
    n_tiles = src.shape[0]
    smem_tile = lambda fn: pl.BlockSpec((1, 1, tm), fn, memory_space=pltpu.SMEM)
    by_group = lambda i, tg: (tg[i], 0, 0)
    w_spec = pl.BlockSpec((1, D_MODEL, GROUP_HIDDEN), by_group)
    fixed = lambda t: pl.BlockSpec(t.shape, lambda i, tg: (0, 0))
    grid_spec = pltpu.PrefetchScalarGridSpec(
        num_scalar_prefetch=1, grid=(n_tiles,),
        in_specs=[smem_tile(lambda i, tg: (i, 0, 0)),
                  smem_tile(lambda i, tg: (jnp.minimum(i + 1, n_tiles - 1), 0, 0)),
                  smem_tile(lambda i, tg: (i, 0, 0)), smem_tile(lambda i, tg: (i + 1, 0, 0)),
                  pl.BlockSpec(memory_space=pl.ANY), fixed(wr_hi), fixed(wr_lo), fixed(br),
                  w_spec, w_spec, pl.BlockSpec((1, GROUP_HIDDEN, D_MODEL), by_group), fixed(g), fixed(b)],
        out_specs=pl.BlockSpec(memory_space=pl.ANY),
        scratch_shapes=[pltpu.VMEM((2, tm * ROW_CHUNKS, LANES), F32), pltpu.VMEM((2, tm, D_MODEL), F32),
                        pltpu.SemaphoreType.DMA((2,)), pltpu.SemaphoreType.DMA((2,))])
    return pl.pallas_call(
        functools.partial(_moe_grouped_kernel, tm=tm),
        grid_spec=grid_spec,
        out_shape=jax.ShapeDtypeStruct((T + tm, D_MODEL), F32),
        compiler_params=_params("arbitrary"), name="moe_grouped",
    )(tile_group, src, src, dst, dst, x, wr_hi, wr_lo, br, weg16, weu16, wed16, g, b)


def _layer_weights(l, w_in, b_merge, gn_g, w_br_ret, w_br_moba, w_out, ln1_g, ln1_b, w_router_group,
                   b_router_group, w_router_expert, b_router_expert, w_exp_gate, w_exp_up, w_exp_down,
                   ln2_g, ln2_b):
    w = w_in[l]
    q0 = 4 * RET_WIDTH
    n_exp = N_GROUPS * EXPERTS_PER_GROUP
    w_r = jnp.concatenate(
        [w_router_group[l], w_router_expert[l].transpose(1, 0, 2).reshape(D_MODEL, n_exp),
         jnp.zeros((D_MODEL, ROUTER_LANES - N_GROUPS - n_exp), F32)], axis=1)
    w_r_hi = w_r.astype(BF16)
    w_r_lo = (w_r - w_r_hi.astype(F32)).astype(BF16)
    b_r = jnp.concatenate([b_router_group[l], b_router_expert[l].reshape(n_exp),
                           jnp.zeros((ROUTER_LANES - N_GROUPS - n_exp,), F32)])[None, :]
    return dict(
        w_proj=w[:, :N_PROJ].astype(BF16),
        w_ret=w[:, :q0].astype(BF16),
        w_mobaT=w[:, q0:N_PROJ].T.reshape(3, MOBA_WIDTH, D_MODEL).astype(BF16),
        w_gate=w[:, N_PROJ:].astype(BF16), b_gate=b_merge[l][None, :],
        gn=gn_g[l][None, :],
        w_br_ret=w_br_ret[l].astype(BF16), w_br_moba=w_br_moba[l].astype(BF16), w_out=w_out[l].astype(BF16),
        ln1_g=ln1_g[l][None, :], ln1_b=ln1_b[l][None, :],
        w_r_hi=w_r_hi, w_r_lo=w_r_lo, b_r=b_r,
        w_eg=_experts_by_group(w_exp_gate, l, side_by_side=True),
        w_eu=_experts_by_group(w_exp_up, l, side_by_side=True),
        w_ed=_experts_by_group(w_exp_down, l, side_by_side=False),
        ln2_g=ln2_g[l][None, :], ln2_b=ln2_b[l][None, :])


def _pad_rows(t, rows):
    return jnp.pad(t, ((0, 0), (0, rows - t.shape[1]), (0, 0)))


def kernel(x_prompt, x_sample, cache_k, cache_v, state_ret, page_table, w_in, b_merge, gn_g, w_br_ret, w_br_moba, w_out, ln1_g, ln1_b, w_router_group, b_router_group, w_router_expert, b_router_expert, w_exp_gate, w_exp_up, w_exp_down, ln2_g, ln2_b):
    B, S, _ = x_prompt.shape
    DB, L, _ = x_sample.shape
    Tp, Ts = B * S, DB * L
    depth = w_in.shape[0]
    n_pool = cache_k.shape[1]
    page_major = lambda c: c.transpose(0, 1, 3, 4, 2).reshape(depth, n_pool, MOBA_WIDTH, PAGE_SIZE)
    cache_kT, cache_vT = page_major(cache_k), page_major(cache_v)
    xp = x_prompt.reshape(Tp, D_MODEL)
    xs = x_sample.reshape(Ts, D_MODEL)
    tm_p = min(512, Tp)
    tm_s = min(256, Ts)
    tm_moe = 256
    outs = [[] for _ in range(4)]
    kv_all = None
    for l in range(depth):
        W = _layer_weights(l, w_in, b_merge, gn_g, w_br_ret, w_br_moba, w_out, ln1_g, ln1_b, w_router_group,
                           b_router_group, w_router_expert, b_router_expert, w_exp_gate, w_exp_up, w_exp_down,
                           ln2_g, ln2_b)
        (rq, rk, rv, rg, kT_p, vT_p, k16, qT16, vT16, kmean) = _in_proj(
            xp, W["w_ret"], W["w_mobaT"], rows=Tp, batch=B, tm=tm_p, layer=l, depth=depth, kv_all=kv_all)
        kv_all = (kT_p, vT_p)
        o_r, s_p = _ret_prompt(rq, rk, rv, rg, W["gn"], B)
        o_m = _moba_prompt(qT16, k16, vT16, kmean, B)
        router = (W["w_r_hi"], W["w_r_lo"], W["b_r"])
        x1, group = _merge(xp, o_r, o_m, W["w_gate"], W["b_gate"], W["w_br_ret"], W["w_br_moba"], W["w_out"],
                           W["ln1_g"], W["ln1_b"], router=router, tm=tm_p)
        plan = _group_plan(group[:, 0], tm_moe)
        xp = _moe_grouped(x1, Tp, *plan, *router, W["w_eg"], W["w_eu"], W["w_ed"], W["ln2_g"], W["ln2_b"], tm=tm_moe)
        (rq, rk, rv, rg, mq, k_s, v_s) = _in_proj(xs, W["w_proj"], tm=tm_s)
        r3 = lambda t: t.reshape(DB, L, t.shape[-1])
        o_r, s_s = _ret_sample(_pad_rows(r3(rq), SAMPLE_ROWS), _pad_rows(r3(rk), SAMPLE_ROWS),
                               _pad_rows(r3(rv), SAMPLE_ROWS), r3(rg), W["gn"], state_ret, l)
        mq3 = r3(mq)
        q16 = _pad_rows(jnp.concatenate([mq3, mq3], axis=1), SAMPLE_ROWS)
        o_m = _moba_sample(q16, _pad_rows(r3(k_s).astype(BF16), SAMPLE_ROWS),
                           _pad_rows(r3(v_s).astype(BF16), SAMPLE_ROWS),
                           cache_kT, cache_vT, page_table, l, L)
        x1 = _merge(xs, o_r.reshape(Ts, RET_WIDTH).astype(BF16), o_m.reshape(Ts, MOBA_WIDTH).astype(BF16),
                    W["w_gate"], W["b_gate"], W["w_br_ret"], W["w_br_moba"], W["w_out"],
                    W["ln1_g"], W["ln1_b"], tm=tm_s)
        xs = _moe(x1, W["w_r_hi"], W["w_r_lo"], W["b_r"], W["w_eg"], W["w_eu"], W["w_ed"],
                  W["ln2_g"], W["ln2_b"], tm=tm_s)
        for lst, val in zip(outs, (s_p, k_s.reshape(DB, L, MOBA_HEADS, MOBA_DH),
                                   v_s.reshape(DB, L, MOBA_HEADS, MOBA_DH), s_s)):
            lst.append(val)
    sp, ksm, vsm, ssm = (jnp.stack(o) for o in outs)
    kTp, vTp = kv_all
    token_major = lambda t: t.reshape(depth, B, MOBA_HEADS, MOBA_DH, S).transpose(0, 1, 4, 2, 3)
    return (xp[:Tp].reshape(B, S, D_MODEL), xs.reshape(DB, L, D_MODEL), token_major(kTp), token_major(vTp), sp,
            ksm, vsm, ssm)
```

```python
import functools

import numpy as np
import jax
import jax.numpy as jnp
from jax import lax
from jax.experimental import pallas as pl
from jax.experimental.pallas import tpu as pltpu

F32 = jnp.float32
BF16 = jnp.bfloat16

D_MODEL = 1024
RET_HEADS = 4
RET_DK = 128
RET_WIDTH = RET_HEADS * RET_DK
RET_CHUNK = 128
MOBA_HEADS = 8
MOBA_DH = 64
MOBA_WIDTH = MOBA_HEADS * MOBA_DH
MOBA_BLOCK = 256
MOBA_TOPK = 3
PAGE_SIZE = 128
N_GROUPS = 4
EXPERTS_PER_GROUP = 8
EXPERT_HIDDEN = 128
GROUP_HIDDEN = EXPERTS_PER_GROUP * EXPERT_HIDDEN
DEPTH = 2
ALPHA = (2 * DEPTH) ** 0.25
LN_EPS = 1e-5
N_PROJ = 4 * RET_WIDTH + 3 * MOBA_WIDTH

LANES = 128
ROW_CHUNKS = D_MODEL // LANES
PAIR = 2 * MOBA_DH
N_PAIRS = MOBA_HEADS // 2
GROUP_HEADS = 4
DENOM_ROWS = 16
LOG2E = 1.4426950408889634
SAMPLE_ROWS = 16
ROUTER_LANES = 128
VMEM_LIMIT = 56 * 1024 * 1024

NT = (((1,), (1,)), ((), ()))
TN = (((0,), (0,)), ((), ()))


def _params(*sem):
    return pltpu.CompilerParams(dimension_semantics=sem, vmem_limit_bytes=VMEM_LIMIT)


def _const_spec(shape):
    nd = len(shape)
    return pl.BlockSpec(shape, lambda *_: (0,) * nd, pipeline_mode=pl.Buffered(1))


def _layer_norm_rows(z, g, b):
    mu = jnp.mean(z, axis=-1, keepdims=True)
    zc = z - mu
    var = jnp.mean(zc * zc, axis=-1, keepdims=True)
    return zc * lax.rsqrt(var + LN_EPS) * g + b


def _sigmoid(x):
    return 1.0 / (1.0 + jnp.exp(-x))


def _in_proj_kernel(x_ref, w_ref, *refs, transposed, tm, carried):
    xb = x_ref[...].astype(BF16)

    def proj(c):
        return jnp.dot(xb, w_ref[:, c * RET_WIDTH:(c + 1) * RET_WIDTH], preferred_element_type=F32)

    if transposed:
        wT_ref = refs[0]
        outs = refs[3:] if carried else refs[1:]
        rq_ref, rk_ref, rv_ref, rg_ref, kT_ref, vT_ref, k16_ref, qT16_ref, vT16_ref, kmean_ref = outs
        projT = lambda c: lax.dot_general(wT_ref[c], xb, NT, preferred_element_type=F32)
        qT16_ref[...] = (projT(0) * MOBA_DH ** -0.5).astype(BF16)
        kT = projT(1)
        kT_ref[0, 0] = kT
        k = kT.T
        k16_ref[...] = k.astype(BF16)
        nb = tm // MOBA_BLOCK
        kmean_ref[0] = jnp.mean(k.reshape(nb, MOBA_BLOCK, MOBA_WIDTH), axis=1)
        vT = projT(2)
        vT_ref[0, 0] = vT
        vT16_ref[...] = vT.astype(BF16)
        for later in range(1, kT_ref.shape[0]):
            kT_ref[later, 0] = jnp.zeros_like(kT)
            vT_ref[later, 0] = jnp.zeros_like(vT)
    else:
        rq_ref, rk_ref, rv_ref, rg_ref, mq_ref, k_ref, v_ref = refs
        mq_ref[...] = (proj(4) * MOBA_DH ** -0.5).astype(BF16)
        k_ref[...] = proj(5)
        v_ref[...] = proj(6)
    rq_ref[...] = proj(0).astype(BF16)
    rk_ref[...] = (proj(1) * RET_DK ** -0.5).astype(BF16)
    rv_ref[...] = proj(2).astype(BF16)
    rg_ref[...] = proj(3)


def _in_proj(x, w16, wT16=None, *, rows=None, batch=1, tm, layer=0, depth=1, kv_all=None):
    T = rows or x.shape[0]
    transposed = wT16 is not None
    aliases = {}
    row = lambda i: (i, 0)
    tile = lambda: pl.BlockSpec((tm, RET_WIDTH), row)
    in_specs = [pl.BlockSpec((tm, D_MODEL), row), _const_spec(w16.shape)]
    args = [x, w16]
    out_shape = [jax.ShapeDtypeStruct((T, RET_WIDTH), BF16)] * 3 + [jax.ShapeDtypeStruct((T, RET_WIDTH), F32)]
    out_specs = [tile() for _ in range(4)]
    if transposed:
        S = T // batch
        per_b = S // tm
        in_specs.append(_const_spec(wT16.shape))
        args.append(wT16)
        col = lambda i: (0, i)
        if kv_all is None:
            kv_spec = pl.BlockSpec((depth, 1, MOBA_WIDTH, tm), lambda i: (0, i // per_b, 0, i % per_b))
        else:
            kv_spec = pl.BlockSpec((1, 1, MOBA_WIDTH, tm), lambda i: (layer, i // per_b, 0, i % per_b))
            aliases = {len(args): len(out_shape), len(args) + 1: len(out_shape) + 1}
            in_specs += [pl.BlockSpec(memory_space=pl.ANY)] * 2
            args += list(kv_all)
        out_shape += [jax.ShapeDtypeStruct((depth, batch, MOBA_WIDTH, S), F32)] * 2 + [
            jax.ShapeDtypeStruct((T, MOBA_WIDTH), BF16),
            jax.ShapeDtypeStruct((MOBA_WIDTH, T), BF16), jax.ShapeDtypeStruct((MOBA_WIDTH, T), BF16),
            jax.ShapeDtypeStruct((T // tm, tm // MOBA_BLOCK, MOBA_WIDTH), F32)]
        out_specs += [kv_spec, kv_spec,
                      tile(), pl.BlockSpec((MOBA_WIDTH, tm), col), pl.BlockSpec((MOBA_WIDTH, tm), col),
                      pl.BlockSpec((1, tm // MOBA_BLOCK, MOBA_WIDTH), lambda i: (i, 0, 0))]
    else:
        out_shape += [jax.ShapeDtypeStruct((T, MOBA_WIDTH), BF16)] + [jax.ShapeDtypeStruct((T, MOBA_WIDTH), F32)] * 2
        out_specs += [tile() for _ in range(3)]
    outs = pl.pallas_call(
        functools.partial(_in_proj_kernel, transposed=transposed, tm=tm, carried=bool(aliases)),
        grid=(T // tm,), in_specs=in_specs, out_specs=out_specs, out_shape=out_shape,
        input_output_aliases=aliases,
        compiler_params=_params("arbitrary"), name="in_proj",
    )(*args)
    if transposed:
        outs = list(outs)
        outs[-1] = outs[-1].reshape(T // MOBA_BLOCK, MOBA_WIDTH)
    return outs


def _ret_tables(L, rows):
    log_g = jnp.log(jnp.asarray(1.0 - 2.0 ** (-5.0 - np.arange(RET_HEADS)), dtype=F32))
    idx = jnp.arange(L, dtype=F32)
    diff = idx[:, None] - idx[None, :]
    decay = jnp.where(diff >= 0, jnp.exp(log_g[:, None, None] * jnp.maximum(diff, 0.0)), 0.0)
    qdec = jnp.exp(log_g[:, None] * (idx + 1.0))
    kdec = jnp.exp(log_g[:, None] * (L - 1.0 - idx))
    g_chunk = jnp.exp(log_g * L)
    pad = rows - L
    decay = jnp.pad(decay, ((0, 0), (0, pad), (0, pad)))
    lanes = lambda t: jnp.broadcast_to(jnp.pad(t, ((0, 0), (0, pad)))[:, :, None], (RET_HEADS, rows, LANES))
    return decay, lanes(qdec), lanes(kdec), jnp.broadcast_to(g_chunk[:, None, None], (RET_HEADS, 1, LANES))


def _ret_head(q, k, v, state, decay, qdec, kdec, g_chunk):
    scores = lax.dot_general(q, k, NT, preferred_element_type=F32) * decay
    inner = jnp.dot(scores.astype(BF16), v, preferred_element_type=F32)
    q_dec = (q.astype(F32) * qdec).astype(BF16)
    cross = jnp.dot(q_dec, state.astype(BF16), preferred_element_type=F32)
    k_dec = (k.astype(F32) * kdec).astype(BF16)
    new_state = state * g_chunk + lax.dot_general(k_dec, v, TN, preferred_element_type=F32)
    return inner + cross, new_state


def _ret_gate(o, rg, gn):
    mu = jnp.mean(o, axis=-1, keepdims=True)
    oc = o - mu
    var = jnp.mean(oc * oc, axis=-1, keepdims=True)
    return (rg * _sigmoid(rg)) * (oc * lax.rsqrt(var + LN_EPS) * gn)


def _ret_prompt_kernel(q_ref, k_ref, v_ref, rg_ref, gn_ref, decay_ref, qdec_ref, kdec_ref, gc_ref,
                       o_ref, state_ref):
    @pl.when(pl.program_id(0) == 0)
    def _():
        state_ref[...] = jnp.zeros_like(state_ref)

    for b in range(q_ref.shape[0]):
        for h in range(RET_HEADS):
            sl = slice(h * RET_DK, (h + 1) * RET_DK)
            o, new_state = _ret_head(q_ref[b, :, sl], k_ref[b, :, sl], v_ref[b, :, sl], state_ref[b, h],
                                     decay_ref[h], qdec_ref[h], kdec_ref[h], gc_ref[h])
            state_ref[b, h] = new_state
            o_ref[b, :, sl] = _ret_gate(o, rg_ref[b, :, sl], gn_ref[:, sl]).astype(BF16)


def _ret_prompt(rq, rk, rv, rg, gn, batch):
    T = rq.shape[0]
    S = T // batch
    tables = _ret_tables(RET_CHUNK, RET_CHUNK)
    by_batch = lambda t: t.reshape(batch, S, RET_WIDTH)
    tile = pl.BlockSpec((batch, RET_CHUNK, RET_WIDTH), lambda c: (0, c, 0))
    o_r, state = pl.pallas_call(
        _ret_prompt_kernel,
        grid=(S // RET_CHUNK,),
        in_specs=[tile, tile, tile, tile, _const_spec(gn.shape)] + [_const_spec(t.shape) for t in tables],
        out_specs=[tile, pl.BlockSpec((batch, RET_HEADS, RET_DK, RET_DK), lambda c: (0, 0, 0, 0))],
        out_shape=[jax.ShapeDtypeStruct((batch, S, RET_WIDTH), BF16),
                   jax.ShapeDtypeStruct((batch, RET_HEADS, RET_DK, RET_DK), F32)],
        compiler_params=_params("arbitrary"), name="ret_prompt",
    )(by_batch(rq), by_batch(rk), by_batch(rv), by_batch(rg), gn, *tables)
    return o_r.reshape(T, RET_WIDTH), state


def _ret_sample_kernel(q_ref, k_ref, v_ref, rg_ref, gn_ref, s_ref, decay_ref, qdec_ref, kdec_ref, gc_ref,
                       *refs, bt, L):
    o_ref, snew_ref = refs[-2:]
    for bi in range(bt):
        for h in range(RET_HEADS):
            sl = slice(h * RET_DK, (h + 1) * RET_DK)
            o, new_state = _ret_head(q_ref[bi, :, sl], k_ref[bi, :, sl], v_ref[bi, :, sl], s_ref[0, bi, h],
                                     decay_ref[h], qdec_ref[h], kdec_ref[h], gc_ref[h])
            snew_ref[0, bi, h] = new_state
            o_ref[bi, :, sl] = _ret_gate(o[:L], rg_ref[bi, :, sl], gn_ref[:, sl])
    for later in range(1, snew_ref.shape[0]):
        snew_ref[later] = jnp.zeros(snew_ref.shape[1:], F32)


def _ret_sample(rq, rk, rv, rg, gn, state, layer, new_all=None, *, bt=8):
    db, L = rg.shape[0], rg.shape[1]
    depth = state.shape[0]
    tables = _ret_tables(L, SAMPLE_ROWS)
    b3 = lambda i: (i, 0, 0)
    qkv = pl.BlockSpec((bt, SAMPLE_ROWS, RET_WIDTH), b3)
    st_in = pl.BlockSpec((1, bt, RET_HEADS, RET_DK, RET_DK), lambda i: (layer, i, 0, 0, 0))
    in_specs = [qkv, qkv, qkv, pl.BlockSpec((bt, L, RET_WIDTH), b3), _const_spec(gn.shape), st_in]
    in_specs += [_const_spec(t.shape) for t in tables]
    args = [rq, rk, rv, rg, gn, state, *tables]
    if new_all is None:
        st_out = pl.BlockSpec((depth, bt, RET_HEADS, RET_DK, RET_DK), lambda i: (0, i, 0, 0, 0))
        aliases = {}
    else:
        st_out = st_in
        aliases = {len(args): 1}
        in_specs.append(pl.BlockSpec(memory_space=pl.ANY))
        args.append(new_all)
    return pl.pallas_call(
        functools.partial(_ret_sample_kernel, bt=bt, L=L),
        grid=(db // bt,),
        in_specs=in_specs,
        out_specs=[pl.BlockSpec((bt, L, RET_WIDTH), b3), st_out],
        out_shape=[jax.ShapeDtypeStruct((db, L, RET_WIDTH), F32),
                   jax.ShapeDtypeStruct((depth, db, RET_HEADS, RET_DK, RET_DK), F32)],
        input_output_aliases=aliases,
        compiler_params=_params("arbitrary"), name="ret_sample",
    )(*args)


def _alibi_slopes():
    return 2.0 ** (-8.0 * np.arange(1, MOBA_HEADS + 1) / MOBA_HEADS)


def _moba_prompt_tables():
    slopes = jnp.asarray(_alibi_slopes(), dtype=F32)[:, None, None]
    kk = jnp.arange(MOBA_BLOCK, dtype=F32)[:, None]
    qq = jnp.arange(MOBA_BLOCK, dtype=F32)[None, :]
    dist = (qq - kk)[None]
    past = -(slopes * dist) * LOG2E
    own = jnp.where(dist >= 0, past, -jnp.inf)
    block_step = -(slopes * float(MOBA_BLOCK)) * LOG2E
    return past, own, jnp.broadcast_to(block_step, (MOBA_HEADS, 1, MOBA_BLOCK))


def _for_blocks(n, body):
    def four(i, carry):
        for u in range(4):
            body(4 * i + u)
        return carry

    lax.fori_loop(0, lax.shift_right_logical(n, 2), four, 0)
    done = lax.bitwise_and(n, -4)

    @pl.when(lax.bitwise_and(n, 2) == 2)
    def _():
        body(done)
        body(done + 1)

    @pl.when(lax.bitwise_and(n, 1) == 1)
    def _():
        body(n - 1)


def _moba_prompt_kernel(qT_ref, k_ref, vT_ref, kmean_ref, past_ref, own_ref, step_ref, o_ref,
                        q_sc, s_sc, m_sc, acc_sc, term_sc):
    j = pl.program_id(2)
    nb = kmean_ref.shape[0]
    dh_row = lax.broadcasted_iota(jnp.int32, (PAIR, MOBA_BLOCK), 0)
    blk = lax.broadcasted_iota(jnp.int32, (nb, MOBA_BLOCK), 0)
    own_start = pl.multiple_of(j * MOBA_BLOCK, MOBA_BLOCK)
    pair_cols = lambda h: slice((h // 2) * PAIR, (h // 2 + 1) * PAIR)

    for h in range(GROUP_HEADS):
        qT = qT_ref[pair_cols(h), :]
        keep = (dh_row < MOBA_DH) if h % 2 == 0 else (dh_row >= MOBA_DH)
        qh = jnp.where(keep, qT, jnp.zeros_like(qT))
        q_sc[h] = qh
        gate = jnp.dot(kmean_ref[:, pair_cols(h)].astype(BF16), qh, preferred_element_type=F32)
        gate = jnp.where(blk < j, gate, -jnp.inf)
        sel = jnp.zeros(gate.shape, dtype=jnp.bool_)
        for _ in range(MOBA_TOPK):
            top = jnp.max(gate, axis=0, keepdims=True)
            first = jnp.min(jnp.where(gate == top, blk, nb), axis=0, keepdims=True)
            pick = jnp.logical_and(blk == first, top > -jnp.inf)
            sel = jnp.logical_or(sel, pick)
            gate = jnp.where(pick, -jnp.inf, gate)
        term_sc[h] = jnp.where(sel, (j - blk).astype(F32) * step_ref[h], jnp.where(blk == j, 0.0, -jnp.inf))
        s = jnp.dot(k_ref[pl.ds(own_start, MOBA_BLOCK), pair_cols(h)], qh, preferred_element_type=F32)
        s = s * LOG2E + own_ref[h]
        s_sc[h, j] = s
        m_sc[h] = jnp.max(s, axis=0, keepdims=True)
        acc_sc[h] = jnp.zeros_like(acc_sc[h])

    def scores(jj):
        start = pl.multiple_of(jj * MOBA_BLOCK, MOBA_BLOCK)
        for h in range(GROUP_HEADS):
            s = jnp.dot(k_ref[pl.ds(start, MOBA_BLOCK), pair_cols(h)], q_sc[h], preferred_element_type=F32)
            s = s * LOG2E + past_ref[h]
            s_sc[h, jj] = s
            m_sc[h] = jnp.maximum(m_sc[h], jnp.max(s, axis=0, keepdims=True) + term_sc[h, pl.ds(jj, 1), :])

    _for_blocks(j, scores)

    ones_rows = jnp.ones((DENOM_ROWS, MOBA_BLOCK), BF16)

    def apply_v(jj):
        start = pl.multiple_of(jj * MOBA_BLOCK, MOBA_BLOCK)
        for h in range(GROUP_HEADS):
            pexp = jnp.exp2((s_sc[h, jj] - (m_sc[h] - term_sc[h, pl.ds(jj, 1), :])).astype(BF16))
            vT = jnp.concatenate([vT_ref[h * MOBA_DH:(h + 1) * MOBA_DH, pl.ds(start, MOBA_BLOCK)], ones_rows], axis=0)
            acc_sc[h] = acc_sc[h] + jnp.dot(vT, pexp, preferred_element_type=F32)

    _for_blocks(j + 1, apply_v)

    def head_out(h):
        acc = acc_sc[h]
        return acc[:MOBA_DH] / acc[MOBA_DH:MOBA_DH + 1]

    for p in range(GROUP_HEADS // 2):
        outT = jnp.concatenate([head_out(2 * p), head_out(2 * p + 1)], axis=0)
        o_ref[:, p * PAIR:(p + 1) * PAIR] = outT.T.astype(BF16)


def _moba_prompt(qT16, k16, vT16, kmean, batch):
    T = k16.shape[0]
    S = T // batch
    nb = S // MOBA_BLOCK
    n_groups = MOBA_HEADS // GROUP_HEADS
    gw = GROUP_HEADS * MOBA_DH
    tables = _moba_prompt_tables()
    head_tile = lambda t: pl.BlockSpec((GROUP_HEADS,) + t.shape[1:], lambda b, g, j: (g, 0, 0))
    return pl.pallas_call(
        _moba_prompt_kernel,
        grid=(batch, n_groups, nb),
        in_specs=[pl.BlockSpec((gw, MOBA_BLOCK), lambda b, g, j: (g, b * nb + j)),
                  pl.BlockSpec((S, gw), lambda b, g, j: (b, g)),
                  pl.BlockSpec((gw, S), lambda b, g, j: (g, b)),
                  pl.BlockSpec((nb, gw), lambda b, g, j: (b, g))]
        + [head_tile(t) for t in tables],
        out_specs=pl.BlockSpec((MOBA_BLOCK, gw), lambda b, g, j: (b * nb + j, g)),
        out_shape=jax.ShapeDtypeStruct((T, MOBA_WIDTH), BF16),
        scratch_shapes=[pltpu.VMEM((GROUP_HEADS, PAIR, MOBA_BLOCK), BF16),
                        pltpu.VMEM((GROUP_HEADS, nb, MOBA_BLOCK, MOBA_BLOCK), F32),
                        pltpu.VMEM((GROUP_HEADS, 1, MOBA_BLOCK), F32),
                        pltpu.VMEM((GROUP_HEADS, MOBA_DH + DENOM_ROWS, MOBA_BLOCK), F32),
                        pltpu.VMEM((GROUP_HEADS, nb, MOBA_BLOCK), F32)],
        compiler_params=_params("arbitrary", "arbitrary", "arbitrary"), name="moba_prompt",
    )(qT16, k16, vT16, kmean, *tables)


def _moba_sample_tables(L, n_pages):
    past_len = n_pages * PAGE_SIZE
    slopes = _alibi_slopes()
    row_slope = np.zeros((N_PAIRS, SAMPLE_ROWS), np.float64)
    row_t = np.zeros((SAMPLE_ROWS,), np.float64)
    for p in range(N_PAIRS):
        row_slope[p, :L] = slopes[2 * p]
        row_slope[p, L:2 * L] = slopes[2 * p + 1]
    row_t[:L] = np.arange(L)
    row_t[L:2 * L] = np.arange(L)
    row_slope = jnp.asarray(row_slope, dtype=F32)[:, :, None]
    q_pos = jnp.asarray(past_len + row_t, dtype=F32)[None, :, None]
    key_pos = jnp.arange(past_len, dtype=F32)[None, None, :]
    past = -(row_slope * (q_pos - key_pos))
    new_pos = jnp.arange(SAMPLE_ROWS, dtype=F32)[None, None, :]
    dist_new = jnp.asarray(row_t, dtype=F32)[None, :, None] - new_pos
    valid = jnp.logical_and(dist_new >= 0, new_pos < L)
    new = jnp.where(valid, -(row_slope * dist_new), -jnp.inf)
    return past, new


def _moba_sample_kernel(pt_ref, q_ref, kn_ref, vn_ref, past_ref, new_ref, *refs, L, n_pages):
    k_pages, v_pages, o_ref = refs[:n_pages], refs[n_pages:2 * n_pages], refs[2 * n_pages]
    del pt_ref
    n_blk = n_pages * PAGE_SIZE // MOBA_BLOCK
    per_blk = MOBA_BLOCK // PAGE_SIZE
    row = lax.broadcasted_iota(jnp.int32, (SAMPLE_ROWS, PAIR), 0)
    lane = lax.broadcasted_iota(jnp.int32, (SAMPLE_ROWS, PAIR), 1)
    keep = jnp.logical_or(jnp.logical_and(row < L, lane < MOBA_DH),
                          jnp.logical_and(jnp.logical_and(row >= L, row < 2 * L), lane >= MOBA_DH))
    for p in range(N_PAIRS):
        cols = slice(p * PAIR, (p + 1) * PAIR)
        q = q_ref[0, :, cols]
        qm = jnp.where(keep, q, jnp.zeros_like(q))
        kT = jnp.concatenate([k_pages[s][0, 0, cols, :] for s in range(n_pages)], axis=1).astype(BF16)
        raw = jnp.dot(qm, kT, preferred_element_type=F32)
        blk_lanes = lambda jj: slice(jj * MOBA_BLOCK, (jj + 1) * MOBA_BLOCK)
        gate = [jnp.sum(raw[:, blk_lanes(jj)], axis=-1, keepdims=True) * (1.0 / MOBA_BLOCK) for jj in range(n_blk)]
        sel = []
        for jj in range(n_blk):
            ahead = jnp.zeros(gate[jj].shape, F32)
            for kk in range(n_blk):
                if kk == jj:
                    continue
                beats = (gate[kk] >= gate[jj]) if kk < jj else (gate[kk] > gate[jj])
                ahead = ahead + jnp.where(beats, 1.0, 0.0)
            sel.append(ahead < float(min(MOBA_TOPK, n_blk)))
        logit = jnp.concatenate(
            [jnp.where(sel[jj], raw[:, blk_lanes(jj)] + past_ref[p, :, blk_lanes(jj)], -jnp.inf)
             for jj in range(n_blk)], axis=1)
        kn = kn_ref[0, :, cols]
        s_new = lax.dot_general(qm, kn, NT, preferred_element_type=F32) + new_ref[p]
        m = jnp.maximum(jnp.max(s_new, axis=-1, keepdims=True), jnp.max(logit, axis=-1, keepdims=True))
        p_new = jnp.exp(s_new - m)
        pexp = jnp.exp(logit - m)
        denom = jnp.sum(p_new, axis=-1, keepdims=True) + jnp.sum(pexp, axis=-1, keepdims=True)
        vT = jnp.concatenate([v_pages[s][0, 0, cols, :] for s in range(n_pages)], axis=1).astype(BF16)
        acc = (jnp.dot(p_new.astype(BF16), vn_ref[0, :, cols], preferred_element_type=F32)
               + lax.dot_general(pexp.astype(BF16), vT, NT, preferred_element_type=F32))
        out = acc / denom
        o_ref[0, :, cols] = jnp.where(lane[:L] < MOBA_DH, out[:L], out[L:2 * L])


def _moba_sample(q16, kn16, vn16, cache_kT, cache_vT, page_table, layer, L):
    db, n_pages = page_table.shape
    past, new = _moba_sample_tables(L, n_pages)
    b3 = lambda b, pt: (b, 0, 0)
    row_spec = pl.BlockSpec((1, SAMPLE_ROWS, MOBA_WIDTH), b3)

    def page_spec(s):
        return pl.BlockSpec((1, 1, MOBA_WIDTH, PAGE_SIZE), lambda b, pt: (layer, pt[b, s], 0, 0))

    in_specs = [row_spec, row_spec, row_spec,
                pl.BlockSpec(past.shape, lambda b, pt: (0, 0, 0)), pl.BlockSpec(new.shape, lambda b, pt: (0, 0, 0))]
    in_specs += [page_spec(s) for s in range(n_pages)] * 2
    grid_spec = pltpu.PrefetchScalarGridSpec(
        num_scalar_prefetch=1, grid=(db,), in_specs=in_specs,
        out_specs=pl.BlockSpec((1, L, MOBA_WIDTH), b3))
    return pl.pallas_call(
        functools.partial(_moba_sample_kernel, L=L, n_pages=n_pages),
        grid_spec=grid_spec,
        out_shape=jax.ShapeDtypeStruct((db, L, MOBA_WIDTH), F32),
        compiler_params=_params("arbitrary"), name="moba_sample",
    )(page_table, q16, kn16, vn16, past, new, *([cache_kT] * n_pages), *([cache_vT] * n_pages))


def _route(x, wrh_ref, wrl_ref, br_ref, group=None):
    tm = x.shape[0]
    hi = x.astype(BF16)
    lo = (x - hi.astype(F32)).astype(BF16)
    logit = (jnp.dot(hi, wrh_ref[...], preferred_element_type=F32)
             + (jnp.dot(hi, wrl_ref[...], preferred_element_type=F32)
                + jnp.dot(lo, wrh_ref[...], preferred_element_type=F32))) + br_ref[...]
    lane = lax.broadcasted_iota(jnp.int32, (tm, ROUTER_LANES), 1)
    neg = -jnp.inf
    gl = jnp.where(lane < N_GROUPS, logit, neg)
    gmax = jnp.max(gl, axis=-1, keepdims=True)
    g_sum = jnp.sum(jnp.exp(gl - gmax), axis=-1, keepdims=True)
    if group is None:
        gidx = jnp.min(jnp.where(gl == gmax, lane, ROUTER_LANES), axis=-1, keepdims=True)
        g_w = 1.0 / g_sum
    else:
        gidx = group
        g_w = jnp.exp(jnp.sum(jnp.where(lane == group, logit, 0.0), axis=-1, keepdims=True) - gmax) / g_sum
    first = N_GROUPS + EXPERTS_PER_GROUP * gidx
    in_group = jnp.logical_and(lane >= first, lane < first + EXPERTS_PER_GROUP)
    el = jnp.where(in_group, logit, neg)
    e1 = jnp.max(el, axis=-1, keepdims=True)
    i1 = jnp.min(jnp.where(el == e1, lane, ROUTER_LANES), axis=-1, keepdims=True)
    el2 = jnp.where(lane == i1, neg, el)
    e2 = jnp.max(el2, axis=-1, keepdims=True)
    i2 = jnp.min(jnp.where(el2 == e2, lane, ROUTER_LANES), axis=-1, keepdims=True)
    t = jnp.exp(e2 - e1)
    return gidx, i1, i2, g_w / (1.0 + t), g_w * t / (1.0 + t), lane


def _expert_lanes(weights, first_lane, tm):
    return jnp.concatenate([jnp.broadcast_to(weights[:, first_lane + e:first_lane + e + 1], (tm, EXPERT_HIDDEN))
                            for e in range(EXPERTS_PER_GROUP)], axis=1)


def _group_experts(xb, cexp, weg, weu, wed):
    hg = jnp.dot(xb, weg, preferred_element_type=F32)
    hu = jnp.dot(xb, weu, preferred_element_type=F32)
    hid = (hg * _sigmoid(hg)) * hu
    return jnp.dot((hid * cexp).astype(BF16), wed, preferred_element_type=F32)


def _merge_kernel(x_ref, or_ref, om_ref, wg_ref, bg_ref, wr_ref, wm_ref, wo_ref, g_ref, b_ref, *refs, with_route):
    x = x_ref[...]
    gates = _sigmoid(jnp.dot(x.astype(BF16), wg_ref[...], preferred_element_type=F32) + bg_ref[...])
    br = jnp.dot(or_ref[...], wr_ref[...], preferred_element_type=F32)
    bm = jnp.dot(om_ref[...], wm_ref[...], preferred_element_type=F32)
    merged = gates[:, :D_MODEL] * br + gates[:, D_MODEL:] * bm
    y = jnp.dot(merged.astype(BF16), wo_ref[...], preferred_element_type=F32)
    x1 = _layer_norm_rows(ALPHA * x + y, g_ref[...], b_ref[...])
    if not with_route:
        refs[0][...] = x1
        return
    wrh_ref, wrl_ref, brt_ref, o_ref, group_ref = refs
    tm = x1.shape[0]
    for c in range(ROW_CHUNKS):
        o_ref[pl.ds(c, tm, stride=ROW_CHUNKS), :] = x1[:, c * LANES:(c + 1) * LANES]
    gidx = _route(x1, wrh_ref, wrl_ref, brt_ref)[0]
    group_ref[...] = jnp.broadcast_to(gidx, group_ref.shape)


def _merge(x, o_r, o_m, wg16, bg, wr16, wm16, wo16, g, b, router=None, *, tm):
    T = o_r.shape[0]
    row = lambda i: (i, 0)
    consts = [wg16, bg, wr16, wm16, wo16, g, b] + list(router or ())
    if router:
        out_shape = [jax.ShapeDtypeStruct((T * ROW_CHUNKS, LANES), F32),
                     jax.ShapeDtypeStruct((T, ROUTER_LANES), jnp.int32)]
        out_specs = [pl.BlockSpec((tm * ROW_CHUNKS, LANES), row), pl.BlockSpec((tm, ROUTER_LANES), row)]
    else:
        out_shape = [jax.ShapeDtypeStruct((T, D_MODEL), F32)]
        out_specs = [pl.BlockSpec((tm, D_MODEL), row)]
    outs = pl.pallas_call(
        functools.partial(_merge_kernel, with_route=bool(router)),
        grid=(T // tm,),
        in_specs=[pl.BlockSpec((tm, D_MODEL), row), pl.BlockSpec((tm, RET_WIDTH), row),
                  pl.BlockSpec((tm, MOBA_WIDTH), row)] + [_const_spec(c.shape) for c in consts],
        out_specs=out_specs, out_shape=out_shape,
        compiler_params=_params("arbitrary"), name="merge",
    )(x, o_r, o_m, *consts)
    return outs if router else outs[0]


def _moe_kernel(x_ref, wrh_ref, wrl_ref, br_ref, weg_ref, weu_ref, wed_ref, g_ref, b_ref, o_ref):
    x = x_ref[...]
    tm = x.shape[0]
    gidx, i1, i2, w1, w2, lane = _route(x, wrh_ref, wrl_ref, br_ref)
    comb = jnp.where(lane == i1, w1, 0.0) + jnp.where(lane == i2, w2, 0.0)
    xb = x.astype(BF16)
    acc = jnp.zeros((tm, D_MODEL), F32)
    for g in range(N_GROUPS):
        cexp = _expert_lanes(comb, N_GROUPS + g * EXPERTS_PER_GROUP, tm)
        acc = acc + _group_experts(xb, cexp, weg_ref[g], weu_ref[g], wed_ref[g])
    o_ref[...] = _layer_norm_rows(ALPHA * x + acc, g_ref[...], b_ref[...])


def _moe(x, wr_hi, wr_lo, br, weg16, weu16, wed16, g, b, *, tm):
    T = x.shape[0]
    row = lambda i: (i, 0)
    consts = [wr_hi, wr_lo, br, weg16, weu16, wed16, g, b]
    return pl.pallas_call(
        _moe_kernel,
        grid=(T // tm,),
        in_specs=[pl.BlockSpec((tm, D_MODEL), row)] + [_const_spec(c.shape) for c in consts],
        out_specs=pl.BlockSpec((tm, D_MODEL), row),
        out_shape=jax.ShapeDtypeStruct((T, D_MODEL), F32),
        compiler_params=_params("arbitrary"), name="moe",
    )(x, *consts)


def _experts_by_group_kernel(w_ref, o_ref, *, side_by_side):
    for e in range(EXPERTS_PER_GROUP):
        cols = slice(e * EXPERT_HIDDEN, (e + 1) * EXPERT_HIDDEN)
        if side_by_side:
            o_ref[0, :, cols] = w_ref[0, e].astype(BF16)
        else:
            o_ref[0, cols, :] = w_ref[0, e].astype(BF16)


def _experts_by_group(w, layer, *, side_by_side):
    return pl.pallas_call(
        functools.partial(_experts_by_group_kernel, side_by_side=side_by_side),
        grid=(N_GROUPS,),
        in_specs=[pl.BlockSpec((1, EXPERTS_PER_GROUP) + w.shape[2:], lambda g: (layer, g, 0, 0))],
        out_specs=pl.BlockSpec((1, D_MODEL, GROUP_HIDDEN) if side_by_side else (1, GROUP_HIDDEN, D_MODEL),
                               lambda g: (g, 0, 0)),
        out_shape=jax.ShapeDtypeStruct((N_GROUPS, D_MODEL, GROUP_HIDDEN) if side_by_side
                                       else (N_GROUPS, GROUP_HIDDEN, D_MODEL), BF16),
        compiler_params=_params("arbitrary"), name="experts_by_group",
    )(w)


def _group_plan(gid, tm):
    T = gid.shape[0]
    groups = jnp.arange(N_GROUPS, dtype=jnp.int32)
    member = (gid[:, None] == groups[None, :]).astype(jnp.int32)
    running = jnp.cumsum(member, axis=0)
    counts = running[-1]
    rank = jnp.sum(running * member, axis=1) - 1
    padded = ((counts + tm - 1) // tm) * tm
    ends = jnp.cumsum(padded)
    starts = ends - padded
    pos = jnp.sum(starts[None, :] * member, axis=1) + rank
    n_tiles = T // tm + N_GROUPS
    rows = jnp.arange(T, dtype=jnp.int32)
    pad_dst = T + jnp.arange(tm, dtype=jnp.int32)
    src = jnp.zeros((n_tiles * tm,), jnp.int32).at[pos].set(rows, unique_indices=True)
    tile_start = jnp.arange(n_tiles, dtype=jnp.int32) * tm
    tile_group = jnp.minimum(jnp.sum((tile_start[:, None] >= ends[None, :]).astype(jnp.int32), axis=1), N_GROUPS - 1)
    in_tile = (tile_group[:, None] == groups[None, :]).astype(jnp.int32)
    fill = jnp.sum(in_tile * (starts + counts)[None, :], axis=1)[:, None] - tile_start[:, None]
    real = (jnp.arange(tm, dtype=jnp.int32)[None, :] < fill).reshape(n_tiles * tm)
    dst = jnp.where(real, src, jnp.tile(pad_dst, n_tiles))
    dst = jnp.concatenate([pad_dst, dst]).reshape(n_tiles + 1, 1, tm)
    return src.reshape(n_tiles, 1, tm), dst, tile_group


def _moe_grouped_kernel(tg_ref, src_now_ref, src_next_ref, dst_prev_ref, dst_now_ref,
                        x_hbm, wrh_ref, wrl_ref, br_ref, weg_ref, weu_ref, wed_ref, g_ref, b_ref, out_hbm,
                        xbuf, obuf, gsem, ssem, *, tm):
    i = pl.program_id(0)
    last = pl.num_programs(0) - 1
    slot = lax.rem(i, 2)
    other = 1 - slot

    def fetch_rows(idx_ref, s, rows=range(tm)):
        for r in rows:
            t = idx_ref[0, 0, r]
            pltpu.make_async_copy(x_hbm.at[pl.ds(pl.multiple_of(t * ROW_CHUNKS, ROW_CHUNKS), ROW_CHUNKS)],
                                  xbuf.at[s, pl.ds(r * ROW_CHUNKS, ROW_CHUNKS)], gsem.at[s]).start(priority=r % 2)

    def fetch_wait(s):
        pltpu.make_async_copy(x_hbm.at[pl.ds(0, tm * ROW_CHUNKS)], xbuf.at[s], gsem.at[s]).wait()

    def write_rows(idx_ref, s, rows=range(tm)):
        for r in rows:
            t = idx_ref[0, 0, r]
            pltpu.make_async_copy(obuf.at[s, pl.ds(r, 1)], out_hbm.at[pl.ds(t, 1)], ssem.at[s]).start(priority=r % 2)

    def write_wait(s):
        pltpu.make_async_copy(obuf.at[s], out_hbm.at[pl.ds(0, tm)], ssem.at[s]).wait()

    @pl.when(i == 0)
    def _():
        obuf[...] = jnp.zeros_like(obuf)
        fetch_rows(src_now_ref, 0)
        fetch_wait(0)

    fetch_rows(src_next_ref, other)
    write_rows(dst_prev_ref, other)
    x = jnp.concatenate([xbuf[slot, pl.ds(c, tm, stride=ROW_CHUNKS), :] for c in range(ROW_CHUNKS)], axis=1)
    group = tg_ref[i]
    _, i1, i2, w1, w2, lane = _route(x, wrh_ref, wrl_ref, br_ref, group=group)
    first = N_GROUPS + EXPERTS_PER_GROUP * group
    cexp = _expert_lanes(jnp.where(lane == i1 - first, w1, 0.0) + jnp.where(lane == i2 - first, w2, 0.0), 0, tm)
    acc = _group_experts(x.astype(BF16), cexp, weg_ref[0], weu_ref[0], wed_ref[0])
    obuf[slot] = _layer_norm_rows(ALPHA * x + acc, g_ref[...], b_ref[...])
    fetch_wait(other)
    write_wait(other)

    @pl.when(i == last)
    def _():
        write_rows(dst_now_ref, slot)
        write_wait(slot)


def _moe_grouped(x, T, src, dst, tile_group, wr_hi, wr_lo, br, weg16, weu16, wed16, g, b, *, tm):
    n_tiles = src.shape[0]
    smem_tile = lambda fn: pl.BlockSpec((1, 1, tm), fn, memory_space=pltpu.SMEM)
    by_group = lambda i, tg: (tg[i], 0, 0)
    w_spec = pl.BlockSpec((1, D_MODEL, GROUP_HIDDEN), by_group)
    fixed = lambda t: pl.BlockSpec(t.shape, lambda i, tg: (0, 0))
    grid_spec = pltpu.PrefetchScalarGridSpec(
        num_scalar_prefetch=1, grid=(n_tiles,),
        in_specs=[smem_tile(lambda i, tg: (i, 0, 0)),
                  smem_tile(lambda i, tg: (jnp.minimum(i + 1, n_tiles - 1), 0, 0)),
                  smem_tile(lambda i, tg: (i, 0, 0)), smem_tile(lambda i, tg: (i + 1, 0, 0)),
                  pl.BlockSpec(memory_space=pl.ANY), fixed(wr_hi), fixed(wr_lo), fixed(br),
                  w_spec, w_spec, pl.BlockSpec((1, GROUP_HIDDEN, D_MODEL), by_group), fixed(g), fixed(b)],
        out_specs=pl.BlockSpec(memory_space=pl.ANY),
        scratch_shapes=[pltpu.VMEM((2, tm * ROW_CHUNKS, LANES), F32), pltpu.VMEM((2, tm, D_MODEL), F32),
                        pltpu.SemaphoreType.DMA((2,)), pltpu.SemaphoreType.DMA((2,))])
    return pl.pallas_call(
        functools.partial(_moe_grouped_kernel, tm=tm),
        grid_spec=grid_spec,
        out_shape=jax.ShapeDtypeStruct((T + tm, D_MODEL), F32),
        compiler_params=_params("arbitrary"), name="moe_grouped",
    )(tile_group, src, src, dst, dst, x, wr_hi, wr_lo, br, weg16, weu16, wed16, g, b)


def _layer_weights(l, w_in, b_merge, gn_g, w_br_ret, w_br_moba, w_out, ln1_g, ln1_b, w_router_group,
                   b_router_group, w_router_expert, b_router_expert, w_exp_gate, w_exp_up, w_exp_down,
                   ln2_g, ln2_b):
    w = w_in[l]
    q0 = 4 * RET_WIDTH
    n_exp = N_GROUPS * EXPERTS_PER_GROUP
    w_r = jnp.concatenate(
        [w_router_group[l], w_router_expert[l].transpose(1, 0, 2).reshape(D_MODEL, n_exp),
         jnp.zeros((D_MODEL, ROUTER_LANES - N_GROUPS - n_exp), F32)], axis=1)
    w_r_hi = w_r.astype(BF16)
    w_r_lo = (w_r - w_r_hi.astype(F32)).astype(BF16)
    b_r = jnp.concatenate([b_router_group[l], b_router_expert[l].reshape(n_exp),
                           jnp.zeros((ROUTER_LANES - N_GROUPS - n_exp,), F32)])[None, :]
    return dict(
        w_proj=w[:, :N_PROJ].astype(BF16),
        w_ret=w[:, :q0].astype(BF16),
        w_mobaT=w[:, q0:N_PROJ].T.reshape(3, MOBA_WIDTH, D_MODEL).astype(BF16),
        w_gate=w[:, N_PROJ:].astype(BF16), b_gate=b_merge[l][None, :],
        gn=gn_g[l][None, :],
        w_br_ret=w_br_ret[l].astype(BF16), w_br_moba=w_br_moba[l].astype(BF16), w_out=w_out[l].astype(BF16),
        ln1_g=ln1_g[l][None, :], ln1_b=ln1_b[l][None, :],
        w_r_hi=w_r_hi, w_r_lo=w_r_lo, b_r=b_r,
        w_eg=_experts_by_group(w_exp_gate, l, side_by_side=True),
        w_eu=_experts_by_group(w_exp_up, l, side_by_side=True),
        w_ed=_experts_by_group(w_exp_down, l, side_by_side=False),
        ln2_g=ln2_g[l][None, :], ln2_b=ln2_b[l][None, :])


def _pad_rows(t, rows):
    return jnp.pad(t, ((0, 0), (0, rows - t.shape[1]), (0, 0)))


def kernel(x_prompt, x_sample, cache_k, cache_v, state_ret, page_table, w_in, b_merge, gn_g, w_br_ret, w_br_moba, w_out, ln1_g, ln1_b, w_router_group, b_router_group, w_router_expert, b_router_expert, w_exp_gate, w_exp_up, w_exp_down, ln2_g, ln2_b):
    B, S, _ = x_prompt.shape
    DB, L, _ = x_sample.shape
    Tp, Ts = B * S, DB * L
    depth = w_in.shape[0]
    n_pool = cache_k.shape[1]
    page_major = lambda c: c.transpose(0, 1, 3, 4, 2).reshape(depth, n_pool, MOBA_WIDTH, PAGE_SIZE)
    cache_kT, cache_vT = page_major(cache_k), page_major(cache_v)
    xp = x_prompt.reshape(Tp, D_MODEL)
    xs = x_sample.reshape(Ts, D_MODEL)
    tm_p = min(512, Tp)
    tm_s = min(256, Ts)
    tm_moe = 256
    outs = [[] for _ in range(3)]
    ssm = None
    kv_all = None
    for l in range(depth):
        W = _layer_weights(l, w_in, b_merge, gn_g, w_br_ret, w_br_moba, w_out, ln1_g, ln1_b, w_router_group,
                           b_router_group, w_router_expert, b_router_expert, w_exp_gate, w_exp_up, w_exp_down,
                           ln2_g, ln2_b)
        (rq, rk, rv, rg, kT_p, vT_p, k16, qT16, vT16, kmean) = _in_proj(
            xp, W["w_ret"], W["w_mobaT"], rows=Tp, batch=B, tm=tm_p, layer=l, depth=depth, kv_all=kv_all)
        kv_all = (kT_p, vT_p)
        o_r, s_p = _ret_prompt(rq, rk, rv, rg, W["gn"], B)
        o_m = _moba_prompt(qT16, k16, vT16, kmean, B)
        router = (W["w_r_hi"], W["w_r_lo"], W["b_r"])
        x1, group = _merge(xp, o_r, o_m, W["w_gate"], W["b_gate"], W["w_br_ret"], W["w_br_moba"], W["w_out"],
                           W["ln1_g"], W["ln1_b"], router=router, tm=tm_p)
        plan = _group_plan(group[:, 0], tm_moe)
        xp = _moe_grouped(x1, Tp, *plan, *router, W["w_eg"], W["w_eu"], W["w_ed"], W["ln2_g"], W["ln2_b"], tm=tm_moe)
        (rq, rk, rv, rg, mq, k_s, v_s) = _in_proj(xs, W["w_proj"], tm=tm_s)
        r3 = lambda t: t.reshape(DB, L, t.shape[-1])
        o_r, ssm = _ret_sample(_pad_rows(r3(rq), SAMPLE_ROWS), _pad_rows(r3(rk), SAMPLE_ROWS),
                               _pad_rows(r3(rv), SAMPLE_ROWS), r3(rg), W["gn"], state_ret, l, ssm)
        mq3 = r3(mq)
        q16 = _pad_rows(jnp.concatenate([mq3, mq3], axis=1), SAMPLE_ROWS)
        o_m = _moba_sample(q16, _pad_rows(r3(k_s).astype(BF16), SAMPLE_ROWS),
                           _pad_rows(r3(v_s).astype(BF16), SAMPLE_ROWS),
                           cache_kT, cache_vT, page_table, l, L)
        x1 = _merge(xs, o_r.reshape(Ts, RET_WIDTH).astype(BF16), o_m.reshape(Ts, MOBA_WIDTH).astype(BF16),
                    W["w_gate"], W["b_gate"], W["w_br_ret"], W["w_br_moba"], W["w_out"],
                    W["ln1_g"], W["ln1_b"], tm=tm_s)
        xs = _moe(x1, W["w_r_hi"], W["w_r_lo"], W["b_r"], W["w_eg"], W["w_eu"], W["w_ed"],
                  W["ln2_g"], W["ln2_b"], tm=tm_s)
        for lst, val in zip(outs, (s_p, k_s.reshape(DB, L, MOBA_HEADS, MOBA_DH),
                                   v_s.reshape(DB, L, MOBA_HEADS, MOBA_DH))):
            lst.append(val)
    sp, ksm, vsm = (jnp.stack(o) for o in outs)
    kTp, vTp = kv_all
    token_major = lambda t: t.reshape(depth, B, MOBA_HEADS, MOBA_DH, S).transpose(0, 1, 4, 2, 3)
    return (xp[:Tp].reshape(B, S, D_MODEL), xs.reshape(DB, L, D_MODEL), token_major(kTp), token_major(vTp), sp,
            ksm, vsm, ssm)
```
